```python
import math
import jax, jax.numpy as jnp
from jax import lax
import numpy as np

D_MODEL = 1024
BATCH = 4
SEQ = 4096
DEPTH = 2

D_MIX = D_MODEL
D_SSM = D_MIX // 2
D_ATTN = D_MIX - D_SSM
SSM_P = 16
SSM_G = D_SSM // SSM_P
SSM_N = 64
HEAD_DIM = 64
N_HEADS = D_ATTN // HEAD_DIM
D_IN = D_SSM + 3 * D_ATTN + N_HEADS
D_FF = 4 * D_MODEL
Q_BLOCK = 128
EPS = 1e-6
DT_MIN = 1e-3
DT_MAX = 1e-1

kernel_name = "hybrid_s5_fox_parallel_heads"


def rms_norm(x, g):
    xf = x.astype(jnp.float32)
    out = xf * lax.rsqrt(jnp.mean(xf * xf, axis=-1, keepdims=True) + EPS)
    return (out * g.astype(jnp.float32)).astype(x.dtype)


def _ssm_combine(left, right):
    ar1, ai1, br1, bi1 = left
    ar2, ai2, br2, bi2 = right
    ar = ar2 * ar1 - ai2 * ai1
    ai = ar2 * ai1 + ai2 * ar1
    br = ar2 * br1 - ai2 * bi1 + br2
    bi = ar2 * bi1 + ai2 * br1 + bi2
    return (ar, ai, br, bi)


def s5_mixer(u, log_dt, lam_re, lam_im, b_re, b_im, c_re, c_im, d_skip, glu_w, glu_b):
    f32 = jnp.float32
    bsz, seq, _ = u.shape
    uf = u.astype(f32).reshape(bsz, seq, SSM_G, SSM_P).transpose(1, 0, 2, 3)
    lam_re = lam_re.astype(f32)
    lam_im = lam_im.astype(f32)
    dt = jnp.exp(log_dt.astype(f32))[:, None]
    mag = jnp.exp(lam_re * dt)
    a_re = mag * jnp.cos(lam_im * dt)
    a_im = mag * jnp.sin(lam_im * dt)
    den = lam_re * lam_re + lam_im * lam_im
    nr = a_re - 1.0
    s_re = (nr * lam_re + a_im * lam_im) / den
    s_im = (a_im * lam_re - nr * lam_im) / den
    b_re = b_re.astype(f32)
    b_im = b_im.astype(f32)
    bb_re = s_re[..., None] * b_re - s_im[..., None] * b_im
    bb_im = s_re[..., None] * b_im + s_im[..., None] * b_re
    bu_re = jnp.einsum('lbgp,gnp->lbgn', uf, bb_re)
    bu_im = jnp.einsum('lbgp,gnp->lbgn', uf, bb_im)
    a_re_l = jnp.broadcast_to(a_re[None, None], (seq, 1, SSM_G, SSM_N))
    a_im_l = jnp.broadcast_to(a_im[None, None], (seq, 1, SSM_G, SSM_N))
    _, _, x_re, x_im = lax.associative_scan(_ssm_combine, (a_re_l, a_im_l, bu_re, bu_im), axis=0)
    y = (jnp.einsum('lbgn,gpn->lbgp', x_re, c_re.astype(f32))
         - jnp.einsum('lbgn,gpn->lbgp', x_im, c_im.astype(f32)))
    y = y + d_skip.astype(f32).reshape(SSM_G, SSM_P) * uf
    y = y.transpose(1, 0, 2, 3).reshape(bsz, seq, D_SSM)
    g = jax.nn.gelu(y)
    out = g * jax.nn.sigmoid(g @ glu_w.astype(f32) + glu_b.astype(f32))
    return out.astype(u.dtype)


def forgetting_attention(q, k, v, f_logit, f_bias):
    bsz, seq, _ = q.shape
    def heads(t):
        return t.reshape(bsz, seq, N_HEADS, HEAD_DIM).transpose(0, 2, 1, 3)
    qh, kh, vh = heads(q), heads(k), heads(v)
    log_f = jax.nn.log_sigmoid(f_logit.astype(jnp.float32) + f_bias.astype(jnp.float32))
    csum = jnp.cumsum(log_f, axis=1).transpose(0, 2, 1)
    scale = 1.0 / math.sqrt(HEAD_DIM)
    outs = []
    for i in range(seq // Q_BLOCK):
        q0 = i * Q_BLOCK
        kend = q0 + Q_BLOCK
        qb = qh[:, :, q0:kend]
        kb = kh[:, :, :kend]
        vb = vh[:, :, :kend]
        s = jnp.einsum('bhqd,bhkd->bhqk', qb, kb).astype(jnp.float32) * scale
        s = s + csum[:, :, q0:kend, None] - csum[:, :, None, :kend]
        mask = (q0 + jnp.arange(Q_BLOCK))[:, None] >= jnp.arange(kend)[None, :]
        s = jnp.where(mask[None, None], s, -jnp.inf)
        p = jax.nn.softmax(s, axis=-1)
        outs.append(jnp.einsum('bhqk,bhkd->bhqd', p.astype(vb.dtype), vb))
    o = jnp.concatenate(outs, axis=2)
    return o.transpose(0, 2, 1, 3).reshape(bsz, seq, D_ATTN)


def sq_relu_mlp(x, w_up, w_down):
    h = jax.nn.relu(x @ w_up)
    return (h * h) @ w_down


def setup_inputs(seed: int = 0) -> dict:
    key = jax.random.key(seed)
    ks = jax.random.split(key, 22)
    f32 = jnp.float32
    nrm = lambda k, s: jax.random.normal(k, s, f32)
    x = nrm(ks[0], (BATCH, SEQ, D_MODEL))
    ln1_g = 1.0 + 0.02 * nrm(ks[1], (DEPTH, D_MODEL))
    w_in = nrm(ks[2], (DEPTH, D_MODEL, D_IN)) * D_MODEL ** -0.5
    gate_scale = jnp.concatenate([jnp.ones((D_IN - N_HEADS,), f32), jnp.full((N_HEADS,), 0.1, f32)])
    w_in = w_in * gate_scale
    ssm_log_dt = jax.random.uniform(ks[3], (DEPTH, SSM_G), f32, math.log(DT_MIN), math.log(DT_MAX))
    ssm_lambda_re = -0.5 + 0.01 * nrm(ks[4], (DEPTH, SSM_G, SSM_N))
    ssm_lambda_im = jnp.broadcast_to(jnp.pi * jnp.arange(SSM_N, dtype=f32), (DEPTH, SSM_G, SSM_N))
    b_scale = (2.0 * SSM_P) ** -0.5
    ssm_b_re = nrm(ks[5], (DEPTH, SSM_G, SSM_N, SSM_P)) * b_scale
    ssm_b_im = nrm(ks[6], (DEPTH, SSM_G, SSM_N, SSM_P)) * b_scale
    c_scale = (2.0 * SSM_N) ** -0.5
    ssm_c_re = nrm(ks[7], (DEPTH, SSM_G, SSM_P, SSM_N)) * c_scale
    ssm_c_im = nrm(ks[8], (DEPTH, SSM_G, SSM_P, SSM_N)) * c_scale
    ssm_d = nrm(ks[9], (DEPTH, D_SSM))
    glu_w = nrm(ks[10], (DEPTH, D_SSM, D_SSM)) * D_SSM ** -0.5
    glu_b = 0.01 * nrm(ks[11], (DEPTH, D_SSM))
    fgate_b = 4.0 + 0.5 * nrm(ks[12], (DEPTH, N_HEADS))
    gn_ssm_g = 1.0 + 0.02 * nrm(ks[13], (DEPTH, D_SSM))
    gn_attn_g = 1.0 + 0.02 * nrm(ks[14], (DEPTH, D_ATTN))
    w_out = nrm(ks[15], (DEPTH, D_MIX, D_MODEL)) * D_MIX ** -0.5
    ln2_g = 1.0 + 0.02 * nrm(ks[16], (DEPTH, D_MODEL))
    w_up = nrm(ks[17], (DEPTH, D_MODEL, D_FF)) * D_MODEL ** -0.5
    w_down = nrm(ks[18], (DEPTH, D_FF, D_MODEL)) * D_FF ** -0.5
    final_g = 1.0 + 0.02 * nrm(ks[19], (D_MODEL,))
    return {"x": x, "ln1_g": ln1_g, "w_in": w_in, "ssm_log_dt": ssm_log_dt,
            "ssm_lambda_re": ssm_lambda_re, "ssm_lambda_im": ssm_lambda_im,
            "ssm_b_re": ssm_b_re, "ssm_b_im": ssm_b_im, "ssm_c_re": ssm_c_re, "ssm_c_im": ssm_c_im,
            "ssm_d": ssm_d, "glu_w": glu_w, "glu_b": glu_b, "fgate_b": fgate_b,
            "gn_ssm_g": gn_ssm_g, "gn_attn_g": gn_attn_g, "w_out": w_out, "ln2_g": ln2_g,
            "w_up": w_up, "w_down": w_down, "final_g": final_g}


def reference(x, ln1_g, w_in, ssm_log_dt, ssm_lambda_re, ssm_lambda_im, ssm_b_re, ssm_b_im,
              ssm_c_re, ssm_c_im, ssm_d, glu_w, glu_b, fgate_b, gn_ssm_g, gn_attn_g, w_out,
              ln2_g, w_up, w_down, final_g):
    h = x
    splits = [D_SSM, D_SSM + D_ATTN, D_SSM + 2 * D_ATTN, D_SSM + 3 * D_ATTN]
    for l in range(DEPTH):
        xn = rms_norm(h, ln1_g[l])
        proj = xn @ w_in[l]
        u, q, k, v, f_logit = jnp.split(proj, splits, axis=-1)
        y_ssm = s5_mixer(u, ssm_log_dt[l], ssm_lambda_re[l], ssm_lambda_im[l], ssm_b_re[l],
                         ssm_b_im[l], ssm_c_re[l], ssm_c_im[l], ssm_d[l], glu_w[l], glu_b[l])
        y_att = forgetting_attention(q, k, v, f_logit, fgate_b[l])
        mixed = jnp.concatenate([rms_norm(y_ssm, gn_ssm_g[l]), rms_norm(y_att, gn_attn_g[l])], axis=-1)
        h = h + mixed @ w_out[l]
        h = h + sq_relu_mlp(rms_norm(h, ln2_g[l]), w_up[l], w_down[l])
    return rms_norm(h, final_g)
```

```python
import functools
import math

import jax
import jax.numpy as jnp
from jax import lax
from jax.experimental import pallas as pl
from jax.experimental.pallas import tpu as pltpu

F32 = jnp.float32
BF16 = jnp.bfloat16

D_MODEL = 1024
BATCH = 4
SEQ = 4096
DEPTH = 2
TOKENS = BATCH * SEQ
D_SSM = 512
D_ATTN = 512
SSM_P = 16
SSM_G = 32
SSM_N = 64
HEAD_DIM = 64
N_HEADS = 8
D_FF = 4096
EPS = 1e-6

LANES = 128
SSM_CHUNK = 16
N_CHUNKS = SEQ // SSM_CHUNK
PAIR_W = 2 * SSM_CHUNK * SSM_P
N_PAIRS = SSM_G // 2
SSM_ROWS = N_CHUNKS * BATCH

TM_IN = 512
TQ = 512
TM_MLP = 512
TF_MLP = 1024
NEG_BIG = -1e30


def _rms(x, g):
    return x * lax.rsqrt(jnp.mean(x * x, axis=-1, keepdims=True) + EPS) * g


def _dot(a, b):
    return jnp.dot(a, b, preferred_element_type=F32)


def _inproj_kernel(x_ref, g_ref, w_ref, wf_ref, fb_ref, u_ref, qkv_ref, lf_ref):
    xn = _rms(x_ref[...], g_ref[...]).astype(BF16)
    u_ref[...] = _dot(xn, w_ref[:, 0:D_SSM])
    for c in range(3):
        lo = D_SSM + c * D_ATTN
        qkv_ref[:, c * D_ATTN:(c + 1) * D_ATTN] = _dot(xn, w_ref[:, lo:lo + D_ATTN]).astype(BF16)
    f = _dot(xn, wf_ref[...]) + fb_ref[...]
    lf_ref[...] = jnp.minimum(f, 0.0) - jnp.log1p(jnp.exp(-jnp.abs(f)))


def _inproj(h, g, w, wf, fb):
    n = TOKENS // TM_IN
    return pl.pallas_call(
        _inproj_kernel,
        grid=(n,),
        in_specs=[
            pl.BlockSpec((TM_IN, D_MODEL), lambda i: (i, 0)),
            pl.BlockSpec((1, D_MODEL), lambda i: (0, 0)),
            pl.BlockSpec((D_MODEL, 4 * D_SSM), lambda i: (0, 0)),
            pl.BlockSpec((D_MODEL, LANES), lambda i: (0, 0)),
            pl.BlockSpec((1, LANES), lambda i: (0, 0)),
        ],
        out_specs=[
            pl.BlockSpec((TM_IN, D_SSM), lambda i: (i, 0)),
            pl.BlockSpec((TM_IN, 3 * D_ATTN), lambda i: (i, 0)),
            pl.BlockSpec((TM_IN, LANES), lambda i: (i, 0)),
        ],
        out_shape=[
            jax.ShapeDtypeStruct((TOKENS, D_SSM), F32),
            jax.ShapeDtypeStruct((TOKENS, 3 * D_ATTN), BF16),
            jax.ShapeDtypeStruct((TOKENS, LANES), F32),
        ],
        compiler_params=pltpu.CompilerParams(
            dimension_semantics=("arbitrary",), vmem_limit_bytes=48 * 1024 * 1024),
        name="inproj",
    )(h, g, w, wf, fb)


CS_BLK = 256


def _cumsum_kernel(x_ref, o_ref):
    r = lax.broadcasted_iota(jnp.int32, (CS_BLK, CS_BLK), 0)
    c = lax.broadcasted_iota(jnp.int32, (CS_BLK, CS_BLK), 1)
    tri = jnp.where(r <= c, 1.0, 0.0).astype(BF16)
    carry = jnp.zeros((x_ref.shape[0], 1), F32)
    for i in range(SEQ // CS_BLK):
        x = x_ref[:, i * CS_BLK:(i + 1) * CS_BLK]
        hi = x.astype(BF16)
        r1 = x - hi.astype(F32)
        mid = r1.astype(BF16)
        low = (r1 - mid.astype(F32)).astype(BF16)
        cs = _dot(hi, tri) + _dot(mid, tri) + _dot(low, tri) + carry
        o_ref[:, i * CS_BLK:(i + 1) * CS_BLK] = cs
        carry = cs[:, CS_BLK - 1:CS_BLK]


def _cumsum(x):
    return pl.pallas_call(
        _cumsum_kernel,
        out_shape=jax.ShapeDtypeStruct(x.shape, F32),
        name="gate_cumsum",
    )(x)


def _ssm_kernel(z_ref, m_ref, wsr_ref, wsi_ref, wcr_ref, wci_ref, a_ref, y_ref,
                sr_ref, si_ref, xr_ref, xi_ref):
    half = PAIR_W // 2
    z = z_ref[0]
    y_ref[0, :, 0:half] = _dot(z[:, 0:half], m_ref[0, 0])
    y_ref[0, :, half:PAIR_W] = _dot(z[:, half:PAIR_W], m_ref[0, 1])
    sr_ref[...] = _dot(z, wsr_ref[0])
    si_ref[...] = _dot(z, wsi_ref[0])

    ar = a_ref[0, 0:1, :]
    ai = a_ref[0, 1:2, :]
    row = lax.broadcasted_iota(jnp.int32, (8, LANES), 0)
    first = row < BATCH

    def step(k, carry):
        cr, ci = carry
        rows = pl.ds(pl.multiple_of(k * 8, 8), 8)
        sr = sr_ref[rows, :]
        si = si_ref[rows, :]
        sr_sw = pltpu.roll(sr, BATCH, 0)
        si_sw = pltpu.roll(si, BATCH, 0)
        sr_a = jnp.where(first, sr, sr_sw)
        si_a = jnp.where(first, si, si_sw)
        sr_b = jnp.where(first, sr_sw, sr)
        si_b = jnp.where(first, si_sw, si)
        br = ar * cr - ai * ci + sr_a
        bi = ar * ci + ai * cr + si_a
        xr_ref[rows, :] = jnp.where(first, cr, br)
        xi_ref[rows, :] = jnp.where(first, ci, bi)
        return (ar * br - ai * bi + sr_b, ar * bi + ai * br + si_b)

    zero = jnp.zeros((8, LANES), F32)
    lax.fori_loop(0, N_CHUNKS // 2, step, (zero, zero))

    y_ref[0] += (_dot(xr_ref[...].astype(BF16), wcr_ref[0])
                 + _dot(xi_ref[...].astype(BF16), wci_ref[0]))


def _ssm(z, m2, wsr, wsi, wcr, wci, a16):
    return pl.pallas_call(
        _ssm_kernel,
        grid=(N_PAIRS,),
        in_specs=[
            pl.BlockSpec((1, SSM_ROWS, PAIR_W), lambda g: (g, 0, 0)),
            pl.BlockSpec((1, 2, PAIR_W // 2, PAIR_W // 2), lambda g: (g, 0, 0, 0)),
            pl.BlockSpec((1, PAIR_W, LANES), lambda g: (g, 0, 0)),
            pl.BlockSpec((1, PAIR_W, LANES), lambda g: (g, 0, 0)),
            pl.BlockSpec((1, LANES, PAIR_W), lambda g: (g, 0, 0)),
            pl.BlockSpec((1, LANES, PAIR_W), lambda g: (g, 0, 0)),
            pl.BlockSpec((1, 2, LANES), lambda g: (g, 0, 0)),
        ],
        out_specs=pl.BlockSpec((1, SSM_ROWS, PAIR_W), lambda g: (g, 0, 0)),
        out_shape=jax.ShapeDtypeStruct((N_PAIRS, SSM_ROWS, PAIR_W), F32),
        scratch_shapes=[pltpu.VMEM((SSM_ROWS, LANES), F32)] * 4,
        compiler_params=pltpu.CompilerParams(dimension_semantics=("arbitrary",)),
        name="ssm_scan",
    )(z, m2, wsr, wsi, wcr, wci, a16)


def _ssm_matrices(log_dt, lam_re, lam_im, b_re, b_im, c_re, c_im, d_skip):
    hp = lax.Precision.HIGHEST
    tc = SSM_CHUNK
    dt = jnp.exp(log_dt)[:, None]
    mag = jnp.exp(lam_re * dt)
    a_re = mag * jnp.cos(lam_im * dt)
    a_im = mag * jnp.sin(lam_im * dt)
    den = lam_re * lam_re + lam_im * lam_im
    nr = a_re - 1.0
    s_re = (nr * lam_re + a_im * lam_im) / den
    s_im = (a_im * lam_re - nr * lam_im) / den
    bb_re = s_re[..., None] * b_re - s_im[..., None] * b_im
    bb_im = s_re[..., None] * b_im + s_im[..., None] * b_re
    tau = jnp.arange(tc + 1, dtype=F32)[None, :, None]
    pmag = jnp.exp((lam_re * dt)[:, None, :] * tau)
    p_re = pmag * jnp.cos((lam_im * dt)[:, None, :] * tau)
    p_im = pmag * jnp.sin((lam_im * dt)[:, None, :] * tau)

    ab_re = p_re[..., None] * bb_re[:, None] - p_im[..., None] * bb_im[:, None]
    ab_im = p_re[..., None] * bb_im[:, None] + p_im[..., None] * bb_re[:, None]
    k = (jnp.einsum('gpn,gtnq->gtpq', c_re, ab_re[:, :tc], precision=hp)
         - jnp.einsum('gpn,gtnq->gtpq', c_im, ab_im[:, :tc], precision=hp))
    s_idx = jnp.arange(tc)[:, None]
    t_idx = jnp.arange(tc)[None, :]
    lag = t_idx - s_idx
    kt = k[:, jnp.clip(lag, 0, tc - 1)]
    kt = jnp.where((lag >= 0)[None, :, :, None, None], kt, 0.0)
    m = kt.transpose(0, 1, 4, 2, 3)
    eye_t = jnp.eye(tc, dtype=F32)[None, :, None, :, None]
    eye_p = jnp.eye(SSM_P, dtype=F32)[None, None, :, None, :]
    m = m + eye_t * eye_p * d_skip.reshape(SSM_G, 1, 1, 1, SSM_P)
    m = m.reshape(N_PAIRS, 2, tc * SSM_P, tc * SSM_P)

    ws_re = ab_re[:, :tc][:, ::-1].transpose(0, 1, 3, 2).reshape(SSM_G, tc * SSM_P, SSM_N)
    ws_im = ab_im[:, :tc][:, ::-1].transpose(0, 1, 3, 2).reshape(SSM_G, tc * SSM_P, SSM_N)
    ca_re = c_re[:, None] * p_re[:, 1:, None, :] - c_im[:, None] * p_im[:, 1:, None, :]
    ca_im = c_re[:, None] * p_im[:, 1:, None, :] + c_im[:, None] * p_re[:, 1:, None, :]
    wc_re = ca_re.transpose(0, 3, 1, 2).reshape(SSM_G, SSM_N, tc * SSM_P)
    wc_im = -ca_im.transpose(0, 3, 1, 2).reshape(SSM_G, SSM_N, tc * SSM_P)

    def pair_rows(w):
        g, r, c = w.shape
        w = w.reshape(g // 2, 2, r, c)
        z = jnp.zeros_like(w[:, 0])
        top = jnp.concatenate([w[:, 0], z], axis=2)
        bot = jnp.concatenate([z, w[:, 1]], axis=2)
        return jnp.concatenate([top, bot], axis=1)

    a16 = jnp.stack([p_re[:, tc].reshape(N_PAIRS, 2 * SSM_N),
                     p_im[:, tc].reshape(N_PAIRS, 2 * SSM_N)], axis=1)
    return (m.astype(BF16), pair_rows(ws_re).astype(BF16), pair_rows(ws_im).astype(BF16),
            pair_rows(wc_re).astype(BF16), pair_rows(wc_im).astype(BF16), a16)


def _attn_kernel(q_ref, k_ref, v_ref, ck_ref, o_ref, acc_ref, m_ref, l_ref):
    p = pl.program_id(1)
    i = pl.program_id(2)
    lane = lax.broadcasted_iota(jnp.int32, (TQ, LANES), 1)
    q2 = q_ref[...]
    nt = (((1,), (1,)), ((), ()))
    rq = lax.broadcasted_iota(jnp.int32, (TQ, TQ), 0)
    rk = lax.broadcasted_iota(jnp.int32, (TQ, TQ), 1)
    causal = rq >= rk
    outs = []
    for hh in range(2):
        in_head = (lane >= hh * HEAD_DIM) & (lane < (hh + 1) * HEAD_DIM)
        qh = jnp.where(in_head, q2, jnp.zeros_like(q2))
        h = 2 * p + hh
        m_ref[...] = jnp.full(m_ref.shape, NEG_BIG, F32)
        l_ref[...] = jnp.zeros(l_ref.shape, F32)
        acc_ref[...] = jnp.zeros(acc_ref.shape, F32)

        def block(j, masked):
            rows = pl.ds(pl.multiple_of(j * TQ, TQ), TQ)
            kb = k_ref[rows, :]
            vb = v_ref[rows, :]
            ck = ck_ref[0, pl.ds(h, 1), rows]
            s = lax.dot_general(qh, kb, nt, preferred_element_type=F32) - ck
            if masked:
                s = jnp.where(causal, s, NEG_BIG)
            m_prev = m_ref[...]
            m_next = jnp.maximum(m_prev, jnp.max(s, axis=1, keepdims=True))
            alpha = jnp.exp(m_prev - m_next)
            pr = jnp.exp(s - m_next)
            l_ref[...] = alpha * l_ref[...] + jnp.sum(pr, axis=1, keepdims=True)
            acc_ref[...] = alpha * acc_ref[...] + _dot(pr.astype(BF16), vb)
            m_ref[...] = m_next

        def body(j, carry):
            block(j, False)
            return carry

        lax.fori_loop(0, i, body, 0)
        block(i, True)
        outs.append(acc_ref[...] / l_ref[...])
    o_ref[...] = jnp.where(lane < HEAD_DIM, outs[0], outs[1]).astype(BF16)


def _attention(qkv, csum):
    nq = SEQ // TQ
    return pl.pallas_call(
        _attn_kernel,
        grid=(BATCH, N_HEADS // 2, nq),
        in_specs=[
            pl.BlockSpec((TQ, LANES), lambda b, p, i: (b * nq + i, p)),
            pl.BlockSpec((SEQ, LANES), lambda b, p, i: (b, 4 + p)),
            pl.BlockSpec((SEQ, LANES), lambda b, p, i: (b, 8 + p)),
            pl.BlockSpec((1, N_HEADS, SEQ), lambda b, p, i: (b, 0, 0)),
        ],
        out_specs=pl.BlockSpec((TQ, LANES), lambda b, p, i: (b * nq + i, p)),
        out_shape=jax.ShapeDtypeStruct((TOKENS, D_ATTN), BF16),
        scratch_shapes=[pltpu.VMEM((TQ, LANES), F32), pltpu.VMEM((TQ, 1), F32),
                        pltpu.VMEM((TQ, 1), F32)],
        compiler_params=pltpu.CompilerParams(
            dimension_semantics=("arbitrary", "arbitrary", "arbitrary"),
            vmem_limit_bytes=48 * 1024 * 1024),
        name="fox_attention",
    )(qkv, qkv, qkv, csum)


def _mlp_kernel(apply_final, h_ref, ys_ref, oa_ref, gw_ref, gb_ref, gs_ref, ga_ref, wo_ref,
                l2_ref, wu_ref, wd_ref, fg_ref, o_ref, xn_ref):
    j = pl.program_id(1)

    @pl.when(j == 0)
    def _():
        g = jax.nn.gelu(ys_ref[...])
        gate = 1.0 / (1.0 + jnp.exp(-(_dot(g.astype(BF16), gw_ref[...]) + gb_ref[...])))
        n_ssm = _rms(g * gate, gs_ref[...]).astype(BF16)
        n_att = _rms(oa_ref[...].astype(F32), ga_ref[...]).astype(BF16)
        h1 = (h_ref[...] + _dot(n_ssm, wo_ref[0:D_SSM, :])
              + _dot(n_att, wo_ref[D_SSM:D_MODEL, :]))
        o_ref[...] = h1
        xn_ref[...] = _rms(h1, l2_ref[...]).astype(BF16)

    a = jnp.maximum(_dot(xn_ref[...], wu_ref[...]), 0.0)
    o_ref[...] += _dot((a * a).astype(BF16), wd_ref[...])

    if apply_final:
        @pl.when(j == pl.num_programs(1) - 1)
        def _():
            o_ref[...] = _rms(o_ref[...], fg_ref[...])


def _mlp(h, y_ssm, o_att, glu_w, glu_b, gn_s, gn_a, w_out, ln2, w_up, w_down, final_g, apply_final):
    nm = TOKENS // TM_MLP
    nf = D_FF // TF_MLP
    row = lambda i, j: (i, 0)
    fixed = lambda i, j: (0, 0)
    return pl.pallas_call(
        functools.partial(_mlp_kernel, apply_final),
        grid=(nm, nf),
        in_specs=[
            pl.BlockSpec((TM_MLP, D_MODEL), row),
            pl.BlockSpec((TM_MLP, D_SSM), row),
            pl.BlockSpec((TM_MLP, D_ATTN), row),
            pl.BlockSpec((D_SSM, D_SSM), fixed),
            pl.BlockSpec((1, D_SSM), fixed),
            pl.BlockSpec((1, D_SSM), fixed),
            pl.BlockSpec((1, D_ATTN), fixed),
            pl.BlockSpec((D_MODEL, D_MODEL), fixed),
            pl.BlockSpec((1, D_MODEL), fixed),
            pl.BlockSpec((D_MODEL, TF_MLP), lambda i, j: (0, j)),
            pl.BlockSpec((TF_MLP, D_MODEL), lambda i, j: (j, 0)),
            pl.BlockSpec((1, D_MODEL), fixed),
        ],
        out_specs=pl.BlockSpec((TM_MLP, D_MODEL), row),
        out_shape=jax.ShapeDtypeStruct((TOKENS, D_MODEL), F32),
        scratch_shapes=[pltpu.VMEM((TM_MLP, D_MODEL), BF16)],
        compiler_params=pltpu.CompilerParams(
            dimension_semantics=("arbitrary", "arbitrary"),
            vmem_limit_bytes=56 * 1024 * 1024),
        name="mixer_out_mlp",
    )(h, y_ssm, o_att, glu_w, glu_b, gn_s, gn_a, w_out, ln2, w_up, w_down, final_g)


def kernel(x, ln1_g, w_in, ssm_log_dt, ssm_lambda_re, ssm_lambda_im, ssm_b_re, ssm_b_im, ssm_c_re, ssm_c_im, ssm_d, glu_w, glu_b, fgate_b, gn_ssm_g, gn_attn_g, w_out, ln2_g, w_up, w_down, final_g):
    h = x.reshape(TOKENS, D_MODEL)
    qscale = 1.0 / math.sqrt(HEAD_DIM)
    for l in range(DEPTH):
        w = w_in[l]
        w_main = jnp.concatenate(
            [w[:, :D_SSM], w[:, D_SSM:D_SSM + D_ATTN] * qscale, w[:, D_SSM + D_ATTN:4 * D_SSM]],
            axis=1).astype(BF16)
        w_gate = jnp.pad(w[:, 4 * D_SSM:], ((0, 0), (0, LANES - N_HEADS))).astype(BF16)
        b_gate = jnp.pad(fgate_b[l], (0, LANES - N_HEADS)).reshape(1, LANES)

        u, qkv, logf = _inproj(h, ln1_g[l].reshape(1, D_MODEL), w_main, w_gate, b_gate)

        logf_t = logf[:, :N_HEADS].reshape(BATCH, SEQ, N_HEADS).transpose(0, 2, 1)
        csum = _cumsum(logf_t.reshape(BATCH * N_HEADS, SEQ)).reshape(BATCH, N_HEADS, SEQ)

        mats = _ssm_matrices(ssm_log_dt[l], ssm_lambda_re[l], ssm_lambda_im[l], ssm_b_re[l],
                             ssm_b_im[l], ssm_c_re[l], ssm_c_im[l], ssm_d[l])
        z = (u.reshape(BATCH, N_CHUNKS, SSM_CHUNK, N_PAIRS, 2, SSM_P)
             .transpose(3, 1, 0, 4, 2, 5).reshape(N_PAIRS, SSM_ROWS, PAIR_W).astype(BF16))
        y = _ssm(z, *mats)
        y_ssm = (y.reshape(N_PAIRS, N_CHUNKS, BATCH, 2, SSM_CHUNK, SSM_P)
                 .transpose(2, 1, 4, 0, 3, 5).reshape(TOKENS, D_SSM))

        o_att = _attention(qkv, csum)

        h = _mlp(h, y_ssm, o_att, glu_w[l].astype(BF16), glu_b[l].reshape(1, D_SSM),
                 gn_ssm_g[l].reshape(1, D_SSM), gn_attn_g[l].reshape(1, D_ATTN),
                 w_out[l].astype(BF16), ln2_g[l].reshape(1, D_MODEL), w_up[l].astype(BF16),
                 w_down[l].astype(BF16), final_g.reshape(1, D_MODEL), l == DEPTH - 1)
    return h.reshape(BATCH, SEQ, D_MODEL)
```

```python
import functools
import math

import jax
import jax.numpy as jnp
from jax import lax
from jax.experimental import pallas as pl
from jax.experimental.pallas import tpu as pltpu

F32 = jnp.float32
BF16 = jnp.bfloat16

D_MODEL = 1024
BATCH = 4
SEQ = 4096
DEPTH = 2
TOKENS = BATCH * SEQ
D_SSM = 512
D_ATTN = 512
SSM_P = 16
SSM_G = 32
SSM_N = 64
HEAD_DIM = 64
N_HEADS = 8
D_FF = 4096
EPS = 1e-6
LOG2E = 1.4426950408889634

LANES = 128
SUBLANES = 8

SSM_CHUNK = SUBLANES
SSM_GB = LANES // SSM_P
SSM_NB = D_SSM // LANES
SSM_W = SSM_CHUNK * LANES
SSM_S = 2 * SSM_GB * SSM_N
SSM_TM = SEQ
SSM_R = SSM_TM // SSM_CHUNK

TM_IN = 512
TQ = TM_IN
TM_MLP = 512
TF_MLP = 1024
CS_BLK = LANES
NEG_BIG = -1e30

NT_DIMS = (((1,), (1,)), ((), ()))
TN_DIMS = (((0,), (0,)), ((), ()))


def _rms(x, g):
    return x * lax.rsqrt(jnp.mean(x * x, axis=-1, keepdims=True) + EPS) * g


def _dot(a, b):
    return jnp.dot(a, b, preferred_element_type=F32)


def _split3(x):
    hi = x.astype(BF16)
    r1 = x - hi.astype(F32)
    mid = r1.astype(BF16)
    low = (r1 - mid.astype(F32)).astype(BF16)
    return hi, mid, low


def _inproj_kernel(x_ref, g_ref, wu_ref, wk_ref, wqt_ref, wvt_ref, wf_ref, fb_ref,
                   u_ref, k_ref, kb_ref, qt_ref, vt_ref, carry_ref):
    i = pl.program_id(0)
    xn = _rms(x_ref[...], g_ref[...]).astype(BF16)
    u_ref[...] = _dot(xn, wu_ref[...])
    k_ref[...] = _dot(xn, wk_ref[...]).astype(BF16)
    qscale = LOG2E / math.sqrt(HEAD_DIM)
    qt = lax.dot_general(wqt_ref[...], xn, NT_DIMS, preferred_element_type=F32)
    qt_ref[0] = (qt * qscale).astype(BF16)
    vt_ref[0] = lax.dot_general(wvt_ref[...], xn, NT_DIMS, preferred_element_type=F32).astype(BF16)

    f = _dot(xn, wf_ref[...]) + fb_ref[...]
    lf = jnp.minimum(f, 0.0) - jnp.log1p(jnp.exp(-jnp.abs(f)))

    @pl.when(i % (SEQ // TM_IN) == 0)
    def _():
        carry_ref[...] = jnp.zeros(carry_ref.shape, F32)

    r = lax.broadcasted_iota(jnp.int32, (CS_BLK, CS_BLK), 0)
    c = lax.broadcasted_iota(jnp.int32, (CS_BLK, CS_BLK), 1)
    tril = jnp.where(c <= r, 1.0, 0.0).astype(BF16)
    lane = lax.broadcasted_iota(jnp.int32, (CS_BLK, LANES), 1)
    carry = carry_ref[...]
    for b in range(TM_IN // CS_BLK):
        hi, mid, low = _split3(lf[b * CS_BLK:(b + 1) * CS_BLK])
        cs = _dot(tril, hi) + _dot(tril, mid) + _dot(tril, low) + carry
        carry = cs[CS_BLK - 1:CS_BLK, :]
        p0, p1, p2 = _split3(cs * (-LOG2E))
        piece = jnp.where(lane < N_HEADS, p0, jnp.where(lane < 2 * N_HEADS, p1, p2))
        kb_ref[b * CS_BLK:(b + 1) * CS_BLK, :] = jnp.where(
            lane < 3 * N_HEADS, piece, jnp.zeros_like(piece))
    carry_ref[...] = carry


def _inproj(h, g, wu, wk, wqt, wvt, wf, fb):
    n = TOKENS // TM_IN
    fixed = lambda i: (0, 0)
    return pl.pallas_call(
        _inproj_kernel,
        grid=(n,),
        in_specs=[
            pl.BlockSpec((TM_IN, D_MODEL), lambda i: (i, 0)),
            pl.BlockSpec((1, D_MODEL), fixed),
            pl.BlockSpec((D_MODEL, D_SSM), fixed),
            pl.BlockSpec((D_MODEL, D_ATTN), fixed),
            pl.BlockSpec((D_ATTN, D_MODEL), fixed),
            pl.BlockSpec((D_ATTN, D_MODEL), fixed),
            pl.BlockSpec((D_MODEL, LANES), fixed),
            pl.BlockSpec((1, LANES), fixed),
        ],
        out_specs=[
            pl.BlockSpec((TM_IN, D_SSM), lambda i: (i, 0)),
            pl.BlockSpec((TM_IN, D_ATTN), lambda i: (i, 0)),
            pl.BlockSpec((TM_IN, LANES), lambda i: (i, 0)),
            pl.BlockSpec((1, D_ATTN, TM_IN), lambda i: (i, 0, 0)),
            pl.BlockSpec((1, D_ATTN, TM_IN), lambda i: (i, 0, 0)),
        ],
        out_shape=[
            jax.ShapeDtypeStruct((TOKENS, D_SSM), F32),
            jax.ShapeDtypeStruct((TOKENS, D_ATTN), BF16),
            jax.ShapeDtypeStruct((TOKENS, LANES), BF16),
            jax.ShapeDtypeStruct((n, D_ATTN, TM_IN), BF16),
            jax.ShapeDtypeStruct((n, D_ATTN, TM_IN), BF16),
        ],
        scratch_shapes=[pltpu.VMEM((1, LANES), F32)],
        compiler_params=pltpu.CompilerParams(
            dimension_semantics=("arbitrary",), vmem_limit_bytes=48 * 1024 * 1024),
        name="inproj",
    )(h, g, wu, wk, wqt, wvt, wf, fb)


def _ssm_kernel(u_ref, m_ref, ws_ref, wc_ref, a_ref, y_ref, s_ref, x_ref):
    half = SSM_S // 2
    z = jnp.concatenate(
        [u_ref[pl.ds(s, SSM_R, stride=SSM_CHUNK), :].astype(BF16) for s in range(SSM_CHUNK)], axis=1)
    s_ref[...] = _dot(z, ws_ref[0])

    ar = a_ref[0, 0:1, :]
    ai = a_ref[0, 1:2, :]

    def step(r, carry):
        cr, ci = carry
        x_ref[pl.ds(r, 1), 0:half] = cr
        x_ref[pl.ds(r, 1), half:SSM_S] = ci
        sr = s_ref[pl.ds(r, 1), 0:half]
        si = s_ref[pl.ds(r, 1), half:SSM_S]
        return (ar * cr - ai * ci + sr, ar * ci + ai * cr + si)

    zero = jnp.zeros((1, half), F32)
    lax.fori_loop(0, SSM_R, step, (zero, zero))

    y = _dot(z, m_ref[0]) + _dot(x_ref[...].astype(BF16), wc_ref[0])
    for t in range(SSM_CHUNK):
        y_ref[pl.ds(t, SSM_R, stride=SSM_CHUNK), :] = y[:, t * LANES:(t + 1) * LANES]


def _ssm(u, m, ws, wc, a8):
    nt = TOKENS // SSM_TM
    return pl.pallas_call(
        _ssm_kernel,
        grid=(SSM_NB, nt),
        in_specs=[
            pl.BlockSpec((SSM_TM, LANES), lambda j, i: (i, j)),
            pl.BlockSpec((1, SSM_W, SSM_W), lambda j, i: (j, 0, 0)),
            pl.BlockSpec((1, SSM_W, SSM_S), lambda j, i: (j, 0, 0)),
            pl.BlockSpec((1, SSM_S, SSM_W), lambda j, i: (j, 0, 0)),
            pl.BlockSpec((1, 2, SSM_S // 2), lambda j, i: (j, 0, 0)),
        ],
        out_specs=pl.BlockSpec((SSM_TM, LANES), lambda j, i: (i, j)),
        out_shape=jax.ShapeDtypeStruct((TOKENS, D_SSM), F32),
        scratch_shapes=[pltpu.VMEM((SSM_R, SSM_S), F32), pltpu.VMEM((SSM_R, SSM_S), F32)],
        compiler_params=pltpu.CompilerParams(
            dimension_semantics=("arbitrary", "arbitrary"), vmem_limit_bytes=48 * 1024 * 1024),
        name="ssm_scan",
    )(u, m, ws, wc, a8)


def _ssm_matrices(log_dt, lam_re, lam_im, b_re, b_im, c_re, c_im, d_skip):
    hp = lax.Precision.HIGHEST
    tc = SSM_CHUNK
    nb, gb = SSM_NB, SSM_GB
    dt = jnp.exp(log_dt)[:, None]
    mag = jnp.exp(lam_re * dt)
    a_re = mag * jnp.cos(lam_im * dt)
    a_im = mag * jnp.sin(lam_im * dt)
    den = lam_re * lam_re + lam_im * lam_im
    nr = a_re - 1.0
    s_re = (nr * lam_re + a_im * lam_im) / den
    s_im = (a_im * lam_re - nr * lam_im) / den
    bb_re = s_re[..., None] * b_re - s_im[..., None] * b_im
    bb_im = s_re[..., None] * b_im + s_im[..., None] * b_re
    tau = jnp.arange(tc + 1, dtype=F32)[None, :, None]
    pmag = jnp.exp((lam_re * dt)[:, None, :] * tau)
    p_re = pmag * jnp.cos((lam_im * dt)[:, None, :] * tau)
    p_im = pmag * jnp.sin((lam_im * dt)[:, None, :] * tau)

    ab_re = p_re[..., None] * bb_re[:, None] - p_im[..., None] * bb_im[:, None]
    ab_im = p_re[..., None] * bb_im[:, None] + p_im[..., None] * bb_re[:, None]
    k = (jnp.einsum('gpn,gtnq->gtpq', c_re, ab_re[:, :tc], precision=hp)
         - jnp.einsum('gpn,gtnq->gtpq', c_im, ab_im[:, :tc], precision=hp))
    lag = jnp.arange(tc)[None, :] - jnp.arange(tc)[:, None]
    kt = k[:, jnp.clip(lag, 0, tc - 1)]
    kt = jnp.where((lag >= 0)[None, :, :, None, None], kt, 0.0)
    eye_t = jnp.eye(tc, dtype=F32)[None, :, :, None, None]
    eye_p = jnp.eye(SSM_P, dtype=F32)[None, None, None, :, :]
    kt = kt + eye_t * eye_p * d_skip.reshape(SSM_G, 1, 1, SSM_P, 1)
    eye_g = jnp.eye(gb, dtype=F32)

    kt = kt.reshape(nb, gb, tc, tc, SSM_P, SSM_P)
    m = jnp.einsum('jgstpq,gh->jsgqthp', kt, eye_g).reshape(nb, SSM_W, SSM_W)

    def state_in(ab):
        w = ab[:, :tc][:, ::-1].reshape(nb, gb, tc, SSM_N, SSM_P)
        return jnp.einsum('jgsnq,gh->jsgqhn', w, eye_g).reshape(nb, SSM_W, gb * SSM_N)
    ws = jnp.concatenate([state_in(ab_re), state_in(ab_im)], axis=2)

    ca_re = c_re[:, None] * p_re[:, 1:, None, :] - c_im[:, None] * p_im[:, 1:, None, :]
    ca_im = c_re[:, None] * p_im[:, 1:, None, :] + c_im[:, None] * p_re[:, 1:, None, :]

    def state_out(ca):
        w = ca.reshape(nb, gb, tc, SSM_P, SSM_N)
        return jnp.einsum('jgtpn,gh->jgnthp', w, eye_g).reshape(nb, gb * SSM_N, SSM_W)
    wc = jnp.concatenate([state_out(ca_re), -state_out(ca_im)], axis=1)

    a8 = jnp.stack([p_re[:, tc].reshape(nb, gb * SSM_N),
                    p_im[:, tc].reshape(nb, gb * SSM_N)], axis=1)
    return m.astype(BF16), ws.astype(BF16), wc.astype(BF16), a8


def _attn_kernel(k_ref, kb_ref, qt_ref, vt_ref, o_ref, rhs_ref, sa_ref, sb_ref, acc_ref, m_ref, l_ref):
    p = pl.program_id(1)
    i = pl.program_id(2)
    qt = qt_ref[0]
    row = lax.broadcasted_iota(jnp.int32, (LANES, TQ), 0)
    for hh in range(2):
        h = 2 * p + hh
        in_head = (row >= hh * HEAD_DIM) & (row < (hh + 1) * HEAD_DIM)
        rhs_ref[hh, 0:LANES, :] = jnp.where(in_head, qt, jnp.zeros_like(qt))
        pick = (row == h) | (row == h + N_HEADS) | (row == h + 2 * N_HEADS)
        rhs_ref[hh, LANES:2 * LANES, :] = jnp.where(pick, 1.0, 0.0).astype(BF16)
    m_ref[...] = jnp.full(m_ref.shape, NEG_BIG, F32)
    l_ref[...] = jnp.zeros(l_ref.shape, F32)
    acc_ref[...] = jnp.zeros(acc_ref.shape, F32)

    key = lax.broadcasted_iota(jnp.int32, (TQ, TQ), 0)
    qry = lax.broadcasted_iota(jnp.int32, (TQ, TQ), 1)
    causal = key <= qry

    def scores(j, s_ref):
        rows = pl.ds(pl.multiple_of(j * TQ, TQ), TQ)
        lhs = jnp.concatenate([k_ref[rows, :], kb_ref[rows, :]], axis=1)
        for hh in range(2):
            s_ref[hh] = _dot(lhs, rhs_ref[hh])

    def update(j, s_ref, masked):
        for hh in range(2):
            st = s_ref[hh]
            if masked:
                st = jnp.where(causal, st, NEG_BIG)
            m_prev = m_ref[hh]
            m_next = jnp.maximum(m_prev, jnp.max(st, axis=0, keepdims=True))
            alpha = jnp.exp2(m_prev - m_next)
            pt = jnp.exp2(st - m_next)
            l_ref[hh] = alpha * l_ref[hh] + jnp.sum(pt, axis=0, keepdims=True)
            vt = vt_ref[j, hh * HEAD_DIM:(hh + 1) * HEAD_DIM, :]
            acc_ref[hh] = alpha * acc_ref[hh] + _dot(vt, pt.astype(BF16))
            m_ref[hh] = m_next

    scores(0, sa_ref)

    def body(jj, carry):
        j = 2 * jj
        scores(j + 1, sb_ref)
        update(j, sa_ref, False)
        scores(j + 2, sa_ref)
        update(j + 1, sb_ref, False)
        return carry

    lax.fori_loop(0, i // 2, body, 0)

    @pl.when(i % 2 == 0)
    def _():
        update(i, sa_ref, True)

    @pl.when(i % 2 == 1)
    def _():
        scores(i, sb_ref)
        update(i - 1, sa_ref, False)
        update(i, sb_ref, True)

    for hh in range(2):
        o_ref[0, hh * HEAD_DIM:(hh + 1) * HEAD_DIM, :] = (acc_ref[hh] / l_ref[hh]).astype(BF16)


def _attention(k, kb, qt, vt):
    nq = SEQ // TQ
    return pl.pallas_call(
        _attn_kernel,
        grid=(BATCH, N_HEADS // 2, nq),
        in_specs=[
            pl.BlockSpec((SEQ, LANES), lambda b, p, i: (b, p)),
            pl.BlockSpec((SEQ, LANES), lambda b, p, i: (b, 0)),
            pl.BlockSpec((1, LANES, TQ), lambda b, p, i: (b * nq + i, p, 0)),
            pl.BlockSpec((nq, LANES, TQ), lambda b, p, i: (b, p, 0)),
        ],
        out_specs=pl.BlockSpec((1, LANES, TQ), lambda b, p, i: (b * nq + i, p, 0)),
        out_shape=jax.ShapeDtypeStruct((TOKENS // TQ, D_ATTN, TQ), BF16),
        scratch_shapes=[pltpu.VMEM((2, 2 * LANES, TQ), BF16),
                        pltpu.VMEM((2, TQ, TQ), F32),
                        pltpu.VMEM((2, TQ, TQ), F32),
                        pltpu.VMEM((2, HEAD_DIM, TQ), F32),
                        pltpu.VMEM((2, 1, TQ), F32),
                        pltpu.VMEM((2, 1, TQ), F32)],
        compiler_params=pltpu.CompilerParams(
            dimension_semantics=("arbitrary", "arbitrary", "arbitrary"),
            vmem_limit_bytes=48 * 1024 * 1024),
        name="fox_attention",
    )(k, kb, qt, vt)


def _mlp_kernel(apply_final, h_ref, ys_ref, oa_ref, gw_ref, gb_ref, gs_ref, ga_ref, wo_ref,
                l2_ref, wu_ref, wd_ref, fg_ref, o_ref, xn_ref):
    j = pl.program_id(1)

    @pl.when(j == 0)
    def _():
        g = jax.nn.gelu(ys_ref[...])
        gate = 1.0 / (1.0 + jnp.exp(-(_dot(g.astype(BF16), gw_ref[...]) + gb_ref[...])))
        n_ssm = _rms(g * gate, gs_ref[...]).astype(BF16)
        oa = oa_ref[0].astype(F32)
        inv = lax.rsqrt(jnp.mean(oa * oa, axis=0, keepdims=True) + EPS)
        n_att = (oa * inv * ga_ref[...]).astype(BF16)
        h1 = (h_ref[...] + _dot(n_ssm, wo_ref[0:D_SSM, :])
              + lax.dot_general(n_att, wo_ref[D_SSM:D_MODEL, :], TN_DIMS, preferred_element_type=F32))
        o_ref[...] = h1
        xn_ref[...] = _rms(h1, l2_ref[...]).astype(BF16)

    a = jnp.maximum(_dot(xn_ref[...], wu_ref[...]), 0.0)
    o_ref[...] += _dot((a * a).astype(BF16), wd_ref[...])

    if apply_final:
        @pl.when(j == pl.num_programs(1) - 1)
        def _():
            o_ref[...] = _rms(o_ref[...], fg_ref[...])


def _mlp(h, y_ssm, o_att, glu_w, glu_b, gn_s, gn_a, w_out, ln2, w_up, w_down, final_g, apply_final):
    nm = TOKENS // TM_MLP
    nf = D_FF // TF_MLP
    row = lambda i, j: (i, 0)
    fixed = lambda i, j: (0, 0)
    return pl.pallas_call(
        functools.partial(_mlp_kernel, apply_final),
        grid=(nm, nf),
        in_specs=[
            pl.BlockSpec((TM_MLP, D_MODEL), row),
            pl.BlockSpec((TM_MLP, D_SSM), row),
            pl.BlockSpec((1, D_ATTN, TM_MLP), lambda i, j: (i, 0, 0)),
            pl.BlockSpec((D_SSM, D_SSM), fixed),
            pl.BlockSpec((1, D_SSM), fixed),
            pl.BlockSpec((1, D_SSM), fixed),
            pl.BlockSpec((D_ATTN, 1), fixed),
            pl.BlockSpec((D_MODEL, D_MODEL), fixed),
            pl.BlockSpec((1, D_MODEL), fixed),
            pl.BlockSpec((D_MODEL, TF_MLP), lambda i, j: (0, j)),
            pl.BlockSpec((TF_MLP, D_MODEL), lambda i, j: (j, 0)),
            pl.BlockSpec((1, D_MODEL), fixed),
        ],
        out_specs=pl.BlockSpec((TM_MLP, D_MODEL), row),
        out_shape=jax.ShapeDtypeStruct((TOKENS, D_MODEL), F32),
        scratch_shapes=[pltpu.VMEM((TM_MLP, D_MODEL), BF16)],
        compiler_params=pltpu.CompilerParams(
            dimension_semantics=("arbitrary", "arbitrary"),
            vmem_limit_bytes=56 * 1024 * 1024),
        name="mixer_out_mlp",
    )(h, y_ssm, o_att, glu_w, glu_b, gn_s, gn_a, w_out, ln2, w_up, w_down, final_g)


def kernel(x, ln1_g, w_in, ssm_log_dt, ssm_lambda_re, ssm_lambda_im, ssm_b_re, ssm_b_im, ssm_c_re, ssm_c_im, ssm_d, glu_w, glu_b, fgate_b, gn_ssm_g, gn_attn_g, w_out, ln2_g, w_up, w_down, final_g):
    assert TM_MLP == TQ
    h = x.reshape(TOKENS, D_MODEL)
    for l in range(DEPTH):
        w = w_in[l]
        w_u = w[:, :D_SSM].astype(BF16)
        w_qt = w[:, D_SSM:D_SSM + D_ATTN].T.astype(BF16)
        w_k = w[:, D_SSM + D_ATTN:D_SSM + 2 * D_ATTN].astype(BF16)
        w_vt = w[:, D_SSM + 2 * D_ATTN:D_SSM + 3 * D_ATTN].T.astype(BF16)
        gate_cols = jnp.concatenate([w[:, 4 * D_SSM:]] * 3, axis=1)
        w_gate = jnp.pad(gate_cols, ((0, 0), (0, LANES - 3 * N_HEADS))).astype(BF16)
        b_gate = jnp.pad(jnp.concatenate([fgate_b[l]] * 3), (0, LANES - 3 * N_HEADS)).reshape(1, LANES)

        u, k, kb, qt, vt = _inproj(h, ln1_g[l].reshape(1, D_MODEL), w_u, w_k, w_qt, w_vt, w_gate, b_gate)

        mats = _ssm_matrices(ssm_log_dt[l], ssm_lambda_re[l], ssm_lambda_im[l], ssm_b_re[l],
                             ssm_b_im[l], ssm_c_re[l], ssm_c_im[l], ssm_d[l])
        y_ssm = _ssm(u, *mats)
        o_att = _attention(k, kb, qt, vt)

        h = _mlp(h, y_ssm, o_att, glu_w[l].astype(BF16), glu_b[l].reshape(1, D_SSM),
                 gn_ssm_g[l].reshape(1, D_SSM), gn_attn_g[l].reshape(D_ATTN, 1),
                 w_out[l].astype(BF16), ln2_g[l].reshape(1, D_MODEL), w_up[l].astype(BF16),
                 w_down[l].astype(BF16), final_g.reshape(1, D_MODEL), l == DEPTH - 1)
    return h.reshape(BATCH, SEQ, D_MODEL)
```

```python
import functools
import math

import jax
import jax.numpy as jnp
from jax import lax
from jax.experimental import pallas as pl
from jax.experimental.pallas import tpu as pltpu

F32 = jnp.float32
BF16 = jnp.bfloat16

D_MODEL = 1024
BATCH = 4
SEQ = 4096
DEPTH = 2
TOKENS = BATCH * SEQ
D_SSM = 512
D_ATTN = 512
SSM_P = 16
SSM_G = 32
SSM_N = 64
HEAD_DIM = 64
N_HEADS = 8
D_FF = 4096
EPS = 1e-6
LOG2E = 1.4426950408889634

LANES = 128
SUBLANES = 8

SSM_CHUNK = SUBLANES
SSM_GB = LANES // SSM_P
SSM_NB = D_SSM // LANES
SSM_W = SSM_CHUNK * LANES
SSM_S = 2 * SSM_GB * SSM_N
SSM_TM = SEQ
SSM_R = SSM_TM // SSM_CHUNK
SSM_SEG = SSM_R // SUBLANES

TM_IN = 512
TQ = TM_IN
TM_MLP = 512
TF_MLP = 1024
CS_BLK = LANES
V_PAD = 16
NEG_BIG = -1e30

NT_DIMS = (((1,), (1,)), ((), ()))
TN_DIMS = (((0,), (0,)), ((), ()))


def _rms(x, g):
    return x * lax.rsqrt(jnp.mean(x * x, axis=-1, keepdims=True) + EPS) * g


def _dot(a, b):
    return jnp.dot(a, b, preferred_element_type=F32)


def _split3(x):
    hi = x.astype(BF16)
    r1 = x - hi.astype(F32)
    mid = r1.astype(BF16)
    low = (r1 - mid.astype(F32)).astype(BF16)
    return hi, mid, low


def _inproj_kernel(x_ref, g_ref, wu_ref, wk_ref, wqt_ref, wvt_ref, wf_ref, fb_ref,
                   u_ref, k_ref, kb_ref, qt_ref, vt_ref, carry_ref):
    i = pl.program_id(0)
    xn = _rms(x_ref[...], g_ref[...]).astype(BF16)
    u_ref[...] = _dot(xn, wu_ref[...])
    k_ref[...] = _dot(xn, wk_ref[...]).astype(BF16)
    qscale = LOG2E / math.sqrt(HEAD_DIM)
    qt = lax.dot_general(wqt_ref[...], xn, NT_DIMS, preferred_element_type=F32)
    qt_ref[0] = (qt * qscale).astype(BF16)
    vt_ref[0] = lax.dot_general(wvt_ref[...], xn, NT_DIMS, preferred_element_type=F32).astype(BF16)

    f = _dot(xn, wf_ref[...]) + fb_ref[...]
    lf = jnp.minimum(f, 0.0) - jnp.log1p(jnp.exp(-jnp.abs(f)))

    @pl.when(i % (SEQ // TM_IN) == 0)
    def _():
        carry_ref[...] = jnp.zeros(carry_ref.shape, F32)

    r = lax.broadcasted_iota(jnp.int32, (CS_BLK, CS_BLK), 0)
    c = lax.broadcasted_iota(jnp.int32, (CS_BLK, CS_BLK), 1)
    tril = jnp.where(c <= r, 1.0, 0.0).astype(BF16)
    lane = lax.broadcasted_iota(jnp.int32, (CS_BLK, LANES), 1)
    carry = carry_ref[...]
    for b in range(TM_IN // CS_BLK):
        hi, mid, low = _split3(lf[b * CS_BLK:(b + 1) * CS_BLK])
        cs = _dot(tril, hi) + _dot(tril, mid) + _dot(tril, low) + carry
        carry = cs[CS_BLK - 1:CS_BLK, :]
        p0, p1, p2 = _split3(cs * (-LOG2E))
        piece = jnp.where(lane < N_HEADS, p0, jnp.where(lane < 2 * N_HEADS, p1, p2))
        kb_ref[b * CS_BLK:(b + 1) * CS_BLK, :] = jnp.where(
            lane < 3 * N_HEADS, piece, jnp.zeros_like(piece))
    carry_ref[...] = carry


def _inproj(h, g, wu, wk, wqt, wvt, wf, fb):
    n = TOKENS // TM_IN
    fixed = lambda i: (0, 0)
    return pl.pallas_call(
        _inproj_kernel,
        grid=(n,),
        in_specs=[
            pl.BlockSpec((TM_IN, D_MODEL), lambda i: (i, 0)),
            pl.BlockSpec((1, D_MODEL), fixed),
            pl.BlockSpec((D_MODEL, D_SSM), fixed),
            pl.BlockSpec((D_MODEL, D_ATTN), fixed),
            pl.BlockSpec((D_ATTN, D_MODEL), fixed),
            pl.BlockSpec((D_ATTN, D_MODEL), fixed),
            pl.BlockSpec((D_MODEL, LANES), fixed),
            pl.BlockSpec((1, LANES), fixed),
        ],
        out_specs=[
            pl.BlockSpec((TM_IN, D_SSM), lambda i: (i, 0)),
            pl.BlockSpec((TM_IN, D_ATTN), lambda i: (i, 0)),
            pl.BlockSpec((TM_IN, LANES), lambda i: (i, 0)),
            pl.BlockSpec((1, D_ATTN, TM_IN), lambda i: (i, 0, 0)),
            pl.BlockSpec((1, D_ATTN, TM_IN), lambda i: (i, 0, 0)),
        ],
        out_shape=[
            jax.ShapeDtypeStruct((TOKENS, D_SSM), F32),
            jax.ShapeDtypeStruct((TOKENS, D_ATTN), BF16),
            jax.ShapeDtypeStruct((TOKENS, LANES), BF16),
            jax.ShapeDtypeStruct((n, D_ATTN, TM_IN), BF16),
            jax.ShapeDtypeStruct((n, D_ATTN, TM_IN), BF16),
        ],
        scratch_shapes=[pltpu.VMEM((1, LANES), F32)],
        compiler_params=pltpu.CompilerParams(
            dimension_semantics=("arbitrary",), vmem_limit_bytes=48 * 1024 * 1024),
        name="inproj",
    )(h, g, wu, wk, wqt, wvt, wf, fb)


def _ssm_kernel(u_ref, m_ref, ws_ref, wct_ref, a_ref, y_ref, s_ref, x_ref, xs_ref):
    nt = SSM_S // 2 // LANES
    tiles = range(nt)
    z = jnp.concatenate(
        [u_ref[pl.ds(s, SSM_R, stride=SSM_CHUNK), :].astype(BF16) for s in range(SSM_CHUNK)], axis=1)
    s = _dot(z, ws_ref[0])
    seg_rows = lambda g: pl.ds(g, SSM_SEG, stride=SUBLANES)
    for c in range(2 * nt):
        for g in range(SUBLANES):
            s_ref[c, seg_rows(g), :] = s[g * SSM_SEG:(g + 1) * SSM_SEG, c * LANES:(c + 1) * LANES]

    lane_tiles = lambda row: [a_ref[0, row:row + 1, c * LANES:(c + 1) * LANES] for c in tiles]
    ar, ai = lane_tiles(0), lane_tiles(1)
    gr, gi = lane_tiles(2), lane_tiles(3)

    def local(k, carry):
        cr, ci = carry
        rows = pl.ds(pl.multiple_of(k * SUBLANES, SUBLANES), SUBLANES)
        nr, ni = [], []
        for c in tiles:
            x_ref[c, rows, :] = cr[c]
            x_ref[nt + c, rows, :] = ci[c]
            nr.append(ar[c] * cr[c] - ai[c] * ci[c] + s_ref[c, rows, :])
            ni.append(ar[c] * ci[c] + ai[c] * cr[c] + s_ref[nt + c, rows, :])
        return tuple(nr), tuple(ni)

    zero = tuple(jnp.zeros((SUBLANES, LANES), F32) for _ in tiles)
    er, ei = lax.fori_loop(0, SSM_SEG, local, (zero, zero))

    for c in tiles:
        cr = jnp.zeros((1, LANES), F32)
        ci = jnp.zeros((1, LANES), F32)
        for g in range(SUBLANES):
            xs_ref[c, g:g + 1, :] = cr
            xs_ref[nt + c, g:g + 1, :] = ci
            cr, ci = (gr[c] * cr - gi[c] * ci + er[c][g:g + 1], gr[c] * ci + gi[c] * cr + ei[c][g:g + 1])
    xr = [xs_ref[c] for c in tiles]
    xi = [xs_ref[nt + c] for c in tiles]

    def fix(k, carry):
        pr, pi = carry
        rows = pl.ds(pl.multiple_of(k * SUBLANES, SUBLANES), SUBLANES)
        nr, ni = [], []
        for c in tiles:
            x_ref[c, rows, :] += pr[c] * xr[c] - pi[c] * xi[c]
            x_ref[nt + c, rows, :] += pr[c] * xi[c] + pi[c] * xr[c]
            nr.append(pr[c] * ar[c] - pi[c] * ai[c])
            ni.append(pr[c] * ai[c] + pi[c] * ar[c])
        return tuple(nr), tuple(ni)

    one = tuple(jnp.ones((1, LANES), F32) for _ in tiles)
    lax.fori_loop(0, SSM_SEG, fix, (one, tuple(jnp.zeros((1, LANES), F32) for _ in tiles)))

    x = jnp.concatenate(
        [jnp.concatenate([x_ref[c, seg_rows(g), :] for g in range(SUBLANES)], axis=0).astype(BF16)
         for c in range(2 * nt)], axis=1)
    y = _dot(z, m_ref[0]) + lax.dot_general(x, wct_ref[0], NT_DIMS, preferred_element_type=F32)
    for t in range(SSM_CHUNK):
        y_ref[pl.ds(t, SSM_R, stride=SSM_CHUNK), :] = y[:, t * LANES:(t + 1) * LANES]


def _ssm(u, m, ws, wct, apow):
    nt = TOKENS // SSM_TM
    return pl.pallas_call(
        _ssm_kernel,
        grid=(SSM_NB, nt),
        in_specs=[
            pl.BlockSpec((SSM_TM, LANES), lambda j, i: (i, j)),
            pl.BlockSpec((1, SSM_W, SSM_W), lambda j, i: (j, 0, 0)),
            pl.BlockSpec((1, SSM_W, SSM_S), lambda j, i: (j, 0, 0)),
            pl.BlockSpec((1, SSM_W, SSM_S), lambda j, i: (j, 0, 0)),
            pl.BlockSpec((1, 4, SSM_S // 2), lambda j, i: (j, 0, 0)),
        ],
        out_specs=pl.BlockSpec((SSM_TM, LANES), lambda j, i: (i, j)),
        out_shape=jax.ShapeDtypeStruct((TOKENS, D_SSM), F32),
        scratch_shapes=[pltpu.VMEM((SSM_S // LANES, SSM_R, LANES), F32),
                        pltpu.VMEM((SSM_S // LANES, SSM_R, LANES), F32),
                        pltpu.VMEM((SSM_S // LANES, SUBLANES, LANES), F32)],
        compiler_params=pltpu.CompilerParams(
            dimension_semantics=("arbitrary", "arbitrary"), vmem_limit_bytes=48 * 1024 * 1024),
        name="ssm_scan",
    )(u, m, ws, wct, apow)


def _split2(x):
    hi = x.astype(BF16)
    return hi, (x - hi.astype(F32)).astype(BF16)


def _dot_nt_f32(a, b):
    ah, al = _split2(a)
    bh, bl = _split2(b)
    d = lambda x, y: lax.dot_general(x, y, NT_DIMS, preferred_element_type=F32)
    return d(ah, bh) + d(ah, bl) + d(al, bh)


def _ssm_prep_kernel(lam_ref, b_ref, c_ref, d_ref, m_ref, ws_ref, wct_ref, a_ref):
    half = SSM_S // 2
    tc = SSM_CHUNK
    ldt, lre, lim = lam_ref[0, 0:1, :], lam_ref[0, 1:2, :], lam_ref[0, 2:3, :]
    dt = jnp.exp(ldt)
    tau = lax.broadcasted_iota(jnp.int32, (2 * SUBLANES, half), 0).astype(F32)
    pmag = jnp.exp(tau * (lre * dt))
    p_re = pmag * jnp.cos(tau * (lim * dt))
    p_im = pmag * jnp.sin(tau * (lim * dt))
    a_re, a_im = p_re[1:2], p_im[1:2]
    den = lre * lre + lim * lim
    nr = a_re - 1.0
    s_re = (nr * lre + a_im * lim) / den
    s_im = (a_im * lre - nr * lim) / den
    b_re, b_im = b_ref[0, 0], b_ref[0, 1]
    bb_re = s_re * b_re - s_im * b_im
    bb_im = s_re * b_im + s_im * b_re
    c_re, c_im = c_ref[0, 0], c_ref[0, 1]
    eye = (lax.broadcasted_iota(jnp.int32, (LANES, LANES), 0)
           == lax.broadcasted_iota(jnp.int32, (LANES, LANES), 1))

    m_ref[...] = jnp.zeros(m_ref.shape, BF16)
    for lag in range(tc):
        pr, pi = p_re[lag:lag + 1], p_im[lag:lag + 1]
        l_re = pr * bb_re - pi * bb_im
        l_im = pr * bb_im + pi * bb_re
        rows = slice((tc - 1 - lag) * LANES, (tc - lag) * LANES)
        ws_ref[0, rows, 0:half] = l_re.astype(BF16)
        ws_ref[0, rows, half:SSM_S] = l_im.astype(BF16)
        x = _dot_nt_f32(l_re, c_re) - _dot_nt_f32(l_im, c_im)
        if lag == 0:
            x = x + jnp.where(eye, d_ref[0], 0.0)
        x = x.astype(BF16)
        for s in range(tc - lag):
            m_ref[0, s * LANES:(s + 1) * LANES, (s + lag) * LANES:(s + lag + 1) * LANES] = x
    for t in range(tc):
        pr, pi = p_re[t + 1:t + 2], p_im[t + 1:t + 2]
        rows = slice(t * LANES, (t + 1) * LANES)
        wct_ref[0, rows, 0:half] = (c_re * pr - c_im * pi).astype(BF16)
        wct_ref[0, rows, half:SSM_S] = (-(c_re * pi + c_im * pr)).astype(BF16)

    qr, qi = p_re[tc:tc + 1], p_im[tc:tc + 1]
    a_ref[0, 0:1, :] = qr
    a_ref[0, 1:2, :] = qi
    for _ in range(SSM_SEG.bit_length() - 1):
        qr, qi = qr * qr - qi * qi, 2.0 * qr * qi
    a_ref[0, 2:3, :] = qr
    a_ref[0, 3:4, :] = qi


def _ssm_prep(lam, bexp, cexp, dexp):
    blk = lambda *shape: pl.BlockSpec((1,) + shape, lambda j: (j,) + (0,) * len(shape))
    big = jax.ShapeDtypeStruct((SSM_NB, SSM_W, SSM_S), BF16)
    return pl.pallas_call(
        _ssm_prep_kernel,
        grid=(SSM_NB,),
        in_specs=[blk(3, SSM_S // 2), blk(2, LANES, SSM_S // 2), blk(2, LANES, SSM_S // 2), blk(1, LANES)],
        out_specs=[blk(SSM_W, SSM_W), blk(SSM_W, SSM_S), blk(SSM_W, SSM_S), blk(4, SSM_S // 2)],
        out_shape=[big, big, big, jax.ShapeDtypeStruct((SSM_NB, 4, SSM_S // 2), F32)],
        compiler_params=pltpu.CompilerParams(
            dimension_semantics=("arbitrary",), vmem_limit_bytes=48 * 1024 * 1024),
        name="ssm_prep",
    )(lam, bexp, cexp, dexp)


def _ssm_operands(log_dt, lam_re, lam_im, b_re, b_im, c_re, c_im, d_skip):
    nb, gb = SSM_NB, SSM_GB
    eye_g = jnp.eye(gb, dtype=F32)
    lam = jnp.stack([jnp.broadcast_to(log_dt[:, None], (SSM_G, SSM_N)), lam_re, lam_im], axis=0)
    lam = lam.reshape(3, nb, gb * SSM_N).transpose(1, 0, 2)

    def expand_b(b):
        w = b.reshape(nb, gb, SSM_N, SSM_P).transpose(0, 1, 3, 2)
        return jnp.einsum('jgqn,gh->jgqhn', w, eye_g).reshape(nb, LANES, gb * SSM_N)

    def expand_c(c):
        w = c.reshape(nb, gb, SSM_P, SSM_N)
        return jnp.einsum('jgpn,gh->jgphn', w, eye_g).reshape(nb, LANES, gb * SSM_N)

    bexp = jnp.stack([expand_b(b_re), expand_b(b_im)], axis=1)
    cexp = jnp.stack([expand_c(c_re), expand_c(c_im)], axis=1)
    return lam, bexp, cexp, d_skip.reshape(nb, 1, LANES)


def _attn_kernel(k_ref, kb_ref, qt_ref, vt_ref, o_ref, rhs_ref, sa_ref, sb_ref, xa_ref, xb_ref,
                 acc_ref, m_ref):
    p = pl.program_id(1)
    i = pl.program_id(2)
    qt = qt_ref[0]
    row = lax.broadcasted_iota(jnp.int32, (LANES, TQ), 0)
    for hh in range(2):
        h = 2 * p + hh
        in_head = (row >= hh * HEAD_DIM) & (row < (hh + 1) * HEAD_DIM)
        rhs_ref[hh, 0:LANES, :] = jnp.where(in_head, qt, jnp.zeros_like(qt))
        pick = (row == h) | (row == h + N_HEADS) | (row == h + 2 * N_HEADS)
        rhs_ref[hh, LANES:2 * LANES, :] = jnp.where(pick, 1.0, 0.0).astype(BF16)
    m_ref[...] = jnp.full(m_ref.shape, NEG_BIG, F32)
    acc_ref[...] = jnp.zeros(acc_ref.shape, F32)

    key = lax.broadcasted_iota(jnp.int32, (TQ, TQ), 0)
    qry = lax.broadcasted_iota(jnp.int32, (TQ, TQ), 1)
    causal = key <= qry
    ones_row = jnp.where(lax.broadcasted_iota(jnp.int32, (V_PAD, TQ), 0) == 0, 1.0, 0.0).astype(BF16)

    def scores(j, s_ref, x_ref):
        rows = pl.ds(pl.multiple_of(j * TQ, TQ), TQ)
        lhs = jnp.concatenate([k_ref[rows, :], kb_ref[rows, :]], axis=1)
        for hh in range(2):
            st = _dot(lhs, rhs_ref[hh])
            s_ref[hh] = st
            x_ref[hh] = jnp.max(st, axis=0, keepdims=True)

    def update(j, s_ref, x_ref, masked):
        for hh in range(2):
            st = s_ref[hh]
            if masked:
                st = jnp.where(causal, st, NEG_BIG)
                blk_max = jnp.max(st, axis=0, keepdims=True)
            else:
                blk_max = x_ref[hh]
            m_prev = m_ref[hh]
            m_next = jnp.maximum(m_prev, blk_max)
            alpha = jnp.exp2(m_prev - m_next)
            pt = jnp.exp2(st - m_next).astype(BF16)
            vt = jnp.concatenate([vt_ref[j, hh * HEAD_DIM:(hh + 1) * HEAD_DIM, :], ones_row], axis=0)
            acc_ref[hh] = alpha * acc_ref[hh] + _dot(vt, pt)
            m_ref[hh] = m_next

    scores(0, sa_ref, xa_ref)

    def body(jj, carry):
        j = 2 * jj
        scores(j + 1, sb_ref, xb_ref)
        update(j, sa_ref, xa_ref, False)
        scores(j + 2, sa_ref, xa_ref)
        update(j + 1, sb_ref, xb_ref, False)
        return carry

    lax.fori_loop(0, i // 2, body, 0)

    @pl.when(i % 2 == 0)
    def _():
        update(i, sa_ref, xa_ref, True)

    @pl.when(i % 2 == 1)
    def _():
        scores(i, sb_ref, xb_ref)
        update(i - 1, sa_ref, xa_ref, False)
        update(i, sb_ref, xb_ref, True)

    for hh in range(2):
        acc = acc_ref[hh]
        o_ref[0, hh * HEAD_DIM:(hh + 1) * HEAD_DIM, :] = (
            acc[0:HEAD_DIM] / acc[HEAD_DIM:HEAD_DIM + 1]).astype(BF16)


def _attention(k, kb, qt, vt):
    nq = SEQ // TQ
    return pl.pallas_call(
        _attn_kernel,
        grid=(BATCH, N_HEADS // 2, nq),
        in_specs=[
            pl.BlockSpec((SEQ, LANES), lambda b, p, i: (b, p)),
            pl.BlockSpec((SEQ, LANES), lambda b, p, i: (b, 0)),
            pl.BlockSpec((1, LANES, TQ), lambda b, p, i: (b * nq + i, p, 0)),
            pl.BlockSpec((nq, LANES, TQ), lambda b, p, i: (b, p, 0)),
        ],
        out_specs=pl.BlockSpec((1, LANES, TQ), lambda b, p, i: (b * nq + i, p, 0)),
        out_shape=jax.ShapeDtypeStruct((TOKENS // TQ, D_ATTN, TQ), BF16),
        scratch_shapes=[pltpu.VMEM((2, 2 * LANES, TQ), BF16),
                        pltpu.VMEM((2, TQ, TQ), F32),
                        pltpu.VMEM((2, TQ, TQ), F32),
                        pltpu.VMEM((2, 1, TQ), F32),
                        pltpu.VMEM((2, 1, TQ), F32),
                        pltpu.VMEM((2, HEAD_DIM + V_PAD, TQ), F32),
                        pltpu.VMEM((2, 1, TQ), F32)],
        compiler_params=pltpu.CompilerParams(
            dimension_semantics=("arbitrary", "arbitrary", "arbitrary"),
            vmem_limit_bytes=48 * 1024 * 1024),
        name="fox_attention",
    )(k, kb, qt, vt)


def _mlp_kernel(apply_final, h_ref, ys_ref, oa_ref, gw_ref, gb_ref, gs_ref, ga_ref, wo_ref,
                l2_ref, wu_ref, wd_ref, fg_ref, o_ref, xn_ref):
    j = pl.program_id(1)

    @pl.when(j == 0)
    def _():
        g = jax.nn.gelu(ys_ref[...])
        gate = 1.0 / (1.0 + jnp.exp(-(_dot(g.astype(BF16), gw_ref[...]) + gb_ref[...])))
        n_ssm = _rms(g * gate, gs_ref[...]).astype(BF16)
        oa = oa_ref[0].astype(F32)
        inv = lax.rsqrt(jnp.mean(oa * oa, axis=0, keepdims=True) + EPS)
        n_att = (oa * inv * ga_ref[...]).astype(BF16)
        h1 = (h_ref[...] + _dot(n_ssm, wo_ref[0:D_SSM, :])
              + lax.dot_general(n_att, wo_ref[D_SSM:D_MODEL, :], TN_DIMS, preferred_element_type=F32))
        o_ref[...] = h1
        xn_ref[...] = _rms(h1, l2_ref[...]).astype(BF16)

    a = jnp.maximum(_dot(xn_ref[...], wu_ref[...]), 0.0)
    o_ref[...] += _dot((a * a).astype(BF16), wd_ref[...])

    if apply_final:
        @pl.when(j == pl.num_programs(1) - 1)
        def _():
            o_ref[...] = _rms(o_ref[...], fg_ref[...])


def _mlp(h, y_ssm, o_att, glu_w, glu_b, gn_s, gn_a, w_out, ln2, w_up, w_down, final_g, apply_final):
    nm = TOKENS // TM_MLP
    nf = D_FF // TF_MLP
    row = lambda i, j: (i, 0)
    fixed = lambda i, j: (0, 0)
    return pl.pallas_call(
        functools.partial(_mlp_kernel, apply_final),
        grid=(nm, nf),
        in_specs=[
            pl.BlockSpec((TM_MLP, D_MODEL), row),
            pl.BlockSpec((TM_MLP, D_SSM), row),
            pl.BlockSpec((1, D_ATTN, TM_MLP), lambda i, j: (i, 0, 0)),
            pl.BlockSpec((D_SSM, D_SSM), fixed),
            pl.BlockSpec((1, D_SSM), fixed),
            pl.BlockSpec((1, D_SSM), fixed),
            pl.BlockSpec((D_ATTN, 1), fixed),
            pl.BlockSpec((D_MODEL, D_MODEL), fixed),
            pl.BlockSpec((1, D_MODEL), fixed),
            pl.BlockSpec((D_MODEL, TF_MLP), lambda i, j: (0, j)),
            pl.BlockSpec((TF_MLP, D_MODEL), lambda i, j: (j, 0)),
            pl.BlockSpec((1, D_MODEL), fixed),
        ],
        out_specs=pl.BlockSpec((TM_MLP, D_MODEL), row),
        out_shape=jax.ShapeDtypeStruct((TOKENS, D_MODEL), F32),
        scratch_shapes=[pltpu.VMEM((TM_MLP, D_MODEL), BF16)],
        compiler_params=pltpu.CompilerParams(
            dimension_semantics=("arbitrary", "arbitrary"),
            vmem_limit_bytes=56 * 1024 * 1024),
        name="mixer_out_mlp",
    )(h, y_ssm, o_att, glu_w, glu_b, gn_s, gn_a, w_out, ln2, w_up, w_down, final_g)


def kernel(x, ln1_g, w_in, ssm_log_dt, ssm_lambda_re, ssm_lambda_im, ssm_b_re, ssm_b_im, ssm_c_re, ssm_c_im, ssm_d, glu_w, glu_b, fgate_b, gn_ssm_g, gn_attn_g, w_out, ln2_g, w_up, w_down, final_g):
    assert TM_MLP == TQ
    h = x.reshape(TOKENS, D_MODEL)
    for l in range(DEPTH):
        w = w_in[l]
        w_u = w[:, :D_SSM].astype(BF16)
        w_qt = w[:, D_SSM:D_SSM + D_ATTN].T.astype(BF16)
        w_k = w[:, D_SSM + D_ATTN:D_SSM + 2 * D_ATTN].astype(BF16)
        w_vt = w[:, D_SSM + 2 * D_ATTN:D_SSM + 3 * D_ATTN].T.astype(BF16)
        gate_cols = jnp.concatenate([w[:, 4 * D_SSM:]] * 3, axis=1)
        w_gate = jnp.pad(gate_cols, ((0, 0), (0, LANES - 3 * N_HEADS))).astype(BF16)
        b_gate = jnp.pad(jnp.concatenate([fgate_b[l]] * 3), (0, LANES - 3 * N_HEADS)).reshape(1, LANES)

        u, k, kb, qt, vt = _inproj(h, ln1_g[l].reshape(1, D_MODEL), w_u, w_k, w_qt, w_vt, w_gate, b_gate)

        mats = _ssm_prep(*_ssm_operands(ssm_log_dt[l], ssm_lambda_re[l], ssm_lambda_im[l], ssm_b_re[l],
                                        ssm_b_im[l], ssm_c_re[l], ssm_c_im[l], ssm_d[l]))
        y_ssm = _ssm(u, *mats)
        o_att = _attention(k, kb, qt, vt)

        h = _mlp(h, y_ssm, o_att, glu_w[l].astype(BF16), glu_b[l].reshape(1, D_SSM),
                 gn_ssm_g[l].reshape(1, D_SSM), gn_attn_g[l].reshape(D_ATTN, 1),
                 w_out[l].astype(BF16), ln2_g[l].reshape(1, D_MODEL), w_up[l].astype(BF16),
                 w_down[l].astype(BF16), final_g.reshape(1, D_MODEL), l == DEPTH - 1)
    return h.reshape(BATCH, SEQ, D_MODEL)
```

```python
import functools
import math

import jax
import jax.numpy as jnp
from jax import lax
from jax.experimental import pallas as pl
from jax.experimental.pallas import tpu as pltpu

F32 = jnp.float32
BF16 = jnp.bfloat16

D_MODEL = 1024
BATCH = 4
SEQ = 4096
DEPTH = 2
TOKENS = BATCH * SEQ
D_SSM = 512
D_ATTN = 512
SSM_P = 16
SSM_G = 32
SSM_N = 64
HEAD_DIM = 64
N_HEADS = 8
D_FF = 4096
EPS = 1e-6
LOG2E = 1.4426950408889634

LANES = 128
SUBLANES = 8

SSM_CHUNK = SUBLANES
SSM_GB = LANES // SSM_P
SSM_NB = D_SSM // LANES
SSM_W = SSM_CHUNK * LANES
SSM_S = 2 * SSM_GB * SSM_N
SSM_TM = SEQ
SSM_R = SSM_TM // SSM_CHUNK
SSM_SEG = SSM_R // SUBLANES

TM_IN = 512
TQ = TM_IN
TM_MLP = 512
TF_MLP = 1024
CS_BLK = LANES
V_PAD = 16
NEG_BIG = -1e30

NT_DIMS = (((1,), (1,)), ((), ()))
TN_DIMS = (((0,), (0,)), ((), ()))


def _rms(x, g):
    return x * lax.rsqrt(jnp.mean(x * x, axis=-1, keepdims=True) + EPS) * g


def _dot(a, b):
    return jnp.dot(a, b, preferred_element_type=F32)


def _split3(x):
    hi = x.astype(BF16)
    r1 = x - hi.astype(F32)
    mid = r1.astype(BF16)
    low = (r1 - mid.astype(F32)).astype(BF16)
    return hi, mid, low


def _inproj_kernel(x_ref, g_ref, wu_ref, wk_ref, wqt_ref, wvt_ref, wf_ref, fb_ref,
                   u_ref, k_ref, kb_ref, qt_ref, vt_ref, carry_ref):
    i = pl.program_id(0)
    xn = _rms(x_ref[...], g_ref[...]).astype(BF16)
    u_ref[...] = _dot(xn, wu_ref[...])
    k_ref[...] = _dot(xn, wk_ref[...]).astype(BF16)
    qscale = LOG2E / math.sqrt(HEAD_DIM)
    qt = lax.dot_general(wqt_ref[...], xn, NT_DIMS, preferred_element_type=F32)
    qt_ref[0] = (qt * qscale).astype(BF16)
    vt_ref[0] = lax.dot_general(wvt_ref[...], xn, NT_DIMS, preferred_element_type=F32).astype(BF16)

    f = _dot(xn, wf_ref[...]) + fb_ref[...]
    lf = jnp.minimum(f, 0.0) - jnp.log1p(jnp.exp(-jnp.abs(f)))

    @pl.when(i % (SEQ // TM_IN) == 0)
    def _():
        carry_ref[...] = jnp.zeros(carry_ref.shape, F32)

    r = lax.broadcasted_iota(jnp.int32, (CS_BLK, CS_BLK), 0)
    c = lax.broadcasted_iota(jnp.int32, (CS_BLK, CS_BLK), 1)
    tril = jnp.where(c <= r, 1.0, 0.0).astype(BF16)
    lane = lax.broadcasted_iota(jnp.int32, (CS_BLK, LANES), 1)
    carry = carry_ref[...]
    for b in range(TM_IN // CS_BLK):
        hi, mid, low = _split3(lf[b * CS_BLK:(b + 1) * CS_BLK])
        cs = _dot(tril, hi) + _dot(tril, mid) + _dot(tril, low) + carry
        carry = cs[CS_BLK - 1:CS_BLK, :]
        p0, p1, p2 = _split3(cs * (-LOG2E))
        piece = jnp.where(lane < N_HEADS, p0, jnp.where(lane < 2 * N_HEADS, p1, p2))
        kb_ref[b * CS_BLK:(b + 1) * CS_BLK, :] = jnp.where(
            lane < 3 * N_HEADS, piece, jnp.zeros_like(piece))
    carry_ref[...] = carry


def _inproj(h, g, wu, wk, wqt, wvt, wf, fb):
    n = TOKENS // TM_IN
    fixed = lambda i: (0, 0)
    return pl.pallas_call(
        _inproj_kernel,
        grid=(n,),
        in_specs=[
            pl.BlockSpec((TM_IN, D_MODEL), lambda i: (i, 0)),
            pl.BlockSpec((1, D_MODEL), fixed),
            pl.BlockSpec((D_MODEL, D_SSM), fixed),
            pl.BlockSpec((D_MODEL, D_ATTN), fixed),
            pl.BlockSpec((D_ATTN, D_MODEL), fixed),
            pl.BlockSpec((D_ATTN, D_MODEL), fixed),
            pl.BlockSpec((D_MODEL, LANES), fixed),
            pl.BlockSpec((1, LANES), fixed),
        ],
        out_specs=[
            pl.BlockSpec((TM_IN, D_SSM), lambda i: (i, 0)),
            pl.BlockSpec((TM_IN, D_ATTN), lambda i: (i, 0)),
            pl.BlockSpec((TM_IN, LANES), lambda i: (i, 0)),
            pl.BlockSpec((1, D_ATTN, TM_IN), lambda i: (i, 0, 0)),
            pl.BlockSpec((1, D_ATTN, TM_IN), lambda i: (i, 0, 0)),
        ],
        out_shape=[
            jax.ShapeDtypeStruct((TOKENS, D_SSM), F32),
            jax.ShapeDtypeStruct((TOKENS, D_ATTN), BF16),
            jax.ShapeDtypeStruct((TOKENS, LANES), BF16),
            jax.ShapeDtypeStruct((n, D_ATTN, TM_IN), BF16),
            jax.ShapeDtypeStruct((n, D_ATTN, TM_IN), BF16),
        ],
        scratch_shapes=[pltpu.VMEM((1, LANES), F32)],
        compiler_params=pltpu.CompilerParams(
            dimension_semantics=("arbitrary",), vmem_limit_bytes=48 * 1024 * 1024),
        name="inproj",
    )(h, g, wu, wk, wqt, wvt, wf, fb)


def _ssm_kernel(u_ref, m_ref, ws_ref, wct_ref, a_ref, y_ref, s_ref, x_ref, xs_ref):
    nt = SSM_S // 2 // LANES
    tiles = range(nt)
    z = jnp.concatenate(
        [u_ref[pl.ds(s, SSM_R, stride=SSM_CHUNK), :].astype(BF16) for s in range(SSM_CHUNK)], axis=1)
    s = _dot(z, ws_ref[0])
    seg_rows = lambda g: pl.ds(g, SSM_SEG, stride=SUBLANES)
    for c in range(2 * nt):
        for g in range(SUBLANES):
            s_ref[c, seg_rows(g), :] = s[g * SSM_SEG:(g + 1) * SSM_SEG, c * LANES:(c + 1) * LANES]

    lane_tiles = lambda row: [a_ref[0, row:row + 1, c * LANES:(c + 1) * LANES] for c in tiles]
    ar, ai = lane_tiles(0), lane_tiles(1)
    gr, gi = lane_tiles(2), lane_tiles(3)

    def local(k, carry):
        cr, ci = carry
        rows = pl.ds(pl.multiple_of(k * SUBLANES, SUBLANES), SUBLANES)
        nr, ni = [], []
        for c in tiles:
            x_ref[c, rows, :] = cr[c]
            x_ref[nt + c, rows, :] = ci[c]
            nr.append(ar[c] * cr[c] - ai[c] * ci[c] + s_ref[c, rows, :])
            ni.append(ar[c] * ci[c] + ai[c] * cr[c] + s_ref[nt + c, rows, :])
        return tuple(nr), tuple(ni)

    zero = tuple(jnp.zeros((SUBLANES, LANES), F32) for _ in tiles)
    er, ei = lax.fori_loop(0, SSM_SEG, local, (zero, zero))

    for c in tiles:
        cr = jnp.zeros((1, LANES), F32)
        ci = jnp.zeros((1, LANES), F32)
        for g in range(SUBLANES):
            xs_ref[c, g:g + 1, :] = cr
            xs_ref[nt + c, g:g + 1, :] = ci
            cr, ci = (gr[c] * cr - gi[c] * ci + er[c][g:g + 1], gr[c] * ci + gi[c] * cr + ei[c][g:g + 1])
    xr = [xs_ref[c] for c in tiles]
    xi = [xs_ref[nt + c] for c in tiles]

    def fix(k, carry):
        pr, pi = carry
        rows = pl.ds(pl.multiple_of(k * SUBLANES, SUBLANES), SUBLANES)
        nr, ni = [], []
        for c in tiles:
            x_ref[c, rows, :] += pr[c] * xr[c] - pi[c] * xi[c]
            x_ref[nt + c, rows, :] += pr[c] * xi[c] + pi[c] * xr[c]
            nr.append(pr[c] * ar[c] - pi[c] * ai[c])
            ni.append(pr[c] * ai[c] + pi[c] * ar[c])
        return tuple(nr), tuple(ni)

    one = tuple(jnp.ones((1, LANES), F32) for _ in tiles)
    lax.fori_loop(0, SSM_SEG, fix, (one, tuple(jnp.zeros((1, LANES), F32) for _ in tiles)))

    x = jnp.concatenate(
        [jnp.concatenate([x_ref[c, seg_rows(g), :] for g in range(SUBLANES)], axis=0).astype(BF16)
         for c in range(2 * nt)], axis=1)
    y = _dot(z, m_ref[0]) + lax.dot_general(x, wct_ref[0], NT_DIMS, preferred_element_type=F32)
    for t in range(SSM_CHUNK):
        y_ref[pl.ds(t, SSM_R, stride=SSM_CHUNK), :] = y[:, t * LANES:(t + 1) * LANES]


def _ssm(u, m, ws, wct, apow):
    nt = TOKENS // SSM_TM
    return pl.pallas_call(
        _ssm_kernel,
        grid=(SSM_NB, nt),
        in_specs=[
            pl.BlockSpec((SSM_TM, LANES), lambda j, i: (i, j)),
            pl.BlockSpec((1, SSM_W, SSM_W), lambda j, i: (j, 0, 0)),
            pl.BlockSpec((1, SSM_W, SSM_S), lambda j, i: (j, 0, 0)),
            pl.BlockSpec((1, SSM_W, SSM_S), lambda j, i: (j, 0, 0)),
            pl.BlockSpec((1, 4, SSM_S // 2), lambda j, i: (j, 0, 0)),
        ],
        out_specs=pl.BlockSpec((SSM_TM, LANES), lambda j, i: (i, j)),
        out_shape=jax.ShapeDtypeStruct((TOKENS, D_SSM), F32),
        scratch_shapes=[pltpu.VMEM((SSM_S // LANES, SSM_R, LANES), F32),
                        pltpu.VMEM((SSM_S // LANES, SSM_R, LANES), F32),
                        pltpu.VMEM((SSM_S // LANES, SUBLANES, LANES), F32)],
        compiler_params=pltpu.CompilerParams(
            dimension_semantics=("arbitrary", "arbitrary"), vmem_limit_bytes=48 * 1024 * 1024),
        name="ssm_scan",
    )(u, m, ws, wct, apow)


def _split2(x):
    hi = x.astype(BF16)
    return hi, (x - hi.astype(F32)).astype(BF16)


def _dot_nt_f32(a, b):
    ah, al = _split2(a)
    bh, bl = _split2(b)
    d = lambda x, y: lax.dot_general(x, y, NT_DIMS, preferred_element_type=F32)
    return d(ah, bh) + d(ah, bl) + d(al, bh)


def _ssm_prep_kernel(lam_ref, b_ref, c_ref, d_ref, m_ref, ws_ref, wct_ref, a_ref):
    half = SSM_S // 2
    tc = SSM_CHUNK
    ldt, lre, lim = lam_ref[0, 0:1, :], lam_ref[0, 1:2, :], lam_ref[0, 2:3, :]
    dt = jnp.exp(ldt)
    tau = lax.broadcasted_iota(jnp.int32, (2 * SUBLANES, half), 0).astype(F32)
    pmag = jnp.exp(tau * (lre * dt))
    p_re = pmag * jnp.cos(tau * (lim * dt))
    p_im = pmag * jnp.sin(tau * (lim * dt))
    a_re, a_im = p_re[1:2], p_im[1:2]
    den = lre * lre + lim * lim
    nr = a_re - 1.0
    s_re = (nr * lre + a_im * lim) / den
    s_im = (a_im * lre - nr * lim) / den
    b_re, b_im = b_ref[0, 0], b_ref[0, 1]
    bb_re = s_re * b_re - s_im * b_im
    bb_im = s_re * b_im + s_im * b_re
    c_re, c_im = c_ref[0, 0], c_ref[0, 1]
    eye = (lax.broadcasted_iota(jnp.int32, (LANES, LANES), 0)
           == lax.broadcasted_iota(jnp.int32, (LANES, LANES), 1))

    m_ref[...] = jnp.zeros(m_ref.shape, BF16)
    for lag in range(tc):
        pr, pi = p_re[lag:lag + 1], p_im[lag:lag + 1]
        l_re = pr * bb_re - pi * bb_im
        l_im = pr * bb_im + pi * bb_re
        rows = slice((tc - 1 - lag) * LANES, (tc - lag) * LANES)
        ws_ref[0, rows, 0:half] = l_re.astype(BF16)
        ws_ref[0, rows, half:SSM_S] = l_im.astype(BF16)
        x = _dot_nt_f32(l_re, c_re) - _dot_nt_f32(l_im, c_im)
        if lag == 0:
            x = x + jnp.where(eye, d_ref[0], 0.0)
        x = x.astype(BF16)
        for s in range(tc - lag):
            m_ref[0, s * LANES:(s + 1) * LANES, (s + lag) * LANES:(s + lag + 1) * LANES] = x
    for t in range(tc):
        pr, pi = p_re[t + 1:t + 2], p_im[t + 1:t + 2]
        rows = slice(t * LANES, (t + 1) * LANES)
        wct_ref[0, rows, 0:half] = (c_re * pr - c_im * pi).astype(BF16)
        wct_ref[0, rows, half:SSM_S] = (-(c_re * pi + c_im * pr)).astype(BF16)

    qr, qi = p_re[tc:tc + 1], p_im[tc:tc + 1]
    a_ref[0, 0:1, :] = qr
    a_ref[0, 1:2, :] = qi
    for _ in range(SSM_SEG.bit_length() - 1):
        qr, qi = qr * qr - qi * qi, 2.0 * qr * qi
    a_ref[0, 2:3, :] = qr
    a_ref[0, 3:4, :] = qi


def _ssm_prep(lam, bexp, cexp, dexp):
    blk = lambda *shape: pl.BlockSpec((1,) + shape, lambda j: (j,) + (0,) * len(shape))
    big = jax.ShapeDtypeStruct((SSM_NB, SSM_W, SSM_S), BF16)
    return pl.pallas_call(
        _ssm_prep_kernel,
        grid=(SSM_NB,),
        in_specs=[blk(3, SSM_S // 2), blk(2, LANES, SSM_S // 2), blk(2, LANES, SSM_S // 2), blk(1, LANES)],
        out_specs=[blk(SSM_W, SSM_W), blk(SSM_W, SSM_S), blk(SSM_W, SSM_S), blk(4, SSM_S // 2)],
        out_shape=[big, big, big, jax.ShapeDtypeStruct((SSM_NB, 4, SSM_S // 2), F32)],
        compiler_params=pltpu.CompilerParams(
            dimension_semantics=("arbitrary",), vmem_limit_bytes=48 * 1024 * 1024),
        name="ssm_prep",
    )(lam, bexp, cexp, dexp)


def _ssm_operands(log_dt, lam_re, lam_im, b_re, b_im, c_re, c_im, d_skip):
    nb, gb = SSM_NB, SSM_GB
    eye_g = jnp.eye(gb, dtype=F32)
    lam = jnp.stack([jnp.broadcast_to(log_dt[:, None], (SSM_G, SSM_N)), lam_re, lam_im], axis=0)
    lam = lam.reshape(3, nb, gb * SSM_N).transpose(1, 0, 2)

    def expand_b(b):
        w = b.reshape(nb, gb, SSM_N, SSM_P).transpose(0, 1, 3, 2)
        return jnp.einsum('jgqn,gh->jgqhn', w, eye_g).reshape(nb, LANES, gb * SSM_N)

    def expand_c(c):
        w = c.reshape(nb, gb, SSM_P, SSM_N)
        return jnp.einsum('jgpn,gh->jgphn', w, eye_g).reshape(nb, LANES, gb * SSM_N)

    bexp = jnp.stack([expand_b(b_re), expand_b(b_im)], axis=1)
    cexp = jnp.stack([expand_c(c_re), expand_c(c_im)], axis=1)
    return lam, bexp, cexp, d_skip.reshape(nb, 1, LANES)


def _attn_kernel(k_ref, kb_ref, qt_ref, vt_ref, o_ref, rhs_ref, sa_ref, sb_ref, xa_ref, xb_ref,
                 acc_ref, m_ref):
    p = pl.program_id(1)
    i = pl.program_id(2)
    qt = qt_ref[0]
    row = lax.broadcasted_iota(jnp.int32, (LANES, TQ), 0)
    for hh in range(2):
        h = 2 * p + hh
        in_head = (row >= hh * HEAD_DIM) & (row < (hh + 1) * HEAD_DIM)
        rhs_ref[hh, 0:LANES, :] = jnp.where(in_head, qt, jnp.zeros_like(qt))
        pick = (row == h) | (row == h + N_HEADS) | (row == h + 2 * N_HEADS)
        rhs_ref[hh, LANES:2 * LANES, :] = jnp.where(pick, 1.0, 0.0).astype(BF16)
    m_ref[...] = jnp.full(m_ref.shape, NEG_BIG, F32)
    acc_ref[...] = jnp.zeros(acc_ref.shape, F32)

    key = lax.broadcasted_iota(jnp.int32, (TQ, TQ), 0)
    qry = lax.broadcasted_iota(jnp.int32, (TQ, TQ), 1)
    causal = key <= qry
    ones_row = jnp.where(lax.broadcasted_iota(jnp.int32, (V_PAD, TQ), 0) == 0, 1.0, 0.0).astype(BF16)

    def scores(j, s_ref, x_ref):
        rows = pl.ds(pl.multiple_of(j * TQ, TQ), TQ)
        lhs = jnp.concatenate([k_ref[rows, :], kb_ref[rows, :]], axis=1)
        for hh in range(2):
            st = _dot(lhs, rhs_ref[hh])
            s_ref[hh] = st
            x_ref[hh] = jnp.max(st, axis=0, keepdims=True)

    def update(j, s_ref, x_ref, masked):
        for hh in range(2):
            st = s_ref[hh]
            if masked:
                st = jnp.where(causal, st, NEG_BIG)
                blk_max = jnp.max(st, axis=0, keepdims=True)
            else:
                blk_max = x_ref[hh]
            m_prev = m_ref[hh]
            m_next = jnp.maximum(m_prev, blk_max)
            alpha = jnp.exp2(m_prev - m_next)
            pt = jnp.exp2(st - m_next).astype(BF16)
            vt = jnp.concatenate([vt_ref[j, hh * HEAD_DIM:(hh + 1) * HEAD_DIM, :], ones_row], axis=0)
            acc_ref[hh] = alpha * acc_ref[hh] + _dot(vt, pt)
            m_ref[hh] = m_next

    scores(0, sa_ref, xa_ref)

    def body(jj, carry):
        j = 2 * jj
        scores(j + 1, sb_ref, xb_ref)
        update(j, sa_ref, xa_ref, False)
        scores(j + 2, sa_ref, xa_ref)
        update(j + 1, sb_ref, xb_ref, False)
        return carry

    lax.fori_loop(0, i // 2, body, 0)

    @pl.when(i % 2 == 0)
    def _():
        update(i, sa_ref, xa_ref, True)

    @pl.when(i % 2 == 1)
    def _():
        scores(i, sb_ref, xb_ref)
        update(i - 1, sa_ref, xa_ref, False)
        update(i, sb_ref, xb_ref, True)

    for hh in range(2):
        acc = acc_ref[hh]
        o_ref[0, hh * HEAD_DIM:(hh + 1) * HEAD_DIM, :] = (
            acc[0:HEAD_DIM] / acc[HEAD_DIM:HEAD_DIM + 1]).astype(BF16)


def _attention(k, kb, qt, vt):
    nq = SEQ // TQ
    return pl.pallas_call(
        _attn_kernel,
        grid=(BATCH, N_HEADS // 2, nq),
        in_specs=[
            pl.BlockSpec((SEQ, LANES), lambda b, p, i: (b, p)),
            pl.BlockSpec((SEQ, LANES), lambda b, p, i: (b, 0)),
            pl.BlockSpec((1, LANES, TQ), lambda b, p, i: (b * nq + i, p, 0)),
            pl.BlockSpec((nq, LANES, TQ), lambda b, p, i: (b, p, 0)),
        ],
        out_specs=pl.BlockSpec((1, LANES, TQ), lambda b, p, i: (b * nq + i, p, 0)),
        out_shape=jax.ShapeDtypeStruct((TOKENS // TQ, D_ATTN, TQ), BF16),
        scratch_shapes=[pltpu.VMEM((2, 2 * LANES, TQ), BF16),
                        pltpu.VMEM((2, TQ, TQ), F32),
                        pltpu.VMEM((2, TQ, TQ), F32),
                        pltpu.VMEM((2, 1, TQ), F32),
                        pltpu.VMEM((2, 1, TQ), F32),
                        pltpu.VMEM((2, HEAD_DIM + V_PAD, TQ), F32),
                        pltpu.VMEM((2, 1, TQ), F32)],
        compiler_params=pltpu.CompilerParams(
            dimension_semantics=("arbitrary", "arbitrary", "arbitrary"),
            vmem_limit_bytes=48 * 1024 * 1024),
        name="fox_attention",
    )(k, kb, qt, vt)


def _mlp_kernel(apply_final, h_ref, ys_ref, oa_ref, gw_ref, gb_ref, gs_ref, ga_ref, wo_ref,
                l2_ref, wu_ref, wd_ref, fg_ref, o_ref):
    g = jax.nn.gelu(ys_ref[...])
    gate = 1.0 / (1.0 + jnp.exp(-(_dot(g.astype(BF16), gw_ref[...]) + gb_ref[...])))
    n_ssm = _rms(g * gate, gs_ref[...]).astype(BF16)
    oa = oa_ref[0].astype(F32)
    inv = lax.rsqrt(jnp.mean(oa * oa, axis=0, keepdims=True) + EPS)
    n_att = (oa * inv * ga_ref[...]).astype(BF16)
    h1 = (h_ref[...] + _dot(n_ssm, wo_ref[0:D_SSM, :])
          + lax.dot_general(n_att, wo_ref[D_SSM:D_MODEL, :], TN_DIMS, preferred_element_type=F32))
    xn = _rms(h1, l2_ref[...]).astype(BF16)
    o_ref[...] = h1
    for c in range(D_FF // TF_MLP):
        cols = slice(c * TF_MLP, (c + 1) * TF_MLP)
        a = jnp.maximum(_dot(xn, wu_ref[:, cols]), 0.0)
        o_ref[...] += _dot((a * a).astype(BF16), wd_ref[cols, :])
    if apply_final:
        o_ref[...] = _rms(o_ref[...], fg_ref[...])


def _mlp(h, y_ssm, o_att, glu_w, glu_b, gn_s, gn_a, w_out, ln2, w_up, w_down, final_g, apply_final):
    nm = TOKENS // TM_MLP
    row = lambda i: (i, 0)
    fixed = lambda *shape: pl.BlockSpec(shape, lambda i: (0, 0), pipeline_mode=pl.Buffered(1))
    return pl.pallas_call(
        functools.partial(_mlp_kernel, apply_final),
        grid=(nm,),
        in_specs=[
            pl.BlockSpec((TM_MLP, D_MODEL), row),
            pl.BlockSpec((TM_MLP, D_SSM), row),
            pl.BlockSpec((1, D_ATTN, TM_MLP), lambda i: (i, 0, 0)),
            fixed(D_SSM, D_SSM),
            fixed(1, D_SSM),
            fixed(1, D_SSM),
            fixed(D_ATTN, 1),
            fixed(D_MODEL, D_MODEL),
            fixed(1, D_MODEL),
            fixed(D_MODEL, D_FF),
            fixed(D_FF, D_MODEL),
            fixed(1, D_MODEL),
        ],
        out_specs=pl.BlockSpec((TM_MLP, D_MODEL), row),
        out_shape=jax.ShapeDtypeStruct((TOKENS, D_MODEL), F32),
        compiler_params=pltpu.CompilerParams(
            dimension_semantics=("arbitrary",), vmem_limit_bytes=56 * 1024 * 1024),
        name="mixer_out_mlp",
    )(h, y_ssm, o_att, glu_w, glu_b, gn_s, gn_a, w_out, ln2, w_up, w_down, final_g)


def kernel(x, ln1_g, w_in, ssm_log_dt, ssm_lambda_re, ssm_lambda_im, ssm_b_re, ssm_b_im, ssm_c_re, ssm_c_im, ssm_d, glu_w, glu_b, fgate_b, gn_ssm_g, gn_attn_g, w_out, ln2_g, w_up, w_down, final_g):
    assert TM_MLP == TQ
    h = x.reshape(TOKENS, D_MODEL)
    for l in range(DEPTH):
        w = w_in[l]
        w_u = w[:, :D_SSM].astype(BF16)
        w_qt = w[:, D_SSM:D_SSM + D_ATTN].T.astype(BF16)
        w_k = w[:, D_SSM + D_ATTN:D_SSM + 2 * D_ATTN].astype(BF16)
        w_vt = w[:, D_SSM + 2 * D_ATTN:D_SSM + 3 * D_ATTN].T.astype(BF16)
        gate_cols = jnp.concatenate([w[:, 4 * D_SSM:]] * 3, axis=1)
        w_gate = jnp.pad(gate_cols, ((0, 0), (0, LANES - 3 * N_HEADS))).astype(BF16)
        b_gate = jnp.pad(jnp.concatenate([fgate_b[l]] * 3), (0, LANES - 3 * N_HEADS)).reshape(1, LANES)

        u, k, kb, qt, vt = _inproj(h, ln1_g[l].reshape(1, D_MODEL), w_u, w_k, w_qt, w_vt, w_gate, b_gate)

        mats = _ssm_prep(*_ssm_operands(ssm_log_dt[l], ssm_lambda_re[l], ssm_lambda_im[l], ssm_b_re[l],
                                        ssm_b_im[l], ssm_c_re[l], ssm_c_im[l], ssm_d[l]))
        y_ssm = _ssm(u, *mats)
        o_att = _attention(k, kb, qt, vt)

        h = _mlp(h, y_ssm, o_att, glu_w[l].astype(BF16), glu_b[l].reshape(1, D_SSM),
                 gn_ssm_g[l].reshape(1, D_SSM), gn_attn_g[l].reshape(D_ATTN, 1),
                 w_out[l].astype(BF16), ln2_g[l].reshape(1, D_MODEL), w_up[l].astype(BF16),
                 w_down[l].astype(BF16), final_g.reshape(1, D_MODEL), l == DEPTH - 1)
    return h.reshape(BATCH, SEQ, D_MODEL)
```

```python
import functools
import math

import jax
import jax.numpy as jnp
from jax import lax
from jax.experimental import pallas as pl
from jax.experimental.pallas import tpu as pltpu

F32 = jnp.float32
BF16 = jnp.bfloat16

D_MODEL = 1024
BATCH = 4
SEQ = 4096
DEPTH = 2
TOKENS = BATCH * SEQ
D_SSM = 512
D_ATTN = 512
SSM_P = 16
SSM_G = 32
SSM_N = 64
HEAD_DIM = 64
N_HEADS = 8
D_FF = 4096
EPS = 1e-6
LOG2E = 1.4426950408889634

LANES = 128
SUBLANES = 8

SSM_CHUNK = SUBLANES
SSM_GB = LANES // SSM_P
SSM_NB = D_SSM // LANES
SSM_W = SSM_CHUNK * LANES
SSM_S = 2 * SSM_GB * SSM_N
SSM_TM = SEQ
SSM_R = SSM_TM // SSM_CHUNK
SSM_SEG = SSM_R // SUBLANES

TM_IN = 1024
TQ = 512
TM_MLP = 512
TF_MLP = 1024
CS_BLK = LANES
V_PAD = 16
NEG_BIG = -1e30

NT_DIMS = (((1,), (1,)), ((), ()))
TN_DIMS = (((0,), (0,)), ((), ()))


def _rms(x, g):
    return x * lax.rsqrt(jnp.mean(x * x, axis=-1, keepdims=True) + EPS) * g


def _dot(a, b):
    return jnp.dot(a, b, preferred_element_type=F32)


def _split3(x):
    hi = x.astype(BF16)
    r1 = x - hi.astype(F32)
    mid = r1.astype(BF16)
    low = (r1 - mid.astype(F32)).astype(BF16)
    return hi, mid, low


def _inproj_kernel(x_ref, g_ref, wu_ref, wk_ref, wqt_ref, wvt_ref, wf_ref, fb_ref,
                   u_ref, k_ref, kb_ref, qt_ref, vt_ref, carry_ref):
    i = pl.program_id(0)
    xn = _rms(x_ref[...], g_ref[...]).astype(BF16)
    u_ref[...] = _dot(xn, wu_ref[...])
    k_ref[...] = _dot(xn, wk_ref[...]).astype(BF16)
    qscale = LOG2E / math.sqrt(HEAD_DIM)
    qt = lax.dot_general(wqt_ref[...], xn, NT_DIMS, preferred_element_type=F32)
    vt = lax.dot_general(wvt_ref[...], xn, NT_DIMS, preferred_element_type=F32)
    for t in range(TM_IN // TQ):
        qt_ref[t] = (qt[:, t * TQ:(t + 1) * TQ] * qscale).astype(BF16)
        vt_ref[t] = vt[:, t * TQ:(t + 1) * TQ].astype(BF16)

    f = _dot(xn, wf_ref[...]) + fb_ref[...]
    lf = jnp.minimum(f, 0.0) - jnp.log1p(jnp.exp(-jnp.abs(f)))

    @pl.when(i % (SEQ // TM_IN) == 0)
    def _():
        carry_ref[...] = jnp.zeros(carry_ref.shape, F32)

    r = lax.broadcasted_iota(jnp.int32, (CS_BLK, CS_BLK), 0)
    c = lax.broadcasted_iota(jnp.int32, (CS_BLK, CS_BLK), 1)
    tril = jnp.where(c <= r, 1.0, 0.0).astype(BF16)
    lane = lax.broadcasted_iota(jnp.int32, (CS_BLK, LANES), 1)
    carry = carry_ref[...]
    for b in range(TM_IN // CS_BLK):
        hi, mid, low = _split3(lf[b * CS_BLK:(b + 1) * CS_BLK])
        cs = _dot(tril, hi) + _dot(tril, mid) + _dot(tril, low) + carry
        carry = cs[CS_BLK - 1:CS_BLK, :]
        p0, p1, p2 = _split3(cs * (-LOG2E))
        piece = jnp.where(lane < N_HEADS, p0, jnp.where(lane < 2 * N_HEADS, p1, p2))
        kb_ref[b * CS_BLK:(b + 1) * CS_BLK, :] = jnp.where(
            lane < 3 * N_HEADS, piece, jnp.zeros_like(piece))
    carry_ref[...] = carry


def _inproj(h, g, wu, wk, wqt, wvt, wf, fb):
    n = TOKENS // TM_IN
    fixed = lambda i: (0, 0)
    return pl.pallas_call(
        _inproj_kernel,
        grid=(n,),
        in_specs=[
            pl.BlockSpec((TM_IN, D_MODEL), lambda i: (i, 0)),
            pl.BlockSpec((1, D_MODEL), fixed),
            pl.BlockSpec((D_MODEL, D_SSM), fixed),
            pl.BlockSpec((D_MODEL, D_ATTN), fixed),
            pl.BlockSpec((D_ATTN, D_MODEL), fixed),
            pl.BlockSpec((D_ATTN, D_MODEL), fixed),
            pl.BlockSpec((D_MODEL, LANES), fixed),
            pl.BlockSpec((1, LANES), fixed),
        ],
        out_specs=[
            pl.BlockSpec((TM_IN, D_SSM), lambda i: (i, 0)),
            pl.BlockSpec((TM_IN, D_ATTN), lambda i: (i, 0)),
            pl.BlockSpec((TM_IN, LANES), lambda i: (i, 0)),
            pl.BlockSpec((TM_IN // TQ, D_ATTN, TQ), lambda i: (i, 0, 0)),
            pl.BlockSpec((TM_IN // TQ, D_ATTN, TQ), lambda i: (i, 0, 0)),
        ],
        out_shape=[
            jax.ShapeDtypeStruct((TOKENS, D_SSM), F32),
            jax.ShapeDtypeStruct((TOKENS, D_ATTN), BF16),
            jax.ShapeDtypeStruct((TOKENS, LANES), BF16),
            jax.ShapeDtypeStruct((TOKENS // TQ, D_ATTN, TQ), BF16),
            jax.ShapeDtypeStruct((TOKENS // TQ, D_ATTN, TQ), BF16),
        ],
        scratch_shapes=[pltpu.VMEM((1, LANES), F32)],
        compiler_params=pltpu.CompilerParams(
            dimension_semantics=("arbitrary",), vmem_limit_bytes=48 * 1024 * 1024),
        name="inproj",
    )(h, g, wu, wk, wqt, wvt, wf, fb)


def _ssm_kernel(u_ref, m_ref, ws_ref, wct_ref, a_ref, y_ref, s_ref, x_ref, xs_ref, yl_ref):
    nt = SSM_S // 2 // LANES
    tiles = range(nt)
    z = jnp.concatenate(
        [u_ref[pl.ds(s, SSM_R, stride=SSM_CHUNK), :].astype(BF16) for s in range(SSM_CHUNK)], axis=1)
    s = _dot(z, ws_ref[0])
    seg_rows = lambda g: pl.ds(g, SSM_SEG, stride=SUBLANES)
    for c in range(2 * nt):
        for g in range(SUBLANES):
            s_ref[c, seg_rows(g), :] = s[g * SSM_SEG:(g + 1) * SSM_SEG, c * LANES:(c + 1) * LANES]
    mxu_n = 2 * LANES
    for c in range(SSM_W // mxu_n):
        hi = (c + 1) * mxu_n
        yl_ref[:, c * mxu_n:hi] = _dot(z[:, 0:hi], m_ref[0, 0:hi, c * mxu_n:hi])

    lane_tiles = lambda row: [a_ref[0, row:row + 1, c * LANES:(c + 1) * LANES] for c in tiles]
    full = lambda rows: [jnp.broadcast_to(r, (SUBLANES, LANES)) for r in rows]
    ar, ai = full(lane_tiles(0)), full(lane_tiles(1))
    gr, gi = lane_tiles(2), lane_tiles(3)

    def local(k, carry):
        cr, ci = carry
        rows = pl.ds(pl.multiple_of(k * SUBLANES, SUBLANES), SUBLANES)
        nr, ni = [], []
        for c in tiles:
            x_ref[c, rows, :] = cr[c]
            x_ref[nt + c, rows, :] = ci[c]
            nr.append(ar[c] * cr[c] - ai[c] * ci[c] + s_ref[c, rows, :])
            ni.append(ar[c] * ci[c] + ai[c] * cr[c] + s_ref[nt + c, rows, :])
        return tuple(nr), tuple(ni)

    zero = tuple(jnp.zeros((SUBLANES, LANES), F32) for _ in tiles)
    er, ei = lax.fori_loop(0, SSM_SEG, local, (zero, zero))

    for c in tiles:
        cr = jnp.zeros((1, LANES), F32)
        ci = jnp.zeros((1, LANES), F32)
        for g in range(SUBLANES):
            xs_ref[c, g:g + 1, :] = cr
            xs_ref[nt + c, g:g + 1, :] = ci
            cr, ci = (gr[c] * cr - gi[c] * ci + er[c][g:g + 1], gr[c] * ci + gi[c] * cr + ei[c][g:g + 1])
    xr = [xs_ref[c] for c in tiles]
    xi = [xs_ref[nt + c] for c in tiles]

    def fix(k, carry):
        pr, pi = carry
        rows = pl.ds(pl.multiple_of(k * SUBLANES, SUBLANES), SUBLANES)
        nr, ni = [], []
        for c in tiles:
            x_ref[c, rows, :] += pr[c] * xr[c] - pi[c] * xi[c]
            x_ref[nt + c, rows, :] += pr[c] * xi[c] + pi[c] * xr[c]
            nr.append(pr[c] * ar[c] - pi[c] * ai[c])
            ni.append(pr[c] * ai[c] + pi[c] * ar[c])
        return tuple(nr), tuple(ni)

    one = tuple(jnp.ones((SUBLANES, LANES), F32) for _ in tiles)
    lax.fori_loop(0, SSM_SEG, fix, (one, zero))

    x = jnp.concatenate(
        [jnp.concatenate([x_ref[c, seg_rows(g), :] for g in range(SUBLANES)], axis=0).astype(BF16)
         for c in range(2 * nt)], axis=1)
    y = yl_ref[...] + lax.dot_general(x, wct_ref[0], NT_DIMS, preferred_element_type=F32)
    for t in range(SSM_CHUNK):
        y_ref[pl.ds(t, SSM_R, stride=SSM_CHUNK), :] = y[:, t * LANES:(t + 1) * LANES]


def _ssm(u, m, ws, wct, apow):
    nt = TOKENS // SSM_TM
    return pl.pallas_call(
        _ssm_kernel,
        grid=(SSM_NB, nt),
        in_specs=[
            pl.BlockSpec((SSM_TM, LANES), lambda j, i: (i, j)),
            pl.BlockSpec((1, SSM_W, SSM_W), lambda j, i: (j, 0, 0)),
            pl.BlockSpec((1, SSM_W, SSM_S), lambda j, i: (j, 0, 0)),
            pl.BlockSpec((1, SSM_W, SSM_S), lambda j, i: (j, 0, 0)),
            pl.BlockSpec((1, 4, SSM_S // 2), lambda j, i: (j, 0, 0)),
        ],
        out_specs=pl.BlockSpec((SSM_TM, LANES), lambda j, i: (i, j)),
        out_shape=jax.ShapeDtypeStruct((TOKENS, D_SSM), F32),
        scratch_shapes=[pltpu.VMEM((SSM_S // LANES, SSM_R, LANES), F32),
                        pltpu.VMEM((SSM_S // LANES, SSM_R, LANES), F32),
                        pltpu.VMEM((SSM_S // LANES, SUBLANES, LANES), F32),
                        pltpu.VMEM((SSM_R, SSM_W), F32)],
        compiler_params=pltpu.CompilerParams(
            dimension_semantics=("arbitrary", "arbitrary"), vmem_limit_bytes=48 * 1024 * 1024),
        name="ssm_scan",
    )(u, m, ws, wct, apow)


def _split2(x):
    hi = x.astype(BF16)
    return hi, (x - hi.astype(F32)).astype(BF16)


def _dot_nt_f32(a, b):
    ah, al = _split2(a)
    bh, bl = _split2(b)
    d = lambda x, y: lax.dot_general(x, y, NT_DIMS, preferred_element_type=F32)
    return d(ah, bh) + d(ah, bl) + d(al, bh)


def _ssm_prep_kernel(lam_ref, b_ref, c_ref, d_ref, m_ref, ws_ref, wct_ref, a_ref):
    half = SSM_S // 2
    tc = SSM_CHUNK
    ldt, lre, lim = lam_ref[0, 0:1, :], lam_ref[0, 1:2, :], lam_ref[0, 2:3, :]
    dt = jnp.exp(ldt)
    tau = lax.broadcasted_iota(jnp.int32, (2 * SUBLANES, half), 0).astype(F32)
    pmag = jnp.exp(tau * (lre * dt))
    p_re = pmag * jnp.cos(tau * (lim * dt))
    p_im = pmag * jnp.sin(tau * (lim * dt))
    a_re, a_im = p_re[1:2], p_im[1:2]
    den = lre * lre + lim * lim
    nr = a_re - 1.0
    s_re = (nr * lre + a_im * lim) / den
    s_im = (a_im * lre - nr * lim) / den
    b_re, b_im = b_ref[0, 0], b_ref[0, 1]
    bb_re = s_re * b_re - s_im * b_im
    bb_im = s_re * b_im + s_im * b_re
    c_re, c_im = c_ref[0, 0], c_ref[0, 1]
    eye = (lax.broadcasted_iota(jnp.int32, (LANES, LANES), 0)
           == lax.broadcasted_iota(jnp.int32, (LANES, LANES), 1))

    m_ref[...] = jnp.zeros(m_ref.shape, BF16)
    for lag in range(tc):
        pr, pi = p_re[lag:lag + 1], p_im[lag:lag + 1]
        l_re = pr * bb_re - pi * bb_im
        l_im = pr * bb_im + pi * bb_re
        rows = slice((tc - 1 - lag) * LANES, (tc - lag) * LANES)
        ws_ref[0, rows, 0:half] = l_re.astype(BF16)
        ws_ref[0, rows, half:SSM_S] = l_im.astype(BF16)
        x = _dot_nt_f32(l_re, c_re) - _dot_nt_f32(l_im, c_im)
        if lag == 0:
            x = x + jnp.where(eye, d_ref[0], 0.0)
        x = x.astype(BF16)
        for s in range(tc - lag):
            m_ref[0, s * LANES:(s + 1) * LANES, (s + lag) * LANES:(s + lag + 1) * LANES] = x
    for t in range(tc):
        pr, pi = p_re[t + 1:t + 2], p_im[t + 1:t + 2]
        rows = slice(t * LANES, (t + 1) * LANES)
        wct_ref[0, rows, 0:half] = (c_re * pr - c_im * pi).astype(BF16)
        wct_ref[0, rows, half:SSM_S] = (-(c_re * pi + c_im * pr)).astype(BF16)

    qr, qi = p_re[tc:tc + 1], p_im[tc:tc + 1]
    a_ref[0, 0:1, :] = qr
    a_ref[0, 1:2, :] = qi
    for _ in range(SSM_SEG.bit_length() - 1):
        qr, qi = qr * qr - qi * qi, 2.0 * qr * qi
    a_ref[0, 2:3, :] = qr
    a_ref[0, 3:4, :] = qi


def _ssm_prep(lam, bexp, cexp, dexp):
    blk = lambda *shape: pl.BlockSpec((1,) + shape, lambda j: (j,) + (0,) * len(shape))
    big = jax.ShapeDtypeStruct((SSM_NB, SSM_W, SSM_S), BF16)
    return pl.pallas_call(
        _ssm_prep_kernel,
        grid=(SSM_NB,),
        in_specs=[blk(3, SSM_S // 2), blk(2, LANES, SSM_S // 2), blk(2, LANES, SSM_S // 2), blk(1, LANES)],
        out_specs=[blk(SSM_W, SSM_W), blk(SSM_W, SSM_S), blk(SSM_W, SSM_S), blk(4, SSM_S // 2)],
        out_shape=[big, big, big, jax.ShapeDtypeStruct((SSM_NB, 4, SSM_S // 2), F32)],
        compiler_params=pltpu.CompilerParams(
            dimension_semantics=("arbitrary",), vmem_limit_bytes=48 * 1024 * 1024),
        name="ssm_prep",
    )(lam, bexp, cexp, dexp)


def _ssm_operands(log_dt, lam_re, lam_im, b_re, b_im, c_re, c_im, d_skip):
    nb, gb = SSM_NB, SSM_GB
    eye_g = jnp.eye(gb, dtype=F32)
    lam = jnp.stack([jnp.broadcast_to(log_dt[:, None], (SSM_G, SSM_N)), lam_re, lam_im], axis=0)
    lam = lam.reshape(3, nb, gb * SSM_N).transpose(1, 0, 2)

    def expand_b(b):
        w = b.reshape(nb, gb, SSM_N, SSM_P).transpose(0, 1, 3, 2)
        return jnp.einsum('jgqn,gh->jgqhn', w, eye_g).reshape(nb, LANES, gb * SSM_N)

    def expand_c(c):
        w = c.reshape(nb, gb, SSM_P, SSM_N)
        return jnp.einsum('jgpn,gh->jgphn', w, eye_g).reshape(nb, LANES, gb * SSM_N)

    bexp = jnp.stack([expand_b(b_re), expand_b(b_im)], axis=1)
    cexp = jnp.stack([expand_c(c_re), expand_c(c_im)], axis=1)
    return lam, bexp, cexp, d_skip.reshape(nb, 1, LANES)


def _attn_kernel(k_ref, kb_ref, qt_ref, vt_ref, o_ref, rhs_ref, sa_ref, sb_ref, xa_ref, xb_ref,
                 acc_ref, m_ref):
    p = pl.program_id(1)
    i = pl.program_id(2)
    qt = qt_ref[0]
    row = lax.broadcasted_iota(jnp.int32, (LANES, TQ), 0)
    for hh in range(2):
        h = 2 * p + hh
        in_head = (row >= hh * HEAD_DIM) & (row < (hh + 1) * HEAD_DIM)
        rhs_ref[hh, 0:LANES, :] = jnp.where(in_head, qt, jnp.zeros_like(qt))
        pick = (row == h) | (row == h + N_HEADS) | (row == h + 2 * N_HEADS)
        rhs_ref[hh, LANES:2 * LANES, :] = jnp.where(pick, 1.0, 0.0).astype(BF16)
    m_ref[...] = jnp.full(m_ref.shape, NEG_BIG, F32)
    acc_ref[...] = jnp.zeros(acc_ref.shape, F32)

    key = lax.broadcasted_iota(jnp.int32, (TQ, TQ), 0)
    qry = lax.broadcasted_iota(jnp.int32, (TQ, TQ), 1)
    causal = key <= qry
    ones_row = jnp.where(lax.broadcasted_iota(jnp.int32, (V_PAD, TQ), 0) == 0, 1.0, 0.0).astype(BF16)

    def scores(j, s_ref, x_ref, heads=(0, 1)):
        rows = pl.ds(pl.multiple_of(j * TQ, TQ), TQ)
        lhs = jnp.concatenate([k_ref[rows, :], kb_ref[rows, :]], axis=1)
        for hh in heads:
            st = _dot(lhs, rhs_ref[hh])
            s_ref[hh] = st
            x_ref[hh] = jnp.max(st, axis=0, keepdims=True)

    def update(j, s_ref, x_ref, masked, heads=(0, 1)):
        for hh in heads:
            st = s_ref[hh]
            if masked:
                st = jnp.where(causal, st, NEG_BIG)
                blk_max = jnp.max(st, axis=0, keepdims=True)
            else:
                blk_max = x_ref[hh]
            m_prev = m_ref[hh]
            m_next = jnp.maximum(m_prev, blk_max)
            alpha = jnp.exp2(m_prev - m_next)
            pt = jnp.exp2(st - m_next).astype(BF16)
            vt = jnp.concatenate([vt_ref[j, hh * HEAD_DIM:(hh + 1) * HEAD_DIM, :], ones_row], axis=0)
            acc_ref[hh] = alpha * acc_ref[hh] + _dot(vt, pt)
            m_ref[hh] = m_next

    scores(0, sa_ref, xa_ref)

    def body(jj, carry):
        j = 2 * jj
        for hh in range(2):
            scores(j + 1, sb_ref, xb_ref, (hh,))
            update(j, sa_ref, xa_ref, False, (hh,))
        for hh in range(2):
            scores(j + 2, sa_ref, xa_ref, (hh,))
            update(j + 1, sb_ref, xb_ref, False, (hh,))
        return carry

    lax.fori_loop(0, i // 2, body, 0)

    @pl.when(i % 2 == 0)
    def _():
        update(i, sa_ref, xa_ref, True)

    @pl.when(i % 2 == 1)
    def _():
        for hh in range(2):
            scores(i, sb_ref, xb_ref, (hh,))
            update(i - 1, sa_ref, xa_ref, False, (hh,))
        update(i, sb_ref, xb_ref, True)

    for hh in range(2):
        acc = acc_ref[hh]
        o_ref[0, hh * HEAD_DIM:(hh + 1) * HEAD_DIM, :] = (
            acc[0:HEAD_DIM] / acc[HEAD_DIM:HEAD_DIM + 1]).astype(BF16)


def _attention(k, kb, qt, vt):
    nq = SEQ // TQ
    return pl.pallas_call(
        _attn_kernel,
        grid=(BATCH, N_HEADS // 2, nq),
        in_specs=[
            pl.BlockSpec((SEQ, LANES), lambda b, p, i: (b, p)),
            pl.BlockSpec((SEQ, LANES), lambda b, p, i: (b, 0)),
            pl.BlockSpec((1, LANES, TQ), lambda b, p, i: (b * nq + i, p, 0)),
            pl.BlockSpec((nq, LANES, TQ), lambda b, p, i: (b, p, 0)),
        ],
        out_specs=pl.BlockSpec((1, LANES, TQ), lambda b, p, i: (b * nq + i, p, 0)),
        out_shape=jax.ShapeDtypeStruct((TOKENS // TQ, D_ATTN, TQ), BF16),
        scratch_shapes=[pltpu.VMEM((2, 2 * LANES, TQ), BF16),
                        pltpu.VMEM((2, TQ, TQ), F32),
                        pltpu.VMEM((2, TQ, TQ), F32),
                        pltpu.VMEM((2, 1, TQ), F32),
                        pltpu.VMEM((2, 1, TQ), F32),
                        pltpu.VMEM((2, HEAD_DIM + V_PAD, TQ), F32),
                        pltpu.VMEM((2, 1, TQ), F32)],
        compiler_params=pltpu.CompilerParams(
            dimension_semantics=("arbitrary", "arbitrary", "arbitrary"),
            vmem_limit_bytes=48 * 1024 * 1024),
        name="fox_attention",
    )(k, kb, qt, vt)


def _mlp_kernel(apply_final, h_ref, ys_ref, oa_ref, gw_ref, gb_ref, gs_ref, ga_ref, wo_ref,
                l2_ref, wu_ref, wd_ref, fg_ref, o_ref):
    g = jax.nn.gelu(ys_ref[...])
    gate = 1.0 / (1.0 + jnp.exp(-(_dot(g.astype(BF16), gw_ref[...]) + gb_ref[...])))
    n_ssm = _rms(g * gate, gs_ref[...]).astype(BF16)
    oa = oa_ref[0].astype(F32)
    inv = lax.rsqrt(jnp.mean(oa * oa, axis=0, keepdims=True) + EPS)
    n_att = (oa * inv * ga_ref[...]).astype(BF16)
    h1 = (h_ref[...] + _dot(n_ssm, wo_ref[0:D_SSM, :])
          + lax.dot_general(n_att, wo_ref[D_SSM:D_MODEL, :], TN_DIMS, preferred_element_type=F32))
    xn = _rms(h1, l2_ref[...]).astype(BF16)
    o_ref[...] = h1
    for c in range(D_FF // TF_MLP):
        cols = slice(c * TF_MLP, (c + 1) * TF_MLP)
        a = jnp.maximum(_dot(xn, wu_ref[:, cols]), 0.0)
        o_ref[...] += _dot((a * a).astype(BF16), wd_ref[cols, :])
    if apply_final:
        o_ref[...] = _rms(o_ref[...], fg_ref[...])


def _mlp(h, y_ssm, o_att, glu_w, glu_b, gn_s, gn_a, w_out, ln2, w_up, w_down, final_g, apply_final):
    nm = TOKENS // TM_MLP
    row = lambda i: (i, 0)
    fixed = lambda *shape: pl.BlockSpec(shape, lambda i: (0, 0), pipeline_mode=pl.Buffered(1))
    return pl.pallas_call(
        functools.partial(_mlp_kernel, apply_final),
        grid=(nm,),
        in_specs=[
            pl.BlockSpec((TM_MLP, D_MODEL), row),
            pl.BlockSpec((TM_MLP, D_SSM), row),
            pl.BlockSpec((1, D_ATTN, TM_MLP), lambda i: (i, 0, 0)),
            fixed(D_SSM, D_SSM),
            fixed(1, D_SSM),
            fixed(1, D_SSM),
            fixed(D_ATTN, 1),
            fixed(D_MODEL, D_MODEL),
            fixed(1, D_MODEL),
            fixed(D_MODEL, D_FF),
            fixed(D_FF, D_MODEL),
            fixed(1, D_MODEL),
        ],
        out_specs=pl.BlockSpec((TM_MLP, D_MODEL), row),
        out_shape=jax.ShapeDtypeStruct((TOKENS, D_MODEL), F32),
        compiler_params=pltpu.CompilerParams(
            dimension_semantics=("arbitrary",), vmem_limit_bytes=56 * 1024 * 1024),
        name="mixer_out_mlp",
    )(h, y_ssm, o_att, glu_w, glu_b, gn_s, gn_a, w_out, ln2, w_up, w_down, final_g)


def kernel(x, ln1_g, w_in, ssm_log_dt, ssm_lambda_re, ssm_lambda_im, ssm_b_re, ssm_b_im, ssm_c_re, ssm_c_im, ssm_d, glu_w, glu_b, fgate_b, gn_ssm_g, gn_attn_g, w_out, ln2_g, w_up, w_down, final_g):
    assert TM_MLP == TQ
    h = x.reshape(TOKENS, D_MODEL)
    for l in range(DEPTH):
        w = w_in[l]
        w_u = w[:, :D_SSM].astype(BF16)
        w_qt = w[:, D_SSM:D_SSM + D_ATTN].T.astype(BF16)
        w_k = w[:, D_SSM + D_ATTN:D_SSM + 2 * D_ATTN].astype(BF16)
        w_vt = w[:, D_SSM + 2 * D_ATTN:D_SSM + 3 * D_ATTN].T.astype(BF16)
        gate_cols = jnp.concatenate([w[:, 4 * D_SSM:]] * 3, axis=1)
        w_gate = jnp.pad(gate_cols, ((0, 0), (0, LANES - 3 * N_HEADS))).astype(BF16)
        b_gate = jnp.pad(jnp.concatenate([fgate_b[l]] * 3), (0, LANES - 3 * N_HEADS)).reshape(1, LANES)

        u, k, kb, qt, vt = _inproj(h, ln1_g[l].reshape(1, D_MODEL), w_u, w_k, w_qt, w_vt, w_gate, b_gate)

        mats = _ssm_prep(*_ssm_operands(ssm_log_dt[l], ssm_lambda_re[l], ssm_lambda_im[l], ssm_b_re[l],
                                        ssm_b_im[l], ssm_c_re[l], ssm_c_im[l], ssm_d[l]))
        y_ssm = _ssm(u, *mats)
        o_att = _attention(k, kb, qt, vt)

        h = _mlp(h, y_ssm, o_att, glu_w[l].astype(BF16), glu_b[l].reshape(1, D_SSM),
                 gn_ssm_g[l].reshape(1, D_SSM), gn_attn_g[l].reshape(D_ATTN, 1),
                 w_out[l].astype(BF16), ln2_g[l].reshape(1, D_MODEL), w_up[l].astype(BF16),
                 w_down[l].astype(BF16), final_g.reshape(1, D_MODEL), l == DEPTH - 1)
    return h.reshape(BATCH, SEQ, D_MODEL)
```

```python
import functools
import math

import jax
import jax.numpy as jnp
from jax import lax
from jax.experimental import pallas as pl
from jax.experimental.pallas import tpu as pltpu

F32 = jnp.float32
BF16 = jnp.bfloat16

D_MODEL = 1024
BATCH = 4
SEQ = 4096
DEPTH = 2
TOKENS = BATCH * SEQ
D_SSM = 512
D_ATTN = 512
SSM_P = 16
SSM_G = 32
SSM_N = 64
HEAD_DIM = 64
N_HEADS = 8
D_FF = 4096
EPS = 1e-6
LOG2E = 1.4426950408889634

LANES = 128
SUBLANES = 8

SSM_CHUNK = SUBLANES
SSM_GB = LANES // SSM_P
SSM_NB = D_SSM // LANES
SSM_W = SSM_CHUNK * LANES
SSM_S = 2 * SSM_GB * SSM_N
SSM_TM = SEQ
SSM_R = SSM_TM // SSM_CHUNK
SSM_SEG = SSM_R // SUBLANES

TM_IN = 1024
TQ = 512
TM_MLP = 512
TF_MLP = 1024
CS_BLK = LANES
V_PAD = 16
NEG_BIG = -1e30

NT_DIMS = (((1,), (1,)), ((), ()))
TN_DIMS = (((0,), (0,)), ((), ()))


def _rms(x, g):
    return x * lax.rsqrt(jnp.mean(x * x, axis=-1, keepdims=True) + EPS) * g


def _dot(a, b):
    return jnp.dot(a, b, preferred_element_type=F32)


def _split3(x):
    hi = x.astype(BF16)
    r1 = x - hi.astype(F32)
    mid = r1.astype(BF16)
    low = (r1 - mid.astype(F32)).astype(BF16)
    return hi, mid, low


def _inproj_kernel(x_ref, g_ref, wu_ref, wk_ref, wqt_ref, wvt_ref, wf_ref, fb_ref,
                   u_ref, k_ref, kb_ref, qt_ref, vt_ref, carry_ref):
    i = pl.program_id(0)
    xn = _rms(x_ref[...], g_ref[...]).astype(BF16)
    u_ref[...] = _dot(xn, wu_ref[...])
    k_ref[...] = _dot(xn, wk_ref[...]).astype(BF16)
    qscale = LOG2E / math.sqrt(HEAD_DIM)
    qt = lax.dot_general(wqt_ref[...], xn, NT_DIMS, preferred_element_type=F32)
    vt = lax.dot_general(wvt_ref[...], xn, NT_DIMS, preferred_element_type=F32)
    for t in range(TM_IN // TQ):
        qt_ref[t] = (qt[:, t * TQ:(t + 1) * TQ] * qscale).astype(BF16)
        vt_ref[t] = vt[:, t * TQ:(t + 1) * TQ].astype(BF16)

    f = _dot(xn, wf_ref[...]) + fb_ref[...]
    lf = jnp.minimum(f, 0.0) - jnp.log1p(jnp.exp(-jnp.abs(f)))

    @pl.when(i % (SEQ // TM_IN) == 0)
    def _():
        carry_ref[...] = jnp.zeros(carry_ref.shape, F32)

    r = lax.broadcasted_iota(jnp.int32, (CS_BLK, CS_BLK), 0)
    c = lax.broadcasted_iota(jnp.int32, (CS_BLK, CS_BLK), 1)
    tril = jnp.where(c <= r, 1.0, 0.0).astype(BF16)
    lane = lax.broadcasted_iota(jnp.int32, (CS_BLK, LANES), 1)
    carry = carry_ref[...]
    for b in range(TM_IN // CS_BLK):
        hi, mid, low = _split3(lf[b * CS_BLK:(b + 1) * CS_BLK])
        cs = _dot(tril, hi) + _dot(tril, mid) + _dot(tril, low) + carry
        carry = cs[CS_BLK - 1:CS_BLK, :]
        p0, p1, p2 = _split3(cs * (-LOG2E))
        piece = jnp.where(lane < N_HEADS, p0, jnp.where(lane < 2 * N_HEADS, p1, p2))
        kb_ref[b * CS_BLK:(b + 1) * CS_BLK, :] = jnp.where(
            lane < 3 * N_HEADS, piece, jnp.zeros_like(piece))
    carry_ref[...] = carry


def _inproj(h, g, wu, wk, wqt, wvt, wf, fb):
    n = TOKENS // TM_IN
    fixed = lambda i: (0, 0)
    return pl.pallas_call(
        _inproj_kernel,
        grid=(n,),
        in_specs=[
            pl.BlockSpec((TM_IN, D_MODEL), lambda i: (i, 0)),
            pl.BlockSpec((1, D_MODEL), fixed),
            pl.BlockSpec((D_MODEL, D_SSM), fixed),
            pl.BlockSpec((D_MODEL, D_ATTN), fixed),
            pl.BlockSpec((D_ATTN, D_MODEL), fixed),
            pl.BlockSpec((D_ATTN, D_MODEL), fixed),
            pl.BlockSpec((D_MODEL, LANES), fixed),
            pl.BlockSpec((1, LANES), fixed),
        ],
        out_specs=[
            pl.BlockSpec((TM_IN, D_SSM), lambda i: (i, 0)),
            pl.BlockSpec((TM_IN, D_ATTN), lambda i: (i, 0)),
            pl.BlockSpec((TM_IN, LANES), lambda i: (i, 0)),
            pl.BlockSpec((TM_IN // TQ, D_ATTN, TQ), lambda i: (i, 0, 0)),
            pl.BlockSpec((TM_IN // TQ, D_ATTN, TQ), lambda i: (i, 0, 0)),
        ],
        out_shape=[
            jax.ShapeDtypeStruct((TOKENS, D_SSM), F32),
            jax.ShapeDtypeStruct((TOKENS, D_ATTN), BF16),
            jax.ShapeDtypeStruct((TOKENS, LANES), BF16),
            jax.ShapeDtypeStruct((TOKENS // TQ, D_ATTN, TQ), BF16),
            jax.ShapeDtypeStruct((TOKENS // TQ, D_ATTN, TQ), BF16),
        ],
        scratch_shapes=[pltpu.VMEM((1, LANES), F32)],
        compiler_params=pltpu.CompilerParams(
            dimension_semantics=("arbitrary",), vmem_limit_bytes=48 * 1024 * 1024),
        name="inproj",
    )(h, g, wu, wk, wqt, wvt, wf, fb)


def _ssm_kernel(u_ref, m_ref, ws_ref, wct_ref, a_ref, y_ref, s_ref, x_ref, xs_ref, yl_ref):
    nt = SSM_S // 2 // LANES
    tiles = range(nt)
    z = jnp.concatenate(
        [u_ref[pl.ds(s, SSM_R, stride=SSM_CHUNK), :].astype(BF16) for s in range(SSM_CHUNK)], axis=1)
    s = _dot(z, ws_ref[0])
    seg_rows = lambda g: pl.ds(g, SSM_SEG, stride=SUBLANES)
    for c in range(2 * nt):
        for g in range(SUBLANES):
            s_ref[c, seg_rows(g), :] = s[g * SSM_SEG:(g + 1) * SSM_SEG, c * LANES:(c + 1) * LANES]
    mxu_n = 2 * LANES
    for c in range(SSM_W // mxu_n):
        hi = (c + 1) * mxu_n
        yl_ref[:, c * mxu_n:hi] = _dot(z[:, 0:hi], m_ref[0, 0:hi, c * mxu_n:hi])

    lane_tiles = lambda row: [a_ref[0, row:row + 1, c * LANES:(c + 1) * LANES] for c in tiles]
    full = lambda rows: [jnp.broadcast_to(r, (SUBLANES, LANES)) for r in rows]
    ar, ai = full(lane_tiles(0)), full(lane_tiles(1))
    gr, gi = lane_tiles(2), lane_tiles(3)

    def local(k, carry):
        cr, ci = carry
        rows = pl.ds(pl.multiple_of(k * SUBLANES, SUBLANES), SUBLANES)
        nr, ni = [], []
        for c in tiles:
            x_ref[c, rows, :] = cr[c]
            x_ref[nt + c, rows, :] = ci[c]
            nr.append(ar[c] * cr[c] - ai[c] * ci[c] + s_ref[c, rows, :])
            ni.append(ar[c] * ci[c] + ai[c] * cr[c] + s_ref[nt + c, rows, :])
        return tuple(nr), tuple(ni)

    zero = tuple(jnp.zeros((SUBLANES, LANES), F32) for _ in tiles)
    er, ei = lax.fori_loop(0, SSM_SEG, local, (zero, zero))

    for c in tiles:
        cr = jnp.zeros((1, LANES), F32)
        ci = jnp.zeros((1, LANES), F32)
        for g in range(SUBLANES):
            xs_ref[c, g:g + 1, :] = cr
            xs_ref[nt + c, g:g + 1, :] = ci
            cr, ci = (gr[c] * cr - gi[c] * ci + er[c][g:g + 1], gr[c] * ci + gi[c] * cr + ei[c][g:g + 1])
    xr = [xs_ref[c] for c in tiles]
    xi = [xs_ref[nt + c] for c in tiles]

    def fix(k, carry):
        pr, pi = carry
        rows = pl.ds(pl.multiple_of(k * SUBLANES, SUBLANES), SUBLANES)
        nr, ni = [], []
        for c in tiles:
            x_ref[c, rows, :] += pr[c] * xr[c] - pi[c] * xi[c]
            x_ref[nt + c, rows, :] += pr[c] * xi[c] + pi[c] * xr[c]
            nr.append(pr[c] * ar[c] - pi[c] * ai[c])
            ni.append(pr[c] * ai[c] + pi[c] * ar[c])
        return tuple(nr), tuple(ni)

    one = tuple(jnp.ones((SUBLANES, LANES), F32) for _ in tiles)
    lax.fori_loop(0, SSM_SEG, fix, (one, zero))

    x = jnp.concatenate(
        [jnp.concatenate([x_ref[c, seg_rows(g), :] for g in range(SUBLANES)], axis=0).astype(BF16)
         for c in range(2 * nt)], axis=1)
    y = yl_ref[...] + lax.dot_general(x, wct_ref[0], NT_DIMS, preferred_element_type=F32)
    for t in range(SSM_CHUNK):
        y_ref[pl.ds(t, SSM_R, stride=SSM_CHUNK), :] = y[:, t * LANES:(t + 1) * LANES]


def _ssm(u, m, ws, wct, apow):
    nt = TOKENS // SSM_TM
    return pl.pallas_call(
        _ssm_kernel,
        grid=(SSM_NB, nt),
        in_specs=[
            pl.BlockSpec((SSM_TM, LANES), lambda j, i: (i, j)),
            pl.BlockSpec((1, SSM_W, SSM_W), lambda j, i: (j, 0, 0)),
            pl.BlockSpec((1, SSM_W, SSM_S), lambda j, i: (j, 0, 0)),
            pl.BlockSpec((1, SSM_W, SSM_S), lambda j, i: (j, 0, 0)),
            pl.BlockSpec((1, 4, SSM_S // 2), lambda j, i: (j, 0, 0)),
        ],
        out_specs=pl.BlockSpec((SSM_TM, LANES), lambda j, i: (i, j)),
        out_shape=jax.ShapeDtypeStruct((TOKENS, D_SSM), F32),
        scratch_shapes=[pltpu.VMEM((SSM_S // LANES, SSM_R, LANES), F32),
                        pltpu.VMEM((SSM_S // LANES, SSM_R, LANES), F32),
                        pltpu.VMEM((SSM_S // LANES, SUBLANES, LANES), F32),
                        pltpu.VMEM((SSM_R, SSM_W), F32)],
        compiler_params=pltpu.CompilerParams(
            dimension_semantics=("arbitrary", "arbitrary"), vmem_limit_bytes=48 * 1024 * 1024),
        name="ssm_scan",
    )(u, m, ws, wct, apow)


def _split2(x):
    hi = x.astype(BF16)
    return hi, (x - hi.astype(F32)).astype(BF16)


def _dot_nt_f32(a, b):
    ah, al = _split2(a)
    bh, bl = _split2(b)
    d = lambda x, y: lax.dot_general(x, y, NT_DIMS, preferred_element_type=F32)
    return d(ah, bh) + d(ah, bl) + d(al, bh)


def _ssm_prep_kernel(lam_ref, b_ref, c_ref, d_ref, m_ref, ws_ref, wct_ref, a_ref):
    half = SSM_S // 2
    tc = SSM_CHUNK
    ldt, lre, lim = lam_ref[0, 0:1, :], lam_ref[0, 1:2, :], lam_ref[0, 2:3, :]
    dt = jnp.exp(ldt)
    tau = lax.broadcasted_iota(jnp.int32, (2 * SUBLANES, half), 0).astype(F32)
    pmag = jnp.exp(tau * (lre * dt))
    p_re = pmag * jnp.cos(tau * (lim * dt))
    p_im = pmag * jnp.sin(tau * (lim * dt))
    a_re, a_im = p_re[1:2], p_im[1:2]
    den = lre * lre + lim * lim
    nr = a_re - 1.0
    s_re = (nr * lre + a_im * lim) / den
    s_im = (a_im * lre - nr * lim) / den
    b_re, b_im = b_ref[0, 0], b_ref[0, 1]
    bb_re = s_re * b_re - s_im * b_im
    bb_im = s_re * b_im + s_im * b_re
    c_re, c_im = c_ref[0, 0], c_ref[0, 1]
    eye = (lax.broadcasted_iota(jnp.int32, (LANES, LANES), 0)
           == lax.broadcasted_iota(jnp.int32, (LANES, LANES), 1))

    m_ref[...] = jnp.zeros(m_ref.shape, BF16)
    for lag in range(tc):
        pr, pi = p_re[lag:lag + 1], p_im[lag:lag + 1]
        l_re = pr * bb_re - pi * bb_im
        l_im = pr * bb_im + pi * bb_re
        rows = slice((tc - 1 - lag) * LANES, (tc - lag) * LANES)
        ws_ref[0, rows, 0:half] = l_re.astype(BF16)
        ws_ref[0, rows, half:SSM_S] = l_im.astype(BF16)
        x = _dot_nt_f32(l_re, c_re) - _dot_nt_f32(l_im, c_im)
        if lag == 0:
            x = x + jnp.where(eye, d_ref[0], 0.0)
        x = x.astype(BF16)
        for s in range(tc - lag):
            m_ref[0, s * LANES:(s + 1) * LANES, (s + lag) * LANES:(s + lag + 1) * LANES] = x
    for t in range(tc):
        pr, pi = p_re[t + 1:t + 2], p_im[t + 1:t + 2]
        rows = slice(t * LANES, (t + 1) * LANES)
        wct_ref[0, rows, 0:half] = (c_re * pr - c_im * pi).astype(BF16)
        wct_ref[0, rows, half:SSM_S] = (-(c_re * pi + c_im * pr)).astype(BF16)

    qr, qi = p_re[tc:tc + 1], p_im[tc:tc + 1]
    a_ref[0, 0:1, :] = qr
    a_ref[0, 1:2, :] = qi
    for _ in range(SSM_SEG.bit_length() - 1):
        qr, qi = qr * qr - qi * qi, 2.0 * qr * qi
    a_ref[0, 2:3, :] = qr
    a_ref[0, 3:4, :] = qi


def _ssm_prep(lam, bexp, cexp, dexp):
    blk = lambda *shape: pl.BlockSpec((1,) + shape, lambda j: (j,) + (0,) * len(shape))
    big = jax.ShapeDtypeStruct((SSM_NB, SSM_W, SSM_S), BF16)
    return pl.pallas_call(
        _ssm_prep_kernel,
        grid=(SSM_NB,),
        in_specs=[blk(3, SSM_S // 2), blk(2, LANES, SSM_S // 2), blk(2, LANES, SSM_S // 2), blk(1, LANES)],
        out_specs=[blk(SSM_W, SSM_W), blk(SSM_W, SSM_S), blk(SSM_W, SSM_S), blk(4, SSM_S // 2)],
        out_shape=[big, big, big, jax.ShapeDtypeStruct((SSM_NB, 4, SSM_S // 2), F32)],
        compiler_params=pltpu.CompilerParams(
            dimension_semantics=("arbitrary",), vmem_limit_bytes=48 * 1024 * 1024),
        name="ssm_prep",
    )(lam, bexp, cexp, dexp)


def _ssm_operands(log_dt, lam_re, lam_im, b_re, b_im, c_re, c_im, d_skip):
    nb, gb = SSM_NB, SSM_GB
    eye_g = jnp.eye(gb, dtype=F32)
    lam = jnp.stack([jnp.broadcast_to(log_dt[:, None], (SSM_G, SSM_N)), lam_re, lam_im], axis=0)
    lam = lam.reshape(3, nb, gb * SSM_N).transpose(1, 0, 2)

    def expand_b(b):
        w = b.reshape(nb, gb, SSM_N, SSM_P).transpose(0, 1, 3, 2)
        return jnp.einsum('jgqn,gh->jgqhn', w, eye_g).reshape(nb, LANES, gb * SSM_N)

    def expand_c(c):
        w = c.reshape(nb, gb, SSM_P, SSM_N)
        return jnp.einsum('jgpn,gh->jgphn', w, eye_g).reshape(nb, LANES, gb * SSM_N)

    bexp = jnp.stack([expand_b(b_re), expand_b(b_im)], axis=1)
    cexp = jnp.stack([expand_c(c_re), expand_c(c_im)], axis=1)
    return lam, bexp, cexp, d_skip.reshape(nb, 1, LANES)


def _attn_kernel(k_ref, kb_ref, qt_ref, vt_ref, o_ref, rhs_ref, sa_ref, sb_ref, xa_ref, xb_ref,
                 acc_ref, m_ref):
    p = pl.program_id(1)
    row = lax.broadcasted_iota(jnp.int32, (LANES, TQ), 0)
    in_head = []
    for hh in range(2):
        h = 2 * p + hh
        in_head.append((row >= hh * HEAD_DIM) & (row < (hh + 1) * HEAD_DIM))
        pick = (row == h) | (row == h + N_HEADS) | (row == h + 2 * N_HEADS)
        rhs_ref[hh, LANES:2 * LANES, :] = jnp.where(pick, 1.0, 0.0).astype(BF16)

    key = lax.broadcasted_iota(jnp.int32, (TQ, TQ), 0)
    qry = lax.broadcasted_iota(jnp.int32, (TQ, TQ), 1)
    causal = key <= qry
    ones_row = jnp.where(lax.broadcasted_iota(jnp.int32, (V_PAD, TQ), 0) == 0, 1.0, 0.0).astype(BF16)

    def scores(j, s_ref, x_ref, heads=(0, 1)):
        rows = pl.ds(pl.multiple_of(j * TQ, TQ), TQ)
        lhs = jnp.concatenate([k_ref[rows, :], kb_ref[rows, :]], axis=1)
        for hh in heads:
            st = _dot(lhs, rhs_ref[hh])
            s_ref[hh] = st
            x_ref[hh] = jnp.max(st, axis=0, keepdims=True)

    def update(j, s_ref, x_ref, masked, heads=(0, 1)):
        for hh in heads:
            st = s_ref[hh]
            if masked:
                st = jnp.where(causal, st, NEG_BIG)
                blk_max = jnp.max(st, axis=0, keepdims=True)
            else:
                blk_max = x_ref[hh]
            m_prev = m_ref[hh]
            m_next = jnp.maximum(m_prev, blk_max)
            alpha = jnp.exp2(m_prev - m_next)
            pt = jnp.exp2(st - m_next).astype(BF16)
            vt = jnp.concatenate([vt_ref[j, hh * HEAD_DIM:(hh + 1) * HEAD_DIM, :], ones_row], axis=0)
            acc_ref[hh] = alpha * acc_ref[hh] + _dot(vt, pt)
            m_ref[hh] = m_next

    def body(jj, carry):
        j = 2 * jj
        for hh in range(2):
            scores(j + 1, sb_ref, xb_ref, (hh,))
            update(j, sa_ref, xa_ref, False, (hh,))
        for hh in range(2):
            scores(j + 2, sa_ref, xa_ref, (hh,))
            update(j + 1, sb_ref, xb_ref, False, (hh,))
        return carry

    def tile(i, carry):
        qt = qt_ref[i]
        for hh in range(2):
            rhs_ref[hh, 0:LANES, :] = jnp.where(in_head[hh], qt, jnp.zeros_like(qt))
        m_ref[...] = jnp.full(m_ref.shape, NEG_BIG, F32)
        acc_ref[...] = jnp.zeros(acc_ref.shape, F32)

        scores(0, sa_ref, xa_ref)
        lax.fori_loop(0, i // 2, body, 0)

        @pl.when(i % 2 == 0)
        def _():
            update(i, sa_ref, xa_ref, True)

        @pl.when(i % 2 == 1)
        def _():
            for hh in range(2):
                scores(i, sb_ref, xb_ref, (hh,))
                update(i - 1, sa_ref, xa_ref, False, (hh,))
            update(i, sb_ref, xb_ref, True)

        for hh in range(2):
            acc = acc_ref[hh]
            o_ref[i, hh * HEAD_DIM:(hh + 1) * HEAD_DIM, :] = (
                acc[0:HEAD_DIM] / acc[HEAD_DIM:HEAD_DIM + 1]).astype(BF16)
        return carry

    lax.fori_loop(0, SEQ // TQ, tile, 0)


def _attention(k, kb, qt, vt):
    nq = SEQ // TQ
    return pl.pallas_call(
        _attn_kernel,
        grid=(BATCH, N_HEADS // 2),
        in_specs=[
            pl.BlockSpec((SEQ, LANES), lambda b, p: (b, p)),
            pl.BlockSpec((SEQ, LANES), lambda b, p: (b, 0)),
            pl.BlockSpec((nq, LANES, TQ), lambda b, p: (b, p, 0)),
            pl.BlockSpec((nq, LANES, TQ), lambda b, p: (b, p, 0)),
        ],
        out_specs=pl.BlockSpec((nq, LANES, TQ), lambda b, p: (b, p, 0)),
        out_shape=jax.ShapeDtypeStruct((TOKENS // TQ, D_ATTN, TQ), BF16),
        scratch_shapes=[pltpu.VMEM((2, 2 * LANES, TQ), BF16),
                        pltpu.VMEM((2, TQ, TQ), F32),
                        pltpu.VMEM((2, TQ, TQ), F32),
                        pltpu.VMEM((2, 1, TQ), F32),
                        pltpu.VMEM((2, 1, TQ), F32),
                        pltpu.VMEM((2, HEAD_DIM + V_PAD, TQ), F32),
                        pltpu.VMEM((2, 1, TQ), F32)],
        compiler_params=pltpu.CompilerParams(
            dimension_semantics=("arbitrary", "arbitrary"), vmem_limit_bytes=48 * 1024 * 1024),
        name="fox_attention",
    )(k, kb, qt, vt)


def _mlp_kernel(apply_final, h_ref, ys_ref, oa_ref, gw_ref, gb_ref, gs_ref, ga_ref, wo_ref,
                l2_ref, wu_ref, wd_ref, fg_ref, o_ref):
    g = jax.nn.gelu(ys_ref[...])
    gate = 1.0 / (1.0 + jnp.exp(-(_dot(g.astype(BF16), gw_ref[...]) + gb_ref[...])))
    n_ssm = _rms(g * gate, gs_ref[...]).astype(BF16)
    oa = oa_ref[0].astype(F32)
    inv = lax.rsqrt(jnp.mean(oa * oa, axis=0, keepdims=True) + EPS)
    n_att = (oa * inv * ga_ref[...]).astype(BF16)
    h1 = (h_ref[...] + _dot(n_ssm, wo_ref[0:D_SSM, :])
          + lax.dot_general(n_att, wo_ref[D_SSM:D_MODEL, :], TN_DIMS, preferred_element_type=F32))
    xn = _rms(h1, l2_ref[...]).astype(BF16)
    o_ref[...] = h1
    for c in range(D_FF // TF_MLP):
        cols = slice(c * TF_MLP, (c + 1) * TF_MLP)
        a = jnp.maximum(_dot(xn, wu_ref[:, cols]), 0.0)
        o_ref[...] += _dot((a * a).astype(BF16), wd_ref[cols, :])
    if apply_final:
        o_ref[...] = _rms(o_ref[...], fg_ref[...])


def _mlp(h, y_ssm, o_att, glu_w, glu_b, gn_s, gn_a, w_out, ln2, w_up, w_down, final_g, apply_final):
    nm = TOKENS // TM_MLP
    row = lambda i: (i, 0)
    fixed = lambda *shape: pl.BlockSpec(shape, lambda i: (0, 0), pipeline_mode=pl.Buffered(1))
    return pl.pallas_call(
        functools.partial(_mlp_kernel, apply_final),
        grid=(nm,),
        in_specs=[
            pl.BlockSpec((TM_MLP, D_MODEL), row),
            pl.BlockSpec((TM_MLP, D_SSM), row),
            pl.BlockSpec((1, D_ATTN, TM_MLP), lambda i: (i, 0, 0)),
            fixed(D_SSM, D_SSM),
            fixed(1, D_SSM),
            fixed(1, D_SSM),
            fixed(D_ATTN, 1),
            fixed(D_MODEL, D_MODEL),
            fixed(1, D_MODEL),
            fixed(D_MODEL, D_FF),
            fixed(D_FF, D_MODEL),
            fixed(1, D_MODEL),
        ],
        out_specs=pl.BlockSpec((TM_MLP, D_MODEL), row),
        out_shape=jax.ShapeDtypeStruct((TOKENS, D_MODEL), F32),
        compiler_params=pltpu.CompilerParams(
            dimension_semantics=("arbitrary",), vmem_limit_bytes=56 * 1024 * 1024),
        name="mixer_out_mlp",
    )(h, y_ssm, o_att, glu_w, glu_b, gn_s, gn_a, w_out, ln2, w_up, w_down, final_g)


def kernel(x, ln1_g, w_in, ssm_log_dt, ssm_lambda_re, ssm_lambda_im, ssm_b_re, ssm_b_im, ssm_c_re, ssm_c_im, ssm_d, glu_w, glu_b, fgate_b, gn_ssm_g, gn_attn_g, w_out, ln2_g, w_up, w_down, final_g):
    assert TM_MLP == TQ
    h = x.reshape(TOKENS, D_MODEL)
    for l in range(DEPTH):
        w = w_in[l]
        w_u = w[:, :D_SSM].astype(BF16)
        w_qt = w[:, D_SSM:D_SSM + D_ATTN].T.astype(BF16)
        w_k = w[:, D_SSM + D_ATTN:D_SSM + 2 * D_ATTN].astype(BF16)
        w_vt = w[:, D_SSM + 2 * D_ATTN:D_SSM + 3 * D_ATTN].T.astype(BF16)
        gate_cols = jnp.concatenate([w[:, 4 * D_SSM:]] * 3, axis=1)
        w_gate = jnp.pad(gate_cols, ((0, 0), (0, LANES - 3 * N_HEADS))).astype(BF16)
        b_gate = jnp.pad(jnp.concatenate([fgate_b[l]] * 3), (0, LANES - 3 * N_HEADS)).reshape(1, LANES)

        u, k, kb, qt, vt = _inproj(h, ln1_g[l].reshape(1, D_MODEL), w_u, w_k, w_qt, w_vt, w_gate, b_gate)

        mats = _ssm_prep(*_ssm_operands(ssm_log_dt[l], ssm_lambda_re[l], ssm_lambda_im[l], ssm_b_re[l],
                                        ssm_b_im[l], ssm_c_re[l], ssm_c_im[l], ssm_d[l]))
        y_ssm = _ssm(u, *mats)
        o_att = _attention(k, kb, qt, vt)

        h = _mlp(h, y_ssm, o_att, glu_w[l].astype(BF16), glu_b[l].reshape(1, D_SSM),
                 gn_ssm_g[l].reshape(1, D_SSM), gn_attn_g[l].reshape(D_ATTN, 1),
                 w_out[l].astype(BF16), ln2_g[l].reshape(1, D_MODEL), w_up[l].astype(BF16),
                 w_down[l].astype(BF16), final_g.reshape(1, D_MODEL), l == DEPTH - 1)
    return h.reshape(BATCH, SEQ, D_MODEL)
```

```python
import functools
import math

import jax
import jax.numpy as jnp
from jax import lax
from jax.experimental import pallas as pl
from jax.experimental.pallas import tpu as pltpu

F32 = jnp.float32
BF16 = jnp.bfloat16

D_MODEL = 1024
BATCH = 4
SEQ = 4096
DEPTH = 2
TOKENS = BATCH * SEQ
D_SSM = 512
D_ATTN = 512
SSM_P = 16
SSM_G = 32
SSM_N = 64
HEAD_DIM = 64
N_HEADS = 8
D_FF = 4096
EPS = 1e-6
LOG2E = 1.4426950408889634

LANES = 128
SUBLANES = 8

SSM_CHUNK = SUBLANES
SSM_GB = LANES // SSM_P
SSM_NB = D_SSM // LANES
SSM_W = SSM_CHUNK * LANES
SSM_S = 2 * SSM_GB * SSM_N
SSM_TM = SEQ
SSM_R = SSM_TM // SSM_CHUNK
SSM_SEG = SSM_R // SUBLANES

TM_IN = 1024
TQ = 512
QW = 2 * TQ
TM_MLP = 512
TF_MLP = 1024
CS_BLK = LANES
V_PAD = 16
NEG_BIG = -1e30

NT_DIMS = (((1,), (1,)), ((), ()))
TN_DIMS = (((0,), (0,)), ((), ()))


def _rms(x, g):
    return x * lax.rsqrt(jnp.mean(x * x, axis=-1, keepdims=True) + EPS) * g


def _dot(a, b):
    return jnp.dot(a, b, preferred_element_type=F32)


def _split3(x):
    hi = x.astype(BF16)
    r1 = x - hi.astype(F32)
    mid = r1.astype(BF16)
    low = (r1 - mid.astype(F32)).astype(BF16)
    return hi, mid, low


def _inproj_kernel(x_ref, g_ref, wu_ref, wk_ref, wqt_ref, wvt_ref, wf_ref, fb_ref,
                   u_ref, k_ref, kb_ref, qt_ref, vt_ref, carry_ref):
    i = pl.program_id(0)
    xn = _rms(x_ref[...], g_ref[...]).astype(BF16)
    u_ref[...] = _dot(xn, wu_ref[...])
    k_ref[...] = _dot(xn, wk_ref[...]).astype(BF16)
    qscale = LOG2E / math.sqrt(HEAD_DIM)
    qt = lax.dot_general(wqt_ref[...], xn, NT_DIMS, preferred_element_type=F32)
    vt = lax.dot_general(wvt_ref[...], xn, NT_DIMS, preferred_element_type=F32)
    for t in range(TM_IN // TQ):
        qt_ref[t] = (qt[:, t * TQ:(t + 1) * TQ] * qscale).astype(BF16)
        vt_ref[t] = vt[:, t * TQ:(t + 1) * TQ].astype(BF16)

    f = _dot(xn, wf_ref[...]) + fb_ref[...]
    lf = jnp.minimum(f, 0.0) - jnp.log1p(jnp.exp(-jnp.abs(f)))

    @pl.when(i % (SEQ // TM_IN) == 0)
    def _():
        carry_ref[...] = jnp.zeros(carry_ref.shape, F32)

    r = lax.broadcasted_iota(jnp.int32, (CS_BLK, CS_BLK), 0)
    c = lax.broadcasted_iota(jnp.int32, (CS_BLK, CS_BLK), 1)
    tril = jnp.where(c <= r, 1.0, 0.0).astype(BF16)
    lane = lax.broadcasted_iota(jnp.int32, (CS_BLK, LANES), 1)
    carry = carry_ref[...]
    for b in range(TM_IN // CS_BLK):
        hi, mid, low = _split3(lf[b * CS_BLK:(b + 1) * CS_BLK])
        cs = _dot(tril, hi) + _dot(tril, mid) + _dot(tril, low) + carry
        carry = cs[CS_BLK - 1:CS_BLK, :]
        p0, p1, p2 = _split3(cs * (-LOG2E))
        piece = jnp.where(lane < N_HEADS, p0, jnp.where(lane < 2 * N_HEADS, p1, p2))
        kb_ref[b * CS_BLK:(b + 1) * CS_BLK, :] = jnp.where(
            lane < 3 * N_HEADS, piece, jnp.zeros_like(piece))
    carry_ref[...] = carry


def _inproj(h, g, wu, wk, wqt, wvt, wf, fb):
    n = TOKENS // TM_IN
    fixed = lambda i: (0, 0)
    return pl.pallas_call(
        _inproj_kernel,
        grid=(n,),
        in_specs=[
            pl.BlockSpec((TM_IN, D_MODEL), lambda i: (i, 0)),
            pl.BlockSpec((1, D_MODEL), fixed),
            pl.BlockSpec((D_MODEL, D_SSM), fixed),
            pl.BlockSpec((D_MODEL, D_ATTN), fixed),
            pl.BlockSpec((D_ATTN, D_MODEL), fixed),
            pl.BlockSpec((D_ATTN, D_MODEL), fixed),
            pl.BlockSpec((D_MODEL, LANES), fixed),
            pl.BlockSpec((1, LANES), fixed),
        ],
        out_specs=[
            pl.BlockSpec((TM_IN, D_SSM), lambda i: (i, 0)),
            pl.BlockSpec((TM_IN, D_ATTN), lambda i: (i, 0)),
            pl.BlockSpec((TM_IN, LANES), lambda i: (i, 0)),
            pl.BlockSpec((TM_IN // TQ, D_ATTN, TQ), lambda i: (i, 0, 0)),
            pl.BlockSpec((TM_IN // TQ, D_ATTN, TQ), lambda i: (i, 0, 0)),
        ],
        out_shape=[
            jax.ShapeDtypeStruct((TOKENS, D_SSM), F32),
            jax.ShapeDtypeStruct((TOKENS, D_ATTN), BF16),
            jax.ShapeDtypeStruct((TOKENS, LANES), BF16),
            jax.ShapeDtypeStruct((TOKENS // TQ, D_ATTN, TQ), BF16),
            jax.ShapeDtypeStruct((TOKENS // TQ, D_ATTN, TQ), BF16),
        ],
        scratch_shapes=[pltpu.VMEM((1, LANES), F32)],
        compiler_params=pltpu.CompilerParams(
            dimension_semantics=("arbitrary",), vmem_limit_bytes=48 * 1024 * 1024),
        name="inproj",
    )(h, g, wu, wk, wqt, wvt, wf, fb)


def _ssm_kernel(u_ref, m_ref, ws_ref, wct_ref, a_ref, y_ref, s_ref, x_ref, xs_ref, yl_ref):
    nt = SSM_S // 2 // LANES
    tiles = range(nt)
    z = jnp.concatenate(
        [u_ref[pl.ds(s, SSM_R, stride=SSM_CHUNK), :].astype(BF16) for s in range(SSM_CHUNK)], axis=1)
    s = _dot(z, ws_ref[0])
    seg_rows = lambda g: pl.ds(g, SSM_SEG, stride=SUBLANES)
    for c in range(2 * nt):
        for g in range(SUBLANES):
            s_ref[c, seg_rows(g), :] = s[g * SSM_SEG:(g + 1) * SSM_SEG, c * LANES:(c + 1) * LANES]
    mxu_n = 2 * LANES
    for c in range(SSM_W // mxu_n):
        hi = (c + 1) * mxu_n
        yl_ref[:, c * mxu_n:hi] = _dot(z[:, 0:hi], m_ref[0, 0:hi, c * mxu_n:hi])

    lane_tiles = lambda row: [a_ref[0, row:row + 1, c * LANES:(c + 1) * LANES] for c in tiles]
    full = lambda rows: [jnp.broadcast_to(r, (SUBLANES, LANES)) for r in rows]
    ar, ai = full(lane_tiles(0)), full(lane_tiles(1))
    gr, gi = lane_tiles(2), lane_tiles(3)

    def local(k, carry):
        cr, ci = carry
        rows = pl.ds(pl.multiple_of(k * SUBLANES, SUBLANES), SUBLANES)
        nr, ni = [], []
        for c in tiles:
            x_ref[c, rows, :] = cr[c]
            x_ref[nt + c, rows, :] = ci[c]
            nr.append(ar[c] * cr[c] - ai[c] * ci[c] + s_ref[c, rows, :])
            ni.append(ar[c] * ci[c] + ai[c] * cr[c] + s_ref[nt + c, rows, :])
        return tuple(nr), tuple(ni)

    zero = tuple(jnp.zeros((SUBLANES, LANES), F32) for _ in tiles)
    er, ei = lax.fori_loop(0, SSM_SEG, local, (zero, zero))

    for c in tiles:
        cr = jnp.zeros((1, LANES), F32)
        ci = jnp.zeros((1, LANES), F32)
        for g in range(SUBLANES):
            xs_ref[c, g:g + 1, :] = cr
            xs_ref[nt + c, g:g + 1, :] = ci
            cr, ci = (gr[c] * cr - gi[c] * ci + er[c][g:g + 1], gr[c] * ci + gi[c] * cr + ei[c][g:g + 1])
    xr = [xs_ref[c] for c in tiles]
    xi = [xs_ref[nt + c] for c in tiles]

    def fix(k, carry):
        pr, pi = carry
        rows = pl.ds(pl.multiple_of(k * SUBLANES, SUBLANES), SUBLANES)
        nr, ni = [], []
        for c in tiles:
            x_ref[c, rows, :] += pr[c] * xr[c] - pi[c] * xi[c]
            x_ref[nt + c, rows, :] += pr[c] * xi[c] + pi[c] * xr[c]
            nr.append(pr[c] * ar[c] - pi[c] * ai[c])
            ni.append(pr[c] * ai[c] + pi[c] * ar[c])
        return tuple(nr), tuple(ni)

    one = tuple(jnp.ones((SUBLANES, LANES), F32) for _ in tiles)
    lax.fori_loop(0, SSM_SEG, fix, (one, zero))

    x = jnp.concatenate(
        [jnp.concatenate([x_ref[c, seg_rows(g), :] for g in range(SUBLANES)], axis=0).astype(BF16)
         for c in range(2 * nt)], axis=1)
    y = yl_ref[...] + lax.dot_general(x, wct_ref[0], NT_DIMS, preferred_element_type=F32)
    for t in range(SSM_CHUNK):
        y_ref[pl.ds(t, SSM_R, stride=SSM_CHUNK), :] = y[:, t * LANES:(t + 1) * LANES]


def _ssm(u, m, ws, wct, apow):
    nt = TOKENS // SSM_TM
    return pl.pallas_call(
        _ssm_kernel,
        grid=(SSM_NB, nt),
        in_specs=[
            pl.BlockSpec((SSM_TM, LANES), lambda j, i: (i, j)),
            pl.BlockSpec((1, SSM_W, SSM_W), lambda j, i: (j, 0, 0)),
            pl.BlockSpec((1, SSM_W, SSM_S), lambda j, i: (j, 0, 0)),
            pl.BlockSpec((1, SSM_W, SSM_S), lambda j, i: (j, 0, 0)),
            pl.BlockSpec((1, 4, SSM_S // 2), lambda j, i: (j, 0, 0)),
        ],
        out_specs=pl.BlockSpec((SSM_TM, LANES), lambda j, i: (i, j)),
        out_shape=jax.ShapeDtypeStruct((TOKENS, D_SSM), F32),
        scratch_shapes=[pltpu.VMEM((SSM_S // LANES, SSM_R, LANES), F32),
                        pltpu.VMEM((SSM_S // LANES, SSM_R, LANES), F32),
                        pltpu.VMEM((SSM_S // LANES, SUBLANES, LANES), F32),
                        pltpu.VMEM((SSM_R, SSM_W), F32)],
        compiler_params=pltpu.CompilerParams(
            dimension_semantics=("arbitrary", "arbitrary"), vmem_limit_bytes=48 * 1024 * 1024),
        name="ssm_scan",
    )(u, m, ws, wct, apow)


def _split2(x):
    hi = x.astype(BF16)
    return hi, (x - hi.astype(F32)).astype(BF16)


def _dot_nt_f32(a, b):
    ah, al = _split2(a)
    bh, bl = _split2(b)
    d = lambda x, y: lax.dot_general(x, y, NT_DIMS, preferred_element_type=F32)
    return d(ah, bh) + d(ah, bl) + d(al, bh)


def _ssm_prep_kernel(lam_ref, b_ref, c_ref, d_ref, m_ref, ws_ref, wct_ref, a_ref):
    half = SSM_S // 2
    tc = SSM_CHUNK
    ldt, lre, lim = lam_ref[0, 0:1, :], lam_ref[0, 1:2, :], lam_ref[0, 2:3, :]
    dt = jnp.exp(ldt)
    tau = lax.broadcasted_iota(jnp.int32, (2 * SUBLANES, half), 0).astype(F32)
    pmag = jnp.exp(tau * (lre * dt))
    p_re = pmag * jnp.cos(tau * (lim * dt))
    p_im = pmag * jnp.sin(tau * (lim * dt))
    a_re, a_im = p_re[1:2], p_im[1:2]
    den = lre * lre + lim * lim
    nr = a_re - 1.0
    s_re = (nr * lre + a_im * lim) / den
    s_im = (a_im * lre - nr * lim) / den
    b_re, b_im = b_ref[0, 0], b_ref[0, 1]
    bb_re = s_re * b_re - s_im * b_im
    bb_im = s_re * b_im + s_im * b_re
    c_re, c_im = c_ref[0, 0], c_ref[0, 1]
    eye = (lax.broadcasted_iota(jnp.int32, (LANES, LANES), 0)
           == lax.broadcasted_iota(jnp.int32, (LANES, LANES), 1))

    m_ref[...] = jnp.zeros(m_ref.shape, BF16)
    for lag in range(tc):
        pr, pi = p_re[lag:lag + 1], p_im[lag:lag + 1]
        l_re = pr * bb_re - pi * bb_im
        l_im = pr * bb_im + pi * bb_re
        rows = slice((tc - 1 - lag) * LANES, (tc - lag) * LANES)
        ws_ref[0, rows, 0:half] = l_re.astype(BF16)
        ws_ref[0, rows, half:SSM_S] = l_im.astype(BF16)
        x = _dot_nt_f32(l_re, c_re) - _dot_nt_f32(l_im, c_im)
        if lag == 0:
            x = x + jnp.where(eye, d_ref[0], 0.0)
        x = x.astype(BF16)
        for s in range(tc - lag):
            m_ref[0, s * LANES:(s + 1) * LANES, (s + lag) * LANES:(s + lag + 1) * LANES] = x
    for t in range(tc):
        pr, pi = p_re[t + 1:t + 2], p_im[t + 1:t + 2]
        rows = slice(t * LANES, (t + 1) * LANES)
        wct_ref[0, rows, 0:half] = (c_re * pr - c_im * pi).astype(BF16)
        wct_ref[0, rows, half:SSM_S] = (-(c_re * pi + c_im * pr)).astype(BF16)

    qr, qi = p_re[tc:tc + 1], p_im[tc:tc + 1]
    a_ref[0, 0:1, :] = qr
    a_ref[0, 1:2, :] = qi
    for _ in range(SSM_SEG.bit_length() - 1):
        qr, qi = qr * qr - qi * qi, 2.0 * qr * qi
    a_ref[0, 2:3, :] = qr
    a_ref[0, 3:4, :] = qi


def _ssm_prep(lam, bexp, cexp, dexp):
    blk = lambda *shape: pl.BlockSpec((1,) + shape, lambda j: (j,) + (0,) * len(shape))
    big = jax.ShapeDtypeStruct((SSM_NB, SSM_W, SSM_S), BF16)
    return pl.pallas_call(
        _ssm_prep_kernel,
        grid=(SSM_NB,),
        in_specs=[blk(3, SSM_S // 2), blk(2, LANES, SSM_S // 2), blk(2, LANES, SSM_S // 2), blk(1, LANES)],
        out_specs=[blk(SSM_W, SSM_W), blk(SSM_W, SSM_S), blk(SSM_W, SSM_S), blk(4, SSM_S // 2)],
        out_shape=[big, big, big, jax.ShapeDtypeStruct((SSM_NB, 4, SSM_S // 2), F32)],
        compiler_params=pltpu.CompilerParams(
            dimension_semantics=("arbitrary",), vmem_limit_bytes=48 * 1024 * 1024),
        name="ssm_prep",
    )(lam, bexp, cexp, dexp)


def _ssm_operands(log_dt, lam_re, lam_im, b_re, b_im, c_re, c_im, d_skip):
    nb, gb = SSM_NB, SSM_GB
    eye_g = jnp.eye(gb, dtype=F32)
    lam = jnp.stack([jnp.broadcast_to(log_dt[:, None], (SSM_G, SSM_N)), lam_re, lam_im], axis=0)
    lam = lam.reshape(3, nb, gb * SSM_N).transpose(1, 0, 2)

    def expand_b(b):
        w = b.reshape(nb, gb, SSM_N, SSM_P).transpose(0, 1, 3, 2)
        return jnp.einsum('jgqn,gh->jgqhn', w, eye_g).reshape(nb, LANES, gb * SSM_N)

    def expand_c(c):
        w = c.reshape(nb, gb, SSM_P, SSM_N)
        return jnp.einsum('jgpn,gh->jgphn', w, eye_g).reshape(nb, LANES, gb * SSM_N)

    bexp = jnp.stack([expand_b(b_re), expand_b(b_im)], axis=1)
    cexp = jnp.stack([expand_c(c_re), expand_c(c_im)], axis=1)
    return lam, bexp, cexp, d_skip.reshape(nb, 1, LANES)


def _attn_kernel(k_ref, kb_ref, qt_ref, vt_ref, o_ref, rhs_ref, sa_ref, sb_ref, xa_ref, xb_ref,
                 acc_ref, m_ref):
    p = pl.program_id(1)
    row = lax.broadcasted_iota(jnp.int32, (LANES, TQ), 0)
    wide_row = lax.broadcasted_iota(jnp.int32, (LANES, QW), 0)
    in_head = []
    for hh in range(2):
        h = 2 * p + hh
        in_head.append((row >= hh * HEAD_DIM) & (row < (hh + 1) * HEAD_DIM))
        pick = (wide_row == h) | (wide_row == h + N_HEADS) | (wide_row == h + 2 * N_HEADS)
        rhs_ref[hh, LANES:2 * LANES, :] = jnp.where(pick, 1.0, 0.0).astype(BF16)

    key = lax.broadcasted_iota(jnp.int32, (TQ, QW), 0)
    qry = lax.broadcasted_iota(jnp.int32, (TQ, QW), 1)
    causal = key <= qry
    ones_row = jnp.where(lax.broadcasted_iota(jnp.int32, (V_PAD, TQ), 0) == 0, 1.0, 0.0).astype(BF16)
    both = (0, 1)
    full = slice(0, QW)
    upper = slice(TQ, QW)

    def scores(j, s_ref, x_ref, heads, lanes=full):
        rows = pl.ds(pl.multiple_of(j * TQ, TQ), TQ)
        lhs = jnp.concatenate([k_ref[rows, :], kb_ref[rows, :]], axis=1)
        for hh in heads:
            st = _dot(lhs, rhs_ref[hh, :, lanes])
            s_ref[hh, :, lanes] = st
            x_ref[hh, :, lanes] = jnp.max(st, axis=0, keepdims=True)

    def update(j, s_ref, x_ref, mask, heads, lanes=full):
        for hh in heads:
            st = s_ref[hh, :, lanes]
            if mask is None:
                blk_max = x_ref[hh, :, lanes]
            else:
                st = jnp.where(mask, st, NEG_BIG)
                blk_max = jnp.max(st, axis=0, keepdims=True)
            m_prev = m_ref[hh, :, lanes]
            m_next = jnp.maximum(m_prev, blk_max)
            alpha = jnp.exp2(m_prev - m_next)
            pt = jnp.exp2(st - m_next).astype(BF16)
            vt = jnp.concatenate([vt_ref[j, hh * HEAD_DIM:(hh + 1) * HEAD_DIM, :], ones_row], axis=0)
            acc_ref[hh, :, lanes] = alpha * acc_ref[hh, :, lanes] + _dot(vt, pt)
            m_ref[hh, :, lanes] = m_next

    def body(jj, carry):
        j = 2 * jj
        for hh in both:
            scores(j + 1, sb_ref, xb_ref, (hh,))
            update(j, sa_ref, xa_ref, None, (hh,))
        for hh in both:
            scores(j + 2, sa_ref, xa_ref, (hh,))
            update(j + 1, sb_ref, xb_ref, None, (hh,))
        return carry

    def tile(i, carry):
        for t in range(QW // TQ):
            qt = qt_ref[(QW // TQ) * i + t]
            for hh in both:
                rhs_ref[hh, 0:LANES, t * TQ:(t + 1) * TQ] = jnp.where(in_head[hh], qt, jnp.zeros_like(qt))
        m_ref[...] = jnp.full(m_ref.shape, NEG_BIG, F32)
        acc_ref[...] = jnp.zeros(acc_ref.shape, F32)

        scores(0, sa_ref, xa_ref, both)
        lax.fori_loop(0, i, body, 0)
        for hh in both:
            scores(2 * i + 1, sb_ref, xb_ref, (hh,), upper)
            update(2 * i, sa_ref, xa_ref, causal, (hh,))
        update(2 * i + 1, sb_ref, xb_ref, causal[:, 0:TQ], both, upper)

        for hh in both:
            acc = acc_ref[hh]
            out = (acc[0:HEAD_DIM] / acc[HEAD_DIM:HEAD_DIM + 1]).astype(BF16)
            for t in range(QW // TQ):
                o_ref[(QW // TQ) * i + t, hh * HEAD_DIM:(hh + 1) * HEAD_DIM, :] = out[:, t * TQ:(t + 1) * TQ]
        return carry

    lax.fori_loop(0, SEQ // QW, tile, 0)


def _attention(k, kb, qt, vt):
    nq = SEQ // TQ
    return pl.pallas_call(
        _attn_kernel,
        grid=(BATCH, N_HEADS // 2),
        in_specs=[
            pl.BlockSpec((SEQ, LANES), lambda b, p: (b, p)),
            pl.BlockSpec((SEQ, LANES), lambda b, p: (b, 0)),
            pl.BlockSpec((nq, LANES, TQ), lambda b, p: (b, p, 0)),
            pl.BlockSpec((nq, LANES, TQ), lambda b, p: (b, p, 0)),
        ],
        out_specs=pl.BlockSpec((nq, LANES, TQ), lambda b, p: (b, p, 0)),
        out_shape=jax.ShapeDtypeStruct((TOKENS // TQ, D_ATTN, TQ), BF16),
        scratch_shapes=[pltpu.VMEM((2, 2 * LANES, QW), BF16),
                        pltpu.VMEM((2, TQ, QW), F32),
                        pltpu.VMEM((2, TQ, QW), F32),
                        pltpu.VMEM((2, 1, QW), F32),
                        pltpu.VMEM((2, 1, QW), F32),
                        pltpu.VMEM((2, HEAD_DIM + V_PAD, QW), F32),
                        pltpu.VMEM((2, 1, QW), F32)],
        compiler_params=pltpu.CompilerParams(
            dimension_semantics=("arbitrary", "arbitrary"), vmem_limit_bytes=48 * 1024 * 1024),
        name="fox_attention",
    )(k, kb, qt, vt)


def _mlp_kernel(apply_final, h_ref, ys_ref, oa_ref, gw_ref, gb_ref, gs_ref, ga_ref, wo_ref,
                l2_ref, wu_ref, wd_ref, fg_ref, o_ref):
    g = jax.nn.gelu(ys_ref[...])
    gate = 1.0 / (1.0 + jnp.exp(-(_dot(g.astype(BF16), gw_ref[...]) + gb_ref[...])))
    n_ssm = _rms(g * gate, gs_ref[...]).astype(BF16)
    oa = jnp.concatenate([oa_ref[t] for t in range(TM_MLP // TQ)], axis=1).astype(F32)
    inv = lax.rsqrt(jnp.mean(oa * oa, axis=0, keepdims=True) + EPS)
    n_att = (oa * inv * ga_ref[...]).astype(BF16)
    h1 = (h_ref[...] + _dot(n_ssm, wo_ref[0:D_SSM, :])
          + lax.dot_general(n_att, wo_ref[D_SSM:D_MODEL, :], TN_DIMS, preferred_element_type=F32))
    xn = _rms(h1, l2_ref[...]).astype(BF16)
    o_ref[...] = h1
    for c in range(D_FF // TF_MLP):
        cols = slice(c * TF_MLP, (c + 1) * TF_MLP)
        a = jnp.maximum(_dot(xn, wu_ref[:, cols]), 0.0)
        o_ref[...] += _dot((a * a).astype(BF16), wd_ref[cols, :])
    if apply_final:
        o_ref[...] = _rms(o_ref[...], fg_ref[...])


def _mlp(h, y_ssm, o_att, glu_w, glu_b, gn_s, gn_a, w_out, ln2, w_up, w_down, final_g, apply_final):
    nm = TOKENS // TM_MLP
    row = lambda i: (i, 0)
    fixed = lambda *shape: pl.BlockSpec(shape, lambda i: (0, 0), pipeline_mode=pl.Buffered(1))
    return pl.pallas_call(
        functools.partial(_mlp_kernel, apply_final),
        grid=(nm,),
        in_specs=[
            pl.BlockSpec((TM_MLP, D_MODEL), row),
            pl.BlockSpec((TM_MLP, D_SSM), row),
            pl.BlockSpec((TM_MLP // TQ, D_ATTN, TQ), lambda i: (i, 0, 0)),
            fixed(D_SSM, D_SSM),
            fixed(1, D_SSM),
            fixed(1, D_SSM),
            fixed(D_ATTN, 1),
            fixed(D_MODEL, D_MODEL),
            fixed(1, D_MODEL),
            fixed(D_MODEL, D_FF),
            fixed(D_FF, D_MODEL),
            fixed(1, D_MODEL),
        ],
        out_specs=pl.BlockSpec((TM_MLP, D_MODEL), row),
        out_shape=jax.ShapeDtypeStruct((TOKENS, D_MODEL), F32),
        compiler_params=pltpu.CompilerParams(
            dimension_semantics=("arbitrary",), vmem_limit_bytes=56 * 1024 * 1024),
        name="mixer_out_mlp",
    )(h, y_ssm, o_att, glu_w, glu_b, gn_s, gn_a, w_out, ln2, w_up, w_down, final_g)


def kernel(x, ln1_g, w_in, ssm_log_dt, ssm_lambda_re, ssm_lambda_im, ssm_b_re, ssm_b_im, ssm_c_re, ssm_c_im, ssm_d, glu_w, glu_b, fgate_b, gn_ssm_g, gn_attn_g, w_out, ln2_g, w_up, w_down, final_g):
    assert TM_MLP % TQ == 0 and TM_IN % TQ == 0
    h = x.reshape(TOKENS, D_MODEL)
    for l in range(DEPTH):
        w = w_in[l]
        w_u = w[:, :D_SSM].astype(BF16)
        w_qt = w[:, D_SSM:D_SSM + D_ATTN].T.astype(BF16)
        w_k = w[:, D_SSM + D_ATTN:D_SSM + 2 * D_ATTN].astype(BF16)
        w_vt = w[:, D_SSM + 2 * D_ATTN:D_SSM + 3 * D_ATTN].T.astype(BF16)
        gate_cols = jnp.concatenate([w[:, 4 * D_SSM:]] * 3, axis=1)
        w_gate = jnp.pad(gate_cols, ((0, 0), (0, LANES - 3 * N_HEADS))).astype(BF16)
        b_gate = jnp.pad(jnp.concatenate([fgate_b[l]] * 3), (0, LANES - 3 * N_HEADS)).reshape(1, LANES)

        u, k, kb, qt, vt = _inproj(h, ln1_g[l].reshape(1, D_MODEL), w_u, w_k, w_qt, w_vt, w_gate, b_gate)

        mats = _ssm_prep(*_ssm_operands(ssm_log_dt[l], ssm_lambda_re[l], ssm_lambda_im[l], ssm_b_re[l],
                                        ssm_b_im[l], ssm_c_re[l], ssm_c_im[l], ssm_d[l]))
        y_ssm = _ssm(u, *mats)
        o_att = _attention(k, kb, qt, vt)

        h = _mlp(h, y_ssm, o_att, glu_w[l].astype(BF16), glu_b[l].reshape(1, D_SSM),
                 gn_ssm_g[l].reshape(1, D_SSM), gn_attn_g[l].reshape(D_ATTN, 1),
                 w_out[l].astype(BF16), ln2_g[l].reshape(1, D_MODEL), w_up[l].astype(BF16),
                 w_down[l].astype(BF16), final_g.reshape(1, D_MODEL), l == DEPTH - 1)
    return h.reshape(BATCH, SEQ, D_MODEL)
```

```python
import functools
import math

import jax
import jax.numpy as jnp
from jax import lax
from jax.experimental import pallas as pl
from jax.experimental.pallas import tpu as pltpu

F32 = jnp.float32
BF16 = jnp.bfloat16

D_MODEL = 1024
BATCH = 4
SEQ = 4096
DEPTH = 2
TOKENS = BATCH * SEQ
D_SSM = 512
D_ATTN = 512
SSM_P = 16
SSM_G = 32
SSM_N = 64
HEAD_DIM = 64
N_HEADS = 8
D_FF = 4096
EPS = 1e-6
LOG2E = 1.4426950408889634

LANES = 128
SUBLANES = 8

SSM_CHUNK = SUBLANES
SSM_GB = LANES // SSM_P
SSM_NB = D_SSM // LANES
SSM_W = SSM_CHUNK * LANES
SSM_S = 2 * SSM_GB * SSM_N
SSM_TM = SEQ
SSM_R = SSM_TM // SSM_CHUNK
SSM_SEG = SSM_R // SUBLANES

TM_IN = 1024
TQ = 512
QW = 2 * TQ
TM_MLP = 1024
TF_MLP = 1024
CS_BLK = LANES
V_PAD = 16
NEG_BIG = -1e30

NT_DIMS = (((1,), (1,)), ((), ()))
TN_DIMS = (((0,), (0,)), ((), ()))


def _rms(x, g):
    return x * lax.rsqrt(jnp.mean(x * x, axis=-1, keepdims=True) + EPS) * g


def _dot(a, b):
    return jnp.dot(a, b, preferred_element_type=F32)


def _split3(x):
    hi = x.astype(BF16)
    r1 = x - hi.astype(F32)
    mid = r1.astype(BF16)
    low = (r1 - mid.astype(F32)).astype(BF16)
    return hi, mid, low


def _inproj_kernel(x_ref, g_ref, wu_ref, wk_ref, wqt_ref, wvt_ref, wf_ref, fb_ref,
                   u_ref, k_ref, kb_ref, qt_ref, vt_ref, carry_ref):
    i = pl.program_id(0)
    xn = _rms(x_ref[...], g_ref[...]).astype(BF16)
    u_ref[...] = _dot(xn, wu_ref[...])
    k_ref[...] = _dot(xn, wk_ref[...]).astype(BF16)
    qscale = LOG2E / math.sqrt(HEAD_DIM)
    qt = lax.dot_general(wqt_ref[...], xn, NT_DIMS, preferred_element_type=F32)
    vt = lax.dot_general(wvt_ref[...], xn, NT_DIMS, preferred_element_type=F32)
    for t in range(TM_IN // TQ):
        qt_ref[t] = (qt[:, t * TQ:(t + 1) * TQ] * qscale).astype(BF16)
        vt_ref[t] = vt[:, t * TQ:(t + 1) * TQ].astype(BF16)

    f = _dot(xn, wf_ref[...]) + fb_ref[...]
    lf = jnp.minimum(f, 0.0) - jnp.log1p(jnp.exp(-jnp.abs(f)))

    @pl.when(i % (SEQ // TM_IN) == 0)
    def _():
        carry_ref[...] = jnp.zeros(carry_ref.shape, F32)

    r = lax.broadcasted_iota(jnp.int32, (CS_BLK, CS_BLK), 0)
    c = lax.broadcasted_iota(jnp.int32, (CS_BLK, CS_BLK), 1)
    tril = jnp.where(c <= r, 1.0, 0.0).astype(BF16)
    lane = lax.broadcasted_iota(jnp.int32, (CS_BLK, LANES), 1)
    carry = carry_ref[...]
    for b in range(TM_IN // CS_BLK):
        hi, mid, low = _split3(lf[b * CS_BLK:(b + 1) * CS_BLK])
        cs = _dot(tril, hi) + _dot(tril, mid) + _dot(tril, low) + carry
        carry = cs[CS_BLK - 1:CS_BLK, :]
        p0, p1, p2 = _split3(cs * (-LOG2E))
        piece = jnp.where(lane < N_HEADS, p0, jnp.where(lane < 2 * N_HEADS, p1, p2))
        kb_ref[b * CS_BLK:(b + 1) * CS_BLK, :] = jnp.where(
            lane < 3 * N_HEADS, piece, jnp.zeros_like(piece))
    carry_ref[...] = carry


def _inproj(h, g, wu, wk, wqt, wvt, wf, fb):
    n = TOKENS // TM_IN
    fixed = lambda i: (0, 0)
    return pl.pallas_call(
        _inproj_kernel,
        grid=(n,),
        in_specs=[
            pl.BlockSpec((TM_IN, D_MODEL), lambda i: (i, 0)),
            pl.BlockSpec((1, D_MODEL), fixed),
            pl.BlockSpec((D_MODEL, D_SSM), fixed),
            pl.BlockSpec((D_MODEL, D_ATTN), fixed),
            pl.BlockSpec((D_ATTN, D_MODEL), fixed),
            pl.BlockSpec((D_ATTN, D_MODEL), fixed),
            pl.BlockSpec((D_MODEL, LANES), fixed),
            pl.BlockSpec((1, LANES), fixed),
        ],
        out_specs=[
            pl.BlockSpec((TM_IN, D_SSM), lambda i: (i, 0)),
            pl.BlockSpec((TM_IN, D_ATTN), lambda i: (i, 0)),
            pl.BlockSpec((TM_IN, LANES), lambda i: (i, 0)),
            pl.BlockSpec((TM_IN // TQ, D_ATTN, TQ), lambda i: (i, 0, 0)),
            pl.BlockSpec((TM_IN // TQ, D_ATTN, TQ), lambda i: (i, 0, 0)),
        ],
        out_shape=[
            jax.ShapeDtypeStruct((TOKENS, D_SSM), F32),
            jax.ShapeDtypeStruct((TOKENS, D_ATTN), BF16),
            jax.ShapeDtypeStruct((TOKENS, LANES), BF16),
            jax.ShapeDtypeStruct((TOKENS // TQ, D_ATTN, TQ), BF16),
            jax.ShapeDtypeStruct((TOKENS // TQ, D_ATTN, TQ), BF16),
        ],
        scratch_shapes=[pltpu.VMEM((1, LANES), F32)],
        compiler_params=pltpu.CompilerParams(
            dimension_semantics=("arbitrary",), vmem_limit_bytes=48 * 1024 * 1024),
        name="inproj",
    )(h, g, wu, wk, wqt, wvt, wf, fb)


def _ssm_kernel(u_ref, m_ref, ws_ref, wct_ref, a_ref, y_ref, s_ref, x_ref, xs_ref, yl_ref):
    nt = SSM_S // 2 // LANES
    tiles = range(nt)
    z = jnp.concatenate(
        [u_ref[pl.ds(s, SSM_R, stride=SSM_CHUNK), :].astype(BF16) for s in range(SSM_CHUNK)], axis=1)
    s = _dot(z, ws_ref[0])
    seg_rows = lambda g: pl.ds(g, SSM_SEG, stride=SUBLANES)
    for c in range(2 * nt):
        for g in range(SUBLANES):
            s_ref[c, seg_rows(g), :] = s[g * SSM_SEG:(g + 1) * SSM_SEG, c * LANES:(c + 1) * LANES]
    mxu_n = 2 * LANES
    for c in range(SSM_W // mxu_n):
        hi = (c + 1) * mxu_n
        yl_ref[:, c * mxu_n:hi] = _dot(z[:, 0:hi], m_ref[0, 0:hi, c * mxu_n:hi])

    lane_tiles = lambda row: [a_ref[0, row:row + 1, c * LANES:(c + 1) * LANES] for c in tiles]
    full = lambda rows: [jnp.broadcast_to(r, (SUBLANES, LANES)) for r in rows]
    ar, ai = full(lane_tiles(0)), full(lane_tiles(1))
    gr, gi = lane_tiles(2), lane_tiles(3)

    def local(k, carry):
        cr, ci = carry
        rows = pl.ds(pl.multiple_of(k * SUBLANES, SUBLANES), SUBLANES)
        nr, ni = [], []
        for c in tiles:
            x_ref[c, rows, :] = cr[c]
            x_ref[nt + c, rows, :] = ci[c]
            nr.append(ar[c] * cr[c] - ai[c] * ci[c] + s_ref[c, rows, :])
            ni.append(ar[c] * ci[c] + ai[c] * cr[c] + s_ref[nt + c, rows, :])
        return tuple(nr), tuple(ni)

    zero = tuple(jnp.zeros((SUBLANES, LANES), F32) for _ in tiles)
    er, ei = lax.fori_loop(0, SSM_SEG, local, (zero, zero))

    for c in tiles:
        cr = jnp.zeros((1, LANES), F32)
        ci = jnp.zeros((1, LANES), F32)
        for g in range(SUBLANES):
            xs_ref[c, g:g + 1, :] = cr
            xs_ref[nt + c, g:g + 1, :] = ci
            cr, ci = (gr[c] * cr - gi[c] * ci + er[c][g:g + 1], gr[c] * ci + gi[c] * cr + ei[c][g:g + 1])
    xr = [xs_ref[c] for c in tiles]
    xi = [xs_ref[nt + c] for c in tiles]

    def fix(k, carry):
        pr, pi = carry
        rows = pl.ds(pl.multiple_of(k * SUBLANES, SUBLANES), SUBLANES)
        nr, ni = [], []
        for c in tiles:
            x_ref[c, rows, :] += pr[c] * xr[c] - pi[c] * xi[c]
            x_ref[nt + c, rows, :] += pr[c] * xi[c] + pi[c] * xr[c]
            nr.append(pr[c] * ar[c] - pi[c] * ai[c])
            ni.append(pr[c] * ai[c] + pi[c] * ar[c])
        return tuple(nr), tuple(ni)

    one = tuple(jnp.ones((SUBLANES, LANES), F32) for _ in tiles)
    lax.fori_loop(0, SSM_SEG, fix, (one, zero))

    x = jnp.concatenate(
        [jnp.concatenate([x_ref[c, seg_rows(g), :] for g in range(SUBLANES)], axis=0).astype(BF16)
         for c in range(2 * nt)], axis=1)
    y = yl_ref[...] + lax.dot_general(x, wct_ref[0], NT_DIMS, preferred_element_type=F32)
    for t in range(SSM_CHUNK):
        y_ref[pl.ds(t, SSM_R, stride=SSM_CHUNK), :] = y[:, t * LANES:(t + 1) * LANES]


def _ssm(u, m, ws, wct, apow):
    nt = TOKENS // SSM_TM
    return pl.pallas_call(
        _ssm_kernel,
        grid=(SSM_NB, nt),
        in_specs=[
            pl.BlockSpec((SSM_TM, LANES), lambda j, i: (i, j)),
            pl.BlockSpec((1, SSM_W, SSM_W), lambda j, i: (j, 0, 0)),
            pl.BlockSpec((1, SSM_W, SSM_S), lambda j, i: (j, 0, 0)),
            pl.BlockSpec((1, SSM_W, SSM_S), lambda j, i: (j, 0, 0)),
            pl.BlockSpec((1, 4, SSM_S // 2), lambda j, i: (j, 0, 0)),
        ],
        out_specs=pl.BlockSpec((SSM_TM, LANES), lambda j, i: (i, j)),
        out_shape=jax.ShapeDtypeStruct((TOKENS, D_SSM), F32),
        scratch_shapes=[pltpu.VMEM((SSM_S // LANES, SSM_R, LANES), F32),
                        pltpu.VMEM((SSM_S // LANES, SSM_R, LANES), F32),
                        pltpu.VMEM((SSM_S // LANES, SUBLANES, LANES), F32),
                        pltpu.VMEM((SSM_R, SSM_W), F32)],
        compiler_params=pltpu.CompilerParams(
            dimension_semantics=("arbitrary", "arbitrary"), vmem_limit_bytes=48 * 1024 * 1024),
        name="ssm_scan",
    )(u, m, ws, wct, apow)


def _split2(x):
    hi = x.astype(BF16)
    return hi, (x - hi.astype(F32)).astype(BF16)


def _dot_nt_f32(a, b):
    ah, al = _split2(a)
    bh, bl = _split2(b)
    d = lambda x, y: lax.dot_general(x, y, NT_DIMS, preferred_element_type=F32)
    return d(ah, bh) + d(ah, bl) + d(al, bh)


def _ssm_prep_kernel(lam_ref, b_ref, c_ref, d_ref, m_ref, ws_ref, wct_ref, a_ref):
    half = SSM_S // 2
    tc = SSM_CHUNK
    ldt, lre, lim = lam_ref[0, 0:1, :], lam_ref[0, 1:2, :], lam_ref[0, 2:3, :]
    dt = jnp.exp(ldt)
    tau = lax.broadcasted_iota(jnp.int32, (2 * SUBLANES, half), 0).astype(F32)
    pmag = jnp.exp(tau * (lre * dt))
    p_re = pmag * jnp.cos(tau * (lim * dt))
    p_im = pmag * jnp.sin(tau * (lim * dt))
    a_re, a_im = p_re[1:2], p_im[1:2]
    den = lre * lre + lim * lim
    nr = a_re - 1.0
    s_re = (nr * lre + a_im * lim) / den
    s_im = (a_im * lre - nr * lim) / den
    b_re, b_im = b_ref[0, 0], b_ref[0, 1]
    bb_re = s_re * b_re - s_im * b_im
    bb_im = s_re * b_im + s_im * b_re
    c_re, c_im = c_ref[0, 0], c_ref[0, 1]
    eye = (lax.broadcasted_iota(jnp.int32, (LANES, LANES), 0)
           == lax.broadcasted_iota(jnp.int32, (LANES, LANES), 1))

    m_ref[...] = jnp.zeros(m_ref.shape, BF16)
    for lag in range(tc):
        pr, pi = p_re[lag:lag + 1], p_im[lag:lag + 1]
        l_re = pr * bb_re - pi * bb_im
        l_im = pr * bb_im + pi * bb_re
        rows = slice((tc - 1 - lag) * LANES, (tc - lag) * LANES)
        ws_ref[0, rows, 0:half] = l_re.astype(BF16)
        ws_ref[0, rows, half:SSM_S] = l_im.astype(BF16)
        x = _dot_nt_f32(l_re, c_re) - _dot_nt_f32(l_im, c_im)
        if lag == 0:
            x = x + jnp.where(eye, d_ref[0], 0.0)
        x = x.astype(BF16)
        for s in range(tc - lag):
            m_ref[0, s * LANES:(s + 1) * LANES, (s + lag) * LANES:(s + lag + 1) * LANES] = x
    for t in range(tc):
        pr, pi = p_re[t + 1:t + 2], p_im[t + 1:t + 2]
        rows = slice(t * LANES, (t + 1) * LANES)
        wct_ref[0, rows, 0:half] = (c_re * pr - c_im * pi).astype(BF16)
        wct_ref[0, rows, half:SSM_S] = (-(c_re * pi + c_im * pr)).astype(BF16)

    qr, qi = p_re[tc:tc + 1], p_im[tc:tc + 1]
    a_ref[0, 0:1, :] = qr
    a_ref[0, 1:2, :] = qi
    for _ in range(SSM_SEG.bit_length() - 1):
        qr, qi = qr * qr - qi * qi, 2.0 * qr * qi
    a_ref[0, 2:3, :] = qr
    a_ref[0, 3:4, :] = qi


def _ssm_prep(lam, bexp, cexp, dexp):
    blk = lambda *shape: pl.BlockSpec((1,) + shape, lambda j: (j,) + (0,) * len(shape))
    big = jax.ShapeDtypeStruct((SSM_NB, SSM_W, SSM_S), BF16)
    return pl.pallas_call(
        _ssm_prep_kernel,
        grid=(SSM_NB,),
        in_specs=[blk(3, SSM_S // 2), blk(2, LANES, SSM_S // 2), blk(2, LANES, SSM_S // 2), blk(1, LANES)],
        out_specs=[blk(SSM_W, SSM_W), blk(SSM_W, SSM_S), blk(SSM_W, SSM_S), blk(4, SSM_S // 2)],
        out_shape=[big, big, big, jax.ShapeDtypeStruct((SSM_NB, 4, SSM_S // 2), F32)],
        compiler_params=pltpu.CompilerParams(
            dimension_semantics=("arbitrary",), vmem_limit_bytes=48 * 1024 * 1024),
        name="ssm_prep",
    )(lam, bexp, cexp, dexp)


def _ssm_operands(log_dt, lam_re, lam_im, b_re, b_im, c_re, c_im, d_skip):
    nb, gb = SSM_NB, SSM_GB
    eye_g = jnp.eye(gb, dtype=F32)
    lam = jnp.stack([jnp.broadcast_to(log_dt[:, None], (SSM_G, SSM_N)), lam_re, lam_im], axis=0)
    lam = lam.reshape(3, nb, gb * SSM_N).transpose(1, 0, 2)

    def expand_b(b):
        w = b.reshape(nb, gb, SSM_N, SSM_P).transpose(0, 1, 3, 2)
        return jnp.einsum('jgqn,gh->jgqhn', w, eye_g).reshape(nb, LANES, gb * SSM_N)

    def expand_c(c):
        w = c.reshape(nb, gb, SSM_P, SSM_N)
        return jnp.einsum('jgpn,gh->jgphn', w, eye_g).reshape(nb, LANES, gb * SSM_N)

    bexp = jnp.stack([expand_b(b_re), expand_b(b_im)], axis=1)
    cexp = jnp.stack([expand_c(c_re), expand_c(c_im)], axis=1)
    return lam, bexp, cexp, d_skip.reshape(nb, 1, LANES)


def _attn_kernel(k_ref, kb_ref, qt_ref, vt_ref, o_ref, rhs_ref, sa_ref, sb_ref, xa_ref, xb_ref,
                 acc_ref, m_ref):
    p = pl.program_id(1)
    row = lax.broadcasted_iota(jnp.int32, (LANES, TQ), 0)
    wide_row = lax.broadcasted_iota(jnp.int32, (LANES, QW), 0)
    in_head = []
    for hh in range(2):
        h = 2 * p + hh
        in_head.append((row >= hh * HEAD_DIM) & (row < (hh + 1) * HEAD_DIM))
        pick = (wide_row == h) | (wide_row == h + N_HEADS) | (wide_row == h + 2 * N_HEADS)
        rhs_ref[hh, LANES:2 * LANES, :] = jnp.where(pick, 1.0, 0.0).astype(BF16)

    key = lax.broadcasted_iota(jnp.int32, (TQ, QW), 0)
    qry = lax.broadcasted_iota(jnp.int32, (TQ, QW), 1)
    causal = key <= qry
    ones_row = jnp.where(lax.broadcasted_iota(jnp.int32, (V_PAD, TQ), 0) == 0, 1.0, 0.0).astype(BF16)
    both = (0, 1)
    full = slice(0, QW)

    def scores(j, s_ref, x_ref, heads, lanes=full):
        rows = pl.ds(pl.multiple_of(j * TQ, TQ), TQ)
        lhs = jnp.concatenate([k_ref[rows, :], kb_ref[rows, :]], axis=1)
        for hh in heads:
            st = _dot(lhs, rhs_ref[hh, :, lanes])
            s_ref[hh, :, lanes] = st
            x_ref[hh, :, lanes] = jnp.max(st, axis=0, keepdims=True)

    def update(j, s_ref, x_ref, mask, heads, lanes=full):
        for hh in heads:
            st = s_ref[hh, :, lanes]
            if mask is None:
                blk_max = x_ref[hh, :, lanes]
            else:
                st = jnp.where(mask, st, NEG_BIG)
                blk_max = jnp.max(st, axis=0, keepdims=True)
            m_prev = m_ref[hh, :, lanes]
            m_next = jnp.maximum(m_prev, blk_max)
            alpha = jnp.exp2(m_prev - m_next)
            pt = jnp.exp2(st - m_next).astype(BF16)
            vt = jnp.concatenate([vt_ref[j, hh * HEAD_DIM:(hh + 1) * HEAD_DIM, :], ones_row], axis=0)
            acc_ref[hh, :, lanes] = alpha * acc_ref[hh, :, lanes] + _dot(vt, pt)
            m_ref[hh, :, lanes] = m_next

    def body(jj, carry):
        j = 2 * jj
        for hh in both:
            scores(j + 1, sb_ref, xb_ref, (hh,))
            update(j, sa_ref, xa_ref, None, (hh,))
        for hh in both:
            scores(j + 2, sa_ref, xa_ref, (hh,))
            update(j + 1, sb_ref, xb_ref, None, (hh,))
        return carry

    def tile(i, carry):
        for t in range(QW // TQ):
            qt = qt_ref[(QW // TQ) * i + t]
            for hh in both:
                rhs_ref[hh, 0:LANES, t * TQ:(t + 1) * TQ] = jnp.where(in_head[hh], qt, jnp.zeros_like(qt))
        m_ref[...] = jnp.full(m_ref.shape, NEG_BIG, F32)
        acc_ref[...] = jnp.zeros(acc_ref.shape, F32)

        scores(0, sa_ref, xa_ref, both)
        first = (QW // TQ) * i
        lax.fori_loop(0, first // 2, body, 0)
        bufs = ((sa_ref, xa_ref), (sb_ref, xb_ref))
        for d in range(QW // TQ):
            lanes = slice(d * TQ, QW)
            for hh in both:
                if d + 1 < QW // TQ:
                    scores(first + d + 1, *bufs[(d + 1) % 2], (hh,), slice((d + 1) * TQ, QW))
                update(first + d, *bufs[d % 2], causal[:, 0:QW - d * TQ], (hh,), lanes)

        for hh in both:
            acc = acc_ref[hh]
            out = (acc[0:HEAD_DIM] / acc[HEAD_DIM:HEAD_DIM + 1]).astype(BF16)
            for t in range(QW // TQ):
                o_ref[(QW // TQ) * i + t, hh * HEAD_DIM:(hh + 1) * HEAD_DIM, :] = out[:, t * TQ:(t + 1) * TQ]
        return carry

    lax.fori_loop(0, SEQ // QW, tile, 0)


def _attention(k, kb, qt, vt):
    nq = SEQ // TQ
    return pl.pallas_call(
        _attn_kernel,
        grid=(BATCH, N_HEADS // 2),
        in_specs=[
            pl.BlockSpec((SEQ, LANES), lambda b, p: (b, p)),
            pl.BlockSpec((SEQ, LANES), lambda b, p: (b, 0)),
            pl.BlockSpec((nq, LANES, TQ), lambda b, p: (b, p, 0)),
            pl.BlockSpec((nq, LANES, TQ), lambda b, p: (b, p, 0)),
        ],
        out_specs=pl.BlockSpec((nq, LANES, TQ), lambda b, p: (b, p, 0)),
        out_shape=jax.ShapeDtypeStruct((TOKENS // TQ, D_ATTN, TQ), BF16),
        scratch_shapes=[pltpu.VMEM((2, 2 * LANES, QW), BF16),
                        pltpu.VMEM((2, TQ, QW), F32),
                        pltpu.VMEM((2, TQ, QW), F32),
                        pltpu.VMEM((2, 1, QW), F32),
                        pltpu.VMEM((2, 1, QW), F32),
                        pltpu.VMEM((2, HEAD_DIM + V_PAD, QW), F32),
                        pltpu.VMEM((2, 1, QW), F32)],
        compiler_params=pltpu.CompilerParams(
            dimension_semantics=("arbitrary", "arbitrary"), vmem_limit_bytes=48 * 1024 * 1024),
        name="fox_attention",
    )(k, kb, qt, vt)


def _mlp_kernel(apply_final, h_ref, ys_ref, oa_ref, gw_ref, gb_ref, gs_ref, ga_ref, wo_ref,
                l2_ref, wu_ref, wd_ref, fg_ref, o_ref):
    g = jax.nn.gelu(ys_ref[...])
    gate = 1.0 / (1.0 + jnp.exp(-(_dot(g.astype(BF16), gw_ref[...]) + gb_ref[...])))
    n_ssm = _rms(g * gate, gs_ref[...]).astype(BF16)
    oa = jnp.concatenate([oa_ref[t] for t in range(TM_MLP // TQ)], axis=1).astype(F32)
    inv = lax.rsqrt(jnp.mean(oa * oa, axis=0, keepdims=True) + EPS)
    n_att = (oa * inv * ga_ref[...]).astype(BF16)
    h1 = (h_ref[...] + _dot(n_ssm, wo_ref[0:D_SSM, :])
          + lax.dot_general(n_att, wo_ref[D_SSM:D_MODEL, :], TN_DIMS, preferred_element_type=F32))
    xn = _rms(h1, l2_ref[...]).astype(BF16)
    o_ref[...] = h1
    for c in range(D_FF // TF_MLP):
        cols = slice(c * TF_MLP, (c + 1) * TF_MLP)
        a = jnp.maximum(_dot(xn, wu_ref[:, cols]), 0.0)
        o_ref[...] += _dot((a * a).astype(BF16), wd_ref[cols, :])
    if apply_final:
        o_ref[...] = _rms(o_ref[...], fg_ref[...])


def _mlp(h, y_ssm, o_att, glu_w, glu_b, gn_s, gn_a, w_out, ln2, w_up, w_down, final_g, apply_final):
    nm = TOKENS // TM_MLP
    row = lambda i: (i, 0)
    fixed = lambda *shape: pl.BlockSpec(shape, lambda i: (0, 0), pipeline_mode=pl.Buffered(1))
    return pl.pallas_call(
        functools.partial(_mlp_kernel, apply_final),
        grid=(nm,),
        in_specs=[
            pl.BlockSpec((TM_MLP, D_MODEL), row),
            pl.BlockSpec((TM_MLP, D_SSM), row),
            pl.BlockSpec((TM_MLP // TQ, D_ATTN, TQ), lambda i: (i, 0, 0)),
            fixed(D_SSM, D_SSM),
            fixed(1, D_SSM),
            fixed(1, D_SSM),
            fixed(D_ATTN, 1),
            fixed(D_MODEL, D_MODEL),
            fixed(1, D_MODEL),
            fixed(D_MODEL, D_FF),
            fixed(D_FF, D_MODEL),
            fixed(1, D_MODEL),
        ],
        out_specs=pl.BlockSpec((TM_MLP, D_MODEL), row),
        out_shape=jax.ShapeDtypeStruct((TOKENS, D_MODEL), F32),
        compiler_params=pltpu.CompilerParams(
            dimension_semantics=("arbitrary",), vmem_limit_bytes=56 * 1024 * 1024),
        name="mixer_out_mlp",
    )(h, y_ssm, o_att, glu_w, glu_b, gn_s, gn_a, w_out, ln2, w_up, w_down, final_g)


def kernel(x, ln1_g, w_in, ssm_log_dt, ssm_lambda_re, ssm_lambda_im, ssm_b_re, ssm_b_im, ssm_c_re, ssm_c_im, ssm_d, glu_w, glu_b, fgate_b, gn_ssm_g, gn_attn_g, w_out, ln2_g, w_up, w_down, final_g):
    assert TM_MLP % TQ == 0 and TM_IN % TQ == 0
    h = x.reshape(TOKENS, D_MODEL)
    for l in range(DEPTH):
        w = w_in[l]
        w_u = w[:, :D_SSM].astype(BF16)
        w_qt = w[:, D_SSM:D_SSM + D_ATTN].T.astype(BF16)
        w_k = w[:, D_SSM + D_ATTN:D_SSM + 2 * D_ATTN].astype(BF16)
        w_vt = w[:, D_SSM + 2 * D_ATTN:D_SSM + 3 * D_ATTN].T.astype(BF16)
        gate_cols = jnp.concatenate([w[:, 4 * D_SSM:]] * 3, axis=1)
        w_gate = jnp.pad(gate_cols, ((0, 0), (0, LANES - 3 * N_HEADS))).astype(BF16)
        b_gate = jnp.pad(jnp.concatenate([fgate_b[l]] * 3), (0, LANES - 3 * N_HEADS)).reshape(1, LANES)

        u, k, kb, qt, vt = _inproj(h, ln1_g[l].reshape(1, D_MODEL), w_u, w_k, w_qt, w_vt, w_gate, b_gate)

        mats = _ssm_prep(*_ssm_operands(ssm_log_dt[l], ssm_lambda_re[l], ssm_lambda_im[l], ssm_b_re[l],
                                        ssm_b_im[l], ssm_c_re[l], ssm_c_im[l], ssm_d[l]))
        y_ssm = _ssm(u, *mats)
        o_att = _attention(k, kb, qt, vt)

        h = _mlp(h, y_ssm, o_att, glu_w[l].astype(BF16), glu_b[l].reshape(1, D_SSM),
                 gn_ssm_g[l].reshape(1, D_SSM), gn_attn_g[l].reshape(D_ATTN, 1),
                 w_out[l].astype(BF16), ln2_g[l].reshape(1, D_MODEL), w_up[l].astype(BF16),
                 w_down[l].astype(BF16), final_g.reshape(1, D_MODEL), l == DEPTH - 1)
    return h.reshape(BATCH, SEQ, D_MODEL)
```

```python
import functools
import math

import jax
import jax.numpy as jnp
from jax import lax
from jax.experimental import pallas as pl
from jax.experimental.pallas import tpu as pltpu

F32 = jnp.float32
BF16 = jnp.bfloat16

D_MODEL = 1024
BATCH = 4
SEQ = 4096
DEPTH = 2
TOKENS = BATCH * SEQ
D_SSM = 512
D_ATTN = 512
SSM_P = 16
SSM_G = 32
SSM_N = 64
HEAD_DIM = 64
N_HEADS = 8
D_FF = 4096
EPS = 1e-6
LOG2E = 1.4426950408889634

LANES = 128
SUBLANES = 8

SSM_CHUNK = SUBLANES
SSM_GB = LANES // SSM_P
SSM_NB = D_SSM // LANES
SSM_W = SSM_CHUNK * LANES
SSM_S = 2 * SSM_GB * SSM_N
SSM_TM = SEQ
SSM_R = SSM_TM // SSM_CHUNK
SSM_SEG = SSM_R // SUBLANES

TM_IN = 1024
TQ = 512
QW = 2 * TQ
TM_MLP = 512
TF_MLP = 1024
CS_BLK = LANES
GATE_ROWS = 32
V_PAD = 16
NEG_BIG = -1e30

NT_DIMS = (((1,), (1,)), ((), ()))
TN_DIMS = (((0,), (0,)), ((), ()))


def _rms(x, g):
    return x * lax.rsqrt(jnp.mean(x * x, axis=-1, keepdims=True) + EPS) * g


def _dot(a, b):
    return jnp.dot(a, b, preferred_element_type=F32)


def _split3(x):
    hi = x.astype(BF16)
    r1 = x - hi.astype(F32)
    mid = r1.astype(BF16)
    low = (r1 - mid.astype(F32)).astype(BF16)
    return hi, mid, low


def _inproj_kernel(x_ref, g_ref, wu_ref, wk_ref, wqt_ref, wvt_ref, fb_ref,
                   u_ref, k_ref, kb_ref, qt_ref, vt_ref, carry_ref):
    i = pl.program_id(0)
    xn = _rms(x_ref[...], g_ref[...]).astype(BF16)
    u_ref[...] = _dot(xn, wu_ref[...])
    k_ref[...] = _dot(xn, wk_ref[...]).astype(BF16)
    qscale = LOG2E / math.sqrt(HEAD_DIM)
    qf = lax.dot_general(wqt_ref[...], xn, NT_DIMS, preferred_element_type=F32)
    vt = lax.dot_general(wvt_ref[...], xn, NT_DIMS, preferred_element_type=F32)
    for t in range(TM_IN // TQ):
        qt_ref[t] = (qf[0:D_ATTN, t * TQ:(t + 1) * TQ] * qscale).astype(BF16)
        vt_ref[t] = vt[:, t * TQ:(t + 1) * TQ].astype(BF16)

    @pl.when(i % (SEQ // TM_IN) == 0)
    def _():
        carry_ref[...] = jnp.zeros(carry_ref.shape, F32)

    r = lax.broadcasted_iota(jnp.int32, (CS_BLK, CS_BLK), 0)
    c = lax.broadcasted_iota(jnp.int32, (CS_BLK, CS_BLK), 1)
    triu = jnp.where(r <= c, 1.0, 0.0).astype(BF16)
    row = lax.broadcasted_iota(jnp.int32, (GATE_ROWS, CS_BLK), 0)
    carry = carry_ref[...]
    for b in range(TM_IN // CS_BLK):
        f = qf[D_ATTN:D_ATTN + GATE_ROWS, b * CS_BLK:(b + 1) * CS_BLK] + fb_ref[...]
        lf = jnp.minimum(f, 0.0) - jnp.log1p(jnp.exp(-jnp.abs(f)))
        hi, mid, low = _split3(lf)
        cs = _dot(hi, triu) + _dot(mid, triu) + _dot(low, triu) + carry
        carry = jnp.broadcast_to(cs[:, CS_BLK - 1:CS_BLK], (GATE_ROWS, CS_BLK))
        p0, p1, p2 = _split3(cs * (-LOG2E))
        piece = jnp.where(row < N_HEADS, p0, jnp.where(row < 2 * N_HEADS, p1, p2)).astype(F32)
        piece = jnp.where(row < 3 * N_HEADS, piece, 0.0)
        wide = jnp.concatenate([piece, jnp.zeros((LANES - GATE_ROWS, CS_BLK), F32)], axis=0)
        kb_ref[b * CS_BLK:(b + 1) * CS_BLK, :] = wide.T.astype(BF16)
    carry_ref[...] = carry


def _inproj(h, g, wu, wk, wqt, wvt, fb):
    n = TOKENS // TM_IN
    fixed = lambda i: (0, 0)
    return pl.pallas_call(
        _inproj_kernel,
        grid=(n,),
        in_specs=[
            pl.BlockSpec((TM_IN, D_MODEL), lambda i: (i, 0)),
            pl.BlockSpec((1, D_MODEL), fixed),
            pl.BlockSpec((D_MODEL, D_SSM), fixed),
            pl.BlockSpec((D_MODEL, D_ATTN), fixed),
            pl.BlockSpec((D_ATTN + GATE_ROWS, D_MODEL), fixed),
            pl.BlockSpec((D_ATTN, D_MODEL), fixed),
            pl.BlockSpec((GATE_ROWS, CS_BLK), fixed),
        ],
        out_specs=[
            pl.BlockSpec((TM_IN, D_SSM), lambda i: (i, 0)),
            pl.BlockSpec((TM_IN, D_ATTN), lambda i: (i, 0)),
            pl.BlockSpec((TM_IN, LANES), lambda i: (i, 0)),
            pl.BlockSpec((TM_IN // TQ, D_ATTN, TQ), lambda i: (i, 0, 0)),
            pl.BlockSpec((TM_IN // TQ, D_ATTN, TQ), lambda i: (i, 0, 0)),
        ],
        out_shape=[
            jax.ShapeDtypeStruct((TOKENS, D_SSM), F32),
            jax.ShapeDtypeStruct((TOKENS, D_ATTN), BF16),
            jax.ShapeDtypeStruct((TOKENS, LANES), BF16),
            jax.ShapeDtypeStruct((TOKENS // TQ, D_ATTN, TQ), BF16),
            jax.ShapeDtypeStruct((TOKENS // TQ, D_ATTN, TQ), BF16),
        ],
        scratch_shapes=[pltpu.VMEM((GATE_ROWS, CS_BLK), F32)],
        compiler_params=pltpu.CompilerParams(
            dimension_semantics=("arbitrary",), vmem_limit_bytes=48 * 1024 * 1024),
        name="inproj",
    )(h, g, wu, wk, wqt, wvt, fb)


def _ssm_kernel(u_ref, m_ref, ws_ref, wct_ref, a_ref, y_ref, s_ref, x_ref, xs_ref, yl_ref):
    nt = SSM_S // 2 // LANES
    tiles = range(nt)
    z = jnp.concatenate(
        [u_ref[pl.ds(s, SSM_R, stride=SSM_CHUNK), :].astype(BF16) for s in range(SSM_CHUNK)], axis=1)
    s = _dot(z, ws_ref[0])
    seg_rows = lambda g: pl.ds(g, SSM_SEG, stride=SUBLANES)
    for c in range(2 * nt):
        for g in range(SUBLANES):
            s_ref[c, seg_rows(g), :] = s[g * SSM_SEG:(g + 1) * SSM_SEG, c * LANES:(c + 1) * LANES]
    mxu_n = 2 * LANES
    for c in range(SSM_W // mxu_n):
        hi = (c + 1) * mxu_n
        yl_ref[:, c * mxu_n:hi] = _dot(z[:, 0:hi], m_ref[0, 0:hi, c * mxu_n:hi])

    lane_tiles = lambda row: [a_ref[0, row:row + 1, c * LANES:(c + 1) * LANES] for c in tiles]
    full = lambda rows: [jnp.broadcast_to(r, (SUBLANES, LANES)) for r in rows]
    ar, ai = full(lane_tiles(0)), full(lane_tiles(1))
    gr, gi = lane_tiles(2), lane_tiles(3)

    def local(k, carry):
        cr, ci = carry
        rows = pl.ds(pl.multiple_of(k * SUBLANES, SUBLANES), SUBLANES)
        nr, ni = [], []
        for c in tiles:
            x_ref[c, rows, :] = cr[c]
            x_ref[nt + c, rows, :] = ci[c]
            nr.append(ar[c] * cr[c] - ai[c] * ci[c] + s_ref[c, rows, :])
            ni.append(ar[c] * ci[c] + ai[c] * cr[c] + s_ref[nt + c, rows, :])
        return tuple(nr), tuple(ni)

    zero = tuple(jnp.zeros((SUBLANES, LANES), F32) for _ in tiles)
    er, ei = lax.fori_loop(0, SSM_SEG, local, (zero, zero))

    for c in tiles:
        cr = jnp.zeros((1, LANES), F32)
        ci = jnp.zeros((1, LANES), F32)
        for g in range(SUBLANES):
            xs_ref[c, g:g + 1, :] = cr
            xs_ref[nt + c, g:g + 1, :] = ci
            cr, ci = (gr[c] * cr - gi[c] * ci + er[c][g:g + 1], gr[c] * ci + gi[c] * cr + ei[c][g:g + 1])
    xr = [xs_ref[c] for c in tiles]
    xi = [xs_ref[nt + c] for c in tiles]

    def fix(k, carry):
        pr, pi = carry
        rows = pl.ds(pl.multiple_of(k * SUBLANES, SUBLANES), SUBLANES)
        nr, ni = [], []
        for c in tiles:
            x_ref[c, rows, :] += pr[c] * xr[c] - pi[c] * xi[c]
            x_ref[nt + c, rows, :] += pr[c] * xi[c] + pi[c] * xr[c]
            nr.append(pr[c] * ar[c] - pi[c] * ai[c])
            ni.append(pr[c] * ai[c] + pi[c] * ar[c])
        return tuple(nr), tuple(ni)

    one = tuple(jnp.ones((SUBLANES, LANES), F32) for _ in tiles)
    lax.fori_loop(0, SSM_SEG, fix, (one, zero))

    x = jnp.concatenate(
        [jnp.concatenate([x_ref[c, seg_rows(g), :] for g in range(SUBLANES)], axis=0).astype(BF16)
         for c in range(2 * nt)], axis=1)
    y = yl_ref[...] + lax.dot_general(x, wct_ref[0], NT_DIMS, preferred_element_type=F32)
    for t in range(SSM_CHUNK):
        y_ref[pl.ds(t, SSM_R, stride=SSM_CHUNK), :] = y[:, t * LANES:(t + 1) * LANES]


def _ssm(u, m, ws, wct, apow, layer):
    nt = TOKENS // SSM_TM
    blk = lambda j, i: (layer * SSM_NB + j, 0, 0)
    return pl.pallas_call(
        _ssm_kernel,
        grid=(SSM_NB, nt),
        in_specs=[
            pl.BlockSpec((SSM_TM, LANES), lambda j, i: (i, j)),
            pl.BlockSpec((1, SSM_W, SSM_W), blk),
            pl.BlockSpec((1, SSM_W, SSM_S), blk),
            pl.BlockSpec((1, SSM_W, SSM_S), blk),
            pl.BlockSpec((1, 4, SSM_S // 2), blk),
        ],
        out_specs=pl.BlockSpec((SSM_TM, LANES), lambda j, i: (i, j)),
        out_shape=jax.ShapeDtypeStruct((TOKENS, D_SSM), F32),
        scratch_shapes=[pltpu.VMEM((SSM_S // LANES, SSM_R, LANES), F32),
                        pltpu.VMEM((SSM_S // LANES, SSM_R, LANES), F32),
                        pltpu.VMEM((SSM_S // LANES, SUBLANES, LANES), F32),
                        pltpu.VMEM((SSM_R, SSM_W), F32)],
        compiler_params=pltpu.CompilerParams(
            dimension_semantics=("arbitrary", "arbitrary"), vmem_limit_bytes=48 * 1024 * 1024),
        name="ssm_scan",
    )(u, m, ws, wct, apow)


def _split2(x):
    hi = x.astype(BF16)
    return hi, (x - hi.astype(F32)).astype(BF16)


def _dot_nt_f32(a, b):
    ah, al = _split2(a)
    bh, bl = _split2(b)
    d = lambda x, y: lax.dot_general(x, y, NT_DIMS, preferred_element_type=F32)
    return d(ah, bh) + d(ah, bl) + d(al, bh)


def _ssm_prep_kernel(lam_ref, b_ref, c_ref, d_ref, m_ref, ws_ref, wct_ref, a_ref):
    half = SSM_S // 2
    tc = SSM_CHUNK
    ldt, lre, lim = lam_ref[0, 0:1, :], lam_ref[0, 1:2, :], lam_ref[0, 2:3, :]
    dt = jnp.exp(ldt)
    tau = lax.broadcasted_iota(jnp.int32, (2 * SUBLANES, half), 0).astype(F32)
    pmag = jnp.exp(tau * (lre * dt))
    p_re = pmag * jnp.cos(tau * (lim * dt))
    p_im = pmag * jnp.sin(tau * (lim * dt))
    a_re, a_im = p_re[1:2], p_im[1:2]
    den = lre * lre + lim * lim
    nr = a_re - 1.0
    s_re = (nr * lre + a_im * lim) / den
    s_im = (a_im * lre - nr * lim) / den
    b_re, b_im = b_ref[0, 0], b_ref[0, 1]
    bb_re = s_re * b_re - s_im * b_im
    bb_im = s_re * b_im + s_im * b_re
    c_re, c_im = c_ref[0, 0], c_ref[0, 1]
    eye = (lax.broadcasted_iota(jnp.int32, (LANES, LANES), 0)
           == lax.broadcasted_iota(jnp.int32, (LANES, LANES), 1))

    m_ref[...] = jnp.zeros(m_ref.shape, BF16)
    for lag in range(tc):
        pr, pi = p_re[lag:lag + 1], p_im[lag:lag + 1]
        l_re = pr * bb_re - pi * bb_im
        l_im = pr * bb_im + pi * bb_re
        rows = slice((tc - 1 - lag) * LANES, (tc - lag) * LANES)
        ws_ref[0, rows, 0:half] = l_re.astype(BF16)
        ws_ref[0, rows, half:SSM_S] = l_im.astype(BF16)
        x = _dot_nt_f32(l_re, c_re) - _dot_nt_f32(l_im, c_im)
        if lag == 0:
            x = x + jnp.where(eye, d_ref[0], 0.0)
        x = x.astype(BF16)
        for s in range(tc - lag):
            m_ref[0, s * LANES:(s + 1) * LANES, (s + lag) * LANES:(s + lag + 1) * LANES] = x
    for t in range(tc):
        pr, pi = p_re[t + 1:t + 2], p_im[t + 1:t + 2]
        rows = slice(t * LANES, (t + 1) * LANES)
        wct_ref[0, rows, 0:half] = (c_re * pr - c_im * pi).astype(BF16)
        wct_ref[0, rows, half:SSM_S] = (-(c_re * pi + c_im * pr)).astype(BF16)

    qr, qi = p_re[tc:tc + 1], p_im[tc:tc + 1]
    a_ref[0, 0:1, :] = qr
    a_ref[0, 1:2, :] = qi
    for _ in range(SSM_SEG.bit_length() - 1):
        qr, qi = qr * qr - qi * qi, 2.0 * qr * qi
    a_ref[0, 2:3, :] = qr
    a_ref[0, 3:4, :] = qi


def _ssm_prep(lam, bexp, cexp, dexp):
    n = lam.shape[0]
    blk = lambda *shape: pl.BlockSpec((1,) + shape, lambda j: (j,) + (0,) * len(shape))
    big = jax.ShapeDtypeStruct((n, SSM_W, SSM_S), BF16)
    return pl.pallas_call(
        _ssm_prep_kernel,
        grid=(n,),
        in_specs=[blk(3, SSM_S // 2), blk(2, LANES, SSM_S // 2), blk(2, LANES, SSM_S // 2), blk(1, LANES)],
        out_specs=[blk(SSM_W, SSM_W), blk(SSM_W, SSM_S), blk(SSM_W, SSM_S), blk(4, SSM_S // 2)],
        out_shape=[big, big, big, jax.ShapeDtypeStruct((n, 4, SSM_S // 2), F32)],
        compiler_params=pltpu.CompilerParams(
            dimension_semantics=("arbitrary",), vmem_limit_bytes=48 * 1024 * 1024),
        name="ssm_prep",
    )(lam, bexp, cexp, dexp)


def _ssm_operands(log_dt, lam_re, lam_im, b_re, b_im, c_re, c_im, d_skip):
    nb, gb = SSM_NB, SSM_GB
    n = DEPTH * nb
    eye_g = jnp.eye(gb, dtype=F32)[None, :, None, :, None]
    ldt = jnp.broadcast_to(log_dt[..., None], (DEPTH, SSM_G, SSM_N))
    lam = jnp.stack([ldt, lam_re, lam_im], axis=1).reshape(DEPTH, 3, nb, gb * SSM_N)
    lam = lam.transpose(0, 2, 1, 3).reshape(n, 3, gb * SSM_N)

    def expand(w):
        return (w[:, :, :, None, :] * eye_g).reshape(n, LANES, gb * SSM_N)

    bt = lambda b: b.reshape(n, gb, SSM_N, SSM_P).transpose(0, 1, 3, 2)
    ct = lambda c: c.reshape(n, gb, SSM_P, SSM_N)
    bexp = jnp.stack([expand(bt(b_re)), expand(bt(b_im))], axis=1)
    cexp = jnp.stack([expand(ct(c_re)), expand(ct(c_im))], axis=1)
    return lam, bexp, cexp, d_skip.reshape(n, 1, LANES)


def _attn_kernel(k_ref, kb_ref, qt_ref, vt_ref, o_ref, rhs_ref, sa_ref, sb_ref, xa_ref, xb_ref,
                 acc_ref, m_ref):
    p = pl.program_id(1)
    row = lax.broadcasted_iota(jnp.int32, (LANES, TQ), 0)
    wide_row = lax.broadcasted_iota(jnp.int32, (LANES, QW), 0)
    in_head = []
    for hh in range(2):
        h = 2 * p + hh
        in_head.append((row >= hh * HEAD_DIM) & (row < (hh + 1) * HEAD_DIM))
        pick = (wide_row == h) | (wide_row == h + N_HEADS) | (wide_row == h + 2 * N_HEADS)
        rhs_ref[hh, LANES:2 * LANES, :] = jnp.where(pick, 1.0, 0.0).astype(BF16)

    key = lax.broadcasted_iota(jnp.int32, (TQ, QW), 0)
    qry = lax.broadcasted_iota(jnp.int32, (TQ, QW), 1)
    causal = key <= qry
    ones_row = jnp.where(lax.broadcasted_iota(jnp.int32, (V_PAD, TQ), 0) == 0, 1.0, 0.0).astype(BF16)
    both = (0, 1)
    full = slice(0, QW)
    upper = slice(TQ, QW)

    def scores(j, s_ref, x_ref, heads, lanes=full):
        rows = pl.ds(pl.multiple_of(j * TQ, TQ), TQ)
        lhs = jnp.concatenate([k_ref[rows, :], kb_ref[rows, :]], axis=1)
        for hh in heads:
            st = _dot(lhs, rhs_ref[hh, :, lanes])
            s_ref[hh, :, lanes] = st
            x_ref[hh, :, lanes] = jnp.max(st, axis=0, keepdims=True)

    def update(j, s_ref, x_ref, mask, heads, lanes=full):
        for hh in heads:
            st = s_ref[hh, :, lanes]
            if mask is None:
                blk_max = x_ref[hh, :, lanes]
            else:
                st = jnp.where(mask, st, NEG_BIG)
                blk_max = jnp.max(st, axis=0, keepdims=True)
            m_prev = m_ref[hh, :, lanes]
            m_next = jnp.maximum(m_prev, blk_max)
            alpha = jnp.exp2(m_prev - m_next)
            pt = jnp.exp2(st - m_next).astype(BF16)
            vt = jnp.concatenate([vt_ref[j, hh * HEAD_DIM:(hh + 1) * HEAD_DIM, :], ones_row], axis=0)
            acc_ref[hh, :, lanes] = alpha * acc_ref[hh, :, lanes] + _dot(vt, pt)
            m_ref[hh, :, lanes] = m_next

    def body(jj, carry):
        j = 2 * jj
        for hh in both:
            scores(j + 1, sb_ref, xb_ref, (hh,))
            update(j, sa_ref, xa_ref, None, (hh,))
        for hh in both:
            scores(j + 2, sa_ref, xa_ref, (hh,))
            update(j + 1, sb_ref, xb_ref, None, (hh,))
        return carry

    def tile(i, carry):
        for t in range(QW // TQ):
            qt = qt_ref[(QW // TQ) * i + t]
            for hh in both:
                rhs_ref[hh, 0:LANES, t * TQ:(t + 1) * TQ] = jnp.where(in_head[hh], qt, jnp.zeros_like(qt))
        m_ref[...] = jnp.full(m_ref.shape, NEG_BIG, F32)
        acc_ref[...] = jnp.zeros(acc_ref.shape, F32)

        scores(0, sa_ref, xa_ref, both)
        lax.fori_loop(0, i, body, 0)
        for hh in both:
            scores(2 * i + 1, sb_ref, xb_ref, (hh,), upper)
            update(2 * i, sa_ref, xa_ref, causal, (hh,))
        update(2 * i + 1, sb_ref, xb_ref, causal[:, 0:TQ], both, upper)

        for hh in both:
            acc = acc_ref[hh]
            out = (acc[0:HEAD_DIM] / acc[HEAD_DIM:HEAD_DIM + 1]).astype(BF16)
            for t in range(QW // TQ):
                o_ref[(QW // TQ) * i + t, hh * HEAD_DIM:(hh + 1) * HEAD_DIM, :] = out[:, t * TQ:(t + 1) * TQ]
        return carry

    lax.fori_loop(0, SEQ // QW, tile, 0)


def _attention(k, kb, qt, vt):
    nq = SEQ // TQ
    return pl.pallas_call(
        _attn_kernel,
        grid=(BATCH, N_HEADS // 2),
        in_specs=[
            pl.BlockSpec((SEQ, LANES), lambda b, p: (b, p)),
            pl.BlockSpec((SEQ, LANES), lambda b, p: (b, 0)),
            pl.BlockSpec((nq, LANES, TQ), lambda b, p: (b, p, 0)),
            pl.BlockSpec((nq, LANES, TQ), lambda b, p: (b, p, 0)),
        ],
        out_specs=pl.BlockSpec((nq, LANES, TQ), lambda b, p: (b, p, 0)),
        out_shape=jax.ShapeDtypeStruct((TOKENS // TQ, D_ATTN, TQ), BF16),
        scratch_shapes=[pltpu.VMEM((2, 2 * LANES, QW), BF16),
                        pltpu.VMEM((2, TQ, QW), F32),
                        pltpu.VMEM((2, TQ, QW), F32),
                        pltpu.VMEM((2, 1, QW), F32),
                        pltpu.VMEM((2, 1, QW), F32),
                        pltpu.VMEM((2, HEAD_DIM + V_PAD, QW), F32),
                        pltpu.VMEM((2, 1, QW), F32)],
        compiler_params=pltpu.CompilerParams(
            dimension_semantics=("arbitrary", "arbitrary"), vmem_limit_bytes=48 * 1024 * 1024),
        name="fox_attention",
    )(k, kb, qt, vt)


def _mlp_kernel(apply_final, h_ref, ys_ref, oa_ref, gw_ref, gb_ref, gs_ref, ga_ref, wo_ref,
                l2_ref, wu_ref, wd_ref, fg_ref, o_ref):
    g = jax.nn.gelu(ys_ref[...])
    gate = 1.0 / (1.0 + jnp.exp(-(_dot(g.astype(BF16), gw_ref[...]) + gb_ref[...])))
    n_ssm = _rms(g * gate, gs_ref[...]).astype(BF16)
    oa = jnp.concatenate([oa_ref[t] for t in range(TM_MLP // TQ)], axis=1).astype(F32)
    inv = lax.rsqrt(jnp.mean(oa * oa, axis=0, keepdims=True) + EPS)
    n_att = (oa * inv * ga_ref[...]).astype(BF16)
    h1 = (h_ref[...] + _dot(n_ssm, wo_ref[0:D_SSM, :])
          + lax.dot_general(n_att, wo_ref[D_SSM:D_MODEL, :], TN_DIMS, preferred_element_type=F32))
    xn = _rms(h1, l2_ref[...]).astype(BF16)
    o_ref[...] = h1
    for c in range(D_FF // TF_MLP):
        cols = slice(c * TF_MLP, (c + 1) * TF_MLP)
        a = jnp.maximum(_dot(xn, wu_ref[:, cols]), 0.0)
        o_ref[...] += _dot((a * a).astype(BF16), wd_ref[cols, :])
    if apply_final:
        o_ref[...] = _rms(o_ref[...], fg_ref[...])


def _mlp(h, y_ssm, o_att, glu_w, glu_b, gn_s, gn_a, w_out, ln2, w_up, w_down, final_g, apply_final):
    nm = TOKENS // TM_MLP
    row = lambda i: (i, 0)
    fixed = lambda *shape: pl.BlockSpec(shape, lambda i: (0, 0), pipeline_mode=pl.Buffered(1))
    return pl.pallas_call(
        functools.partial(_mlp_kernel, apply_final),
        grid=(nm,),
        in_specs=[
            pl.BlockSpec((TM_MLP, D_MODEL), row),
            pl.BlockSpec((TM_MLP, D_SSM), row),
            pl.BlockSpec((TM_MLP // TQ, D_ATTN, TQ), lambda i: (i, 0, 0)),
            fixed(D_SSM, D_SSM),
            fixed(1, D_SSM),
            fixed(1, D_SSM),
            fixed(D_ATTN, 1),
            fixed(D_MODEL, D_MODEL),
            fixed(1, D_MODEL),
            fixed(D_MODEL, D_FF),
            fixed(D_FF, D_MODEL),
            fixed(1, D_MODEL),
        ],
        out_specs=pl.BlockSpec((TM_MLP, D_MODEL), row),
        out_shape=jax.ShapeDtypeStruct((TOKENS, D_MODEL), F32),
        compiler_params=pltpu.CompilerParams(
            dimension_semantics=("arbitrary",), vmem_limit_bytes=56 * 1024 * 1024),
        name="mixer_out_mlp",
    )(h, y_ssm, o_att, glu_w, glu_b, gn_s, gn_a, w_out, ln2, w_up, w_down, final_g)


def kernel(x, ln1_g, w_in, ssm_log_dt, ssm_lambda_re, ssm_lambda_im, ssm_b_re, ssm_b_im, ssm_c_re, ssm_c_im, ssm_d, glu_w, glu_b, fgate_b, gn_ssm_g, gn_attn_g, w_out, ln2_g, w_up, w_down, final_g):
    assert TM_MLP % TQ == 0 and TM_IN % TQ == 0
    h = x.reshape(TOKENS, D_MODEL)
    mats = _ssm_prep(*_ssm_operands(ssm_log_dt, ssm_lambda_re, ssm_lambda_im, ssm_b_re, ssm_b_im,
                                    ssm_c_re, ssm_c_im, ssm_d))
    for l in range(DEPTH):
        w = w_in[l]
        w_u = w[:, :D_SSM].astype(BF16)
        w_gt = w[:, 4 * D_SSM:].T
        w_qt = jnp.concatenate([w[:, D_SSM:D_SSM + D_ATTN].T, w_gt, w_gt, w_gt,
                                jnp.zeros((GATE_ROWS - 3 * N_HEADS, D_MODEL), F32)], axis=0).astype(BF16)
        w_k = w[:, D_SSM + D_ATTN:D_SSM + 2 * D_ATTN].astype(BF16)
        w_vt = w[:, D_SSM + 2 * D_ATTN:D_SSM + 3 * D_ATTN].T.astype(BF16)
        b_gate = jnp.pad(jnp.concatenate([fgate_b[l]] * 3), (0, GATE_ROWS - 3 * N_HEADS))
        b_gate = jnp.broadcast_to(b_gate[:, None], (GATE_ROWS, CS_BLK))

        u, k, kb, qt, vt = _inproj(h, ln1_g[l].reshape(1, D_MODEL), w_u, w_k, w_qt, w_vt, b_gate)

        y_ssm = _ssm(u, *mats, l)
        o_att = _attention(k, kb, qt, vt)

        h = _mlp(h, y_ssm, o_att, glu_w[l].astype(BF16), glu_b[l].reshape(1, D_SSM),
                 gn_ssm_g[l].reshape(1, D_SSM), gn_attn_g[l].reshape(D_ATTN, 1),
                 w_out[l].astype(BF16), ln2_g[l].reshape(1, D_MODEL), w_up[l].astype(BF16),
                 w_down[l].astype(BF16), final_g.reshape(1, D_MODEL), l == DEPTH - 1)
    return h.reshape(BATCH, SEQ, D_MODEL)
```

```python
import functools
import math

import jax
import jax.numpy as jnp
from jax import lax
from jax.experimental import pallas as pl
from jax.experimental.pallas import tpu as pltpu

F32 = jnp.float32
BF16 = jnp.bfloat16

D_MODEL = 1024
BATCH = 4
SEQ = 4096
DEPTH = 2
TOKENS = BATCH * SEQ
D_SSM = 512
D_ATTN = 512
SSM_P = 16
SSM_G = 32
SSM_N = 64
HEAD_DIM = 64
N_HEADS = 8
D_FF = 4096
EPS = 1e-6
LOG2E = 1.4426950408889634

LANES = 128
SUBLANES = 8

SSM_CHUNK = SUBLANES
SSM_GB = LANES // SSM_P
SSM_NB = D_SSM // LANES
SSM_W = SSM_CHUNK * LANES
SSM_S = 2 * SSM_GB * SSM_N
SSM_TM = SEQ
SSM_R = SSM_TM // SSM_CHUNK
SSM_SEG = SSM_R // SUBLANES

TM_IN = 1024
TQ = 512
QW = 2 * TQ
TM_MLP = 512
TF_MLP = 1024
CS_BLK = LANES
GATE_ROWS = 32
V_PAD = 16
NEG_BIG = -1e30

NT_DIMS = (((1,), (1,)), ((), ()))
TN_DIMS = (((0,), (0,)), ((), ()))


def _rms(x, g):
    return x * lax.rsqrt(jnp.mean(x * x, axis=-1, keepdims=True) + EPS) * g


def _dot(a, b):
    return jnp.dot(a, b, preferred_element_type=F32)


def _split3(x):
    hi = x.astype(BF16)
    r1 = x - hi.astype(F32)
    mid = r1.astype(BF16)
    low = (r1 - mid.astype(F32)).astype(BF16)
    return hi, mid, low


def _inproj_kernel(x_ref, g_ref, wu_ref, wk_ref, wqt_ref, wvt_ref, fb_ref,
                   u_ref, k_ref, kb_ref, qt_ref, vt_ref, carry_ref):
    i = pl.program_id(0)

    @pl.when(i % (SEQ // TM_IN) == 0)
    def _():
        carry_ref[...] = jnp.zeros(carry_ref.shape, F32)

    qscale = LOG2E / math.sqrt(HEAD_DIM)
    r = lax.broadcasted_iota(jnp.int32, (CS_BLK, CS_BLK), 0)
    c = lax.broadcasted_iota(jnp.int32, (CS_BLK, CS_BLK), 1)
    triu = jnp.where(r <= c, 1.0, 0.0).astype(BF16)
    row = lax.broadcasted_iota(jnp.int32, (GATE_ROWS, CS_BLK), 0)
    carry = carry_ref[...]
    for t in range(TM_IN // TQ):
        tok = slice(t * TQ, (t + 1) * TQ)
        xn = _rms(x_ref[tok, :], g_ref[...]).astype(BF16)
        u_ref[tok, :] = _dot(xn, wu_ref[...])
        k_ref[tok, :] = _dot(xn, wk_ref[...]).astype(BF16)
        qf = lax.dot_general(wqt_ref[...], xn, NT_DIMS, preferred_element_type=F32)
        qt_ref[t] = (qf[0:D_ATTN] * qscale).astype(BF16)
        vt_ref[t] = lax.dot_general(wvt_ref[...], xn, NT_DIMS, preferred_element_type=F32).astype(BF16)

        for b in range(TQ // CS_BLK):
            f = qf[D_ATTN:D_ATTN + GATE_ROWS, b * CS_BLK:(b + 1) * CS_BLK] + fb_ref[...]
            lf = jnp.minimum(f, 0.0) - jnp.log1p(jnp.exp(-jnp.abs(f)))
            hi, mid, low = _split3(lf)
            local = _dot(hi, triu) + _dot(mid, triu) + _dot(low, triu)
            cs = local + carry
            carry = carry + jnp.broadcast_to(local[:, CS_BLK - 1:CS_BLK], (GATE_ROWS, CS_BLK))
            p0, p1, p2 = _split3(cs * (-LOG2E))
            piece = jnp.where(row < N_HEADS, p0, jnp.where(row < 2 * N_HEADS, p1, p2)).astype(F32)
            piece = jnp.where(row < 3 * N_HEADS, piece, 0.0)
            wide = jnp.concatenate([piece, jnp.zeros((LANES - GATE_ROWS, CS_BLK), F32)], axis=0)
            lo = t * TQ + b * CS_BLK
            kb_ref[lo:lo + CS_BLK, :] = wide.T.astype(BF16)
    carry_ref[...] = carry


def _inproj(h, g, wu, wk, wqt, wvt, fb):
    n = TOKENS // TM_IN
    fixed = lambda i: (0, 0)
    return pl.pallas_call(
        _inproj_kernel,
        grid=(n,),
        in_specs=[
            pl.BlockSpec((TM_IN, D_MODEL), lambda i: (i, 0)),
            pl.BlockSpec((1, D_MODEL), fixed),
            pl.BlockSpec((D_MODEL, D_SSM), fixed),
            pl.BlockSpec((D_MODEL, D_ATTN), fixed),
            pl.BlockSpec((D_ATTN + GATE_ROWS, D_MODEL), fixed),
            pl.BlockSpec((D_ATTN, D_MODEL), fixed),
            pl.BlockSpec((GATE_ROWS, CS_BLK), fixed),
        ],
        out_specs=[
            pl.BlockSpec((TM_IN, D_SSM), lambda i: (i, 0)),
            pl.BlockSpec((TM_IN, D_ATTN), lambda i: (i, 0)),
            pl.BlockSpec((TM_IN, LANES), lambda i: (i, 0)),
            pl.BlockSpec((TM_IN // TQ, D_ATTN, TQ), lambda i: (i, 0, 0)),
            pl.BlockSpec((TM_IN // TQ, D_ATTN, TQ), lambda i: (i, 0, 0)),
        ],
        out_shape=[
            jax.ShapeDtypeStruct((TOKENS, D_SSM), F32),
            jax.ShapeDtypeStruct((TOKENS, D_ATTN), BF16),
            jax.ShapeDtypeStruct((TOKENS, LANES), BF16),
            jax.ShapeDtypeStruct((TOKENS // TQ, D_ATTN, TQ), BF16),
            jax.ShapeDtypeStruct((TOKENS // TQ, D_ATTN, TQ), BF16),
        ],
        scratch_shapes=[pltpu.VMEM((GATE_ROWS, CS_BLK), F32)],
        compiler_params=pltpu.CompilerParams(
            dimension_semantics=("arbitrary",), vmem_limit_bytes=48 * 1024 * 1024),
        name="inproj",
    )(h, g, wu, wk, wqt, wvt, fb)


def _ssm_kernel(u_ref, m_ref, ws_ref, wct_ref, a_ref, y_ref, s_ref, x_ref, xs_ref, yl_ref):
    nt = SSM_S // 2 // LANES
    tiles = range(nt)
    z = jnp.concatenate(
        [u_ref[pl.ds(s, SSM_R, stride=SSM_CHUNK), :].astype(BF16) for s in range(SSM_CHUNK)], axis=1)
    s = _dot(z, ws_ref[0])
    seg_rows = lambda g: pl.ds(g, SSM_SEG, stride=SUBLANES)
    for c in range(2 * nt):
        for g in range(SUBLANES):
            s_ref[c, seg_rows(g), :] = s[g * SSM_SEG:(g + 1) * SSM_SEG, c * LANES:(c + 1) * LANES]
    mxu_n = 2 * LANES
    for c in range(SSM_W // mxu_n):
        hi = (c + 1) * mxu_n
        yl_ref[:, c * mxu_n:hi] = _dot(z[:, 0:hi], m_ref[0, 0:hi, c * mxu_n:hi])

    lane_tiles = lambda row: [a_ref[0, row:row + 1, c * LANES:(c + 1) * LANES] for c in tiles]
    full = lambda rows: [jnp.broadcast_to(r, (SUBLANES, LANES)) for r in rows]
    ar, ai = full(lane_tiles(0)), full(lane_tiles(1))
    gr, gi = lane_tiles(2), lane_tiles(3)

    def local(k, carry):
        cr, ci = carry
        rows = pl.ds(pl.multiple_of(k * SUBLANES, SUBLANES), SUBLANES)
        nr, ni = [], []
        for c in tiles:
            x_ref[c, rows, :] = cr[c]
            x_ref[nt + c, rows, :] = ci[c]
            nr.append(ar[c] * cr[c] - ai[c] * ci[c] + s_ref[c, rows, :])
            ni.append(ar[c] * ci[c] + ai[c] * cr[c] + s_ref[nt + c, rows, :])
        return tuple(nr), tuple(ni)

    zero = tuple(jnp.zeros((SUBLANES, LANES), F32) for _ in tiles)
    er, ei = lax.fori_loop(0, SSM_SEG, local, (zero, zero))

    for c in tiles:
        cr = jnp.zeros((1, LANES), F32)
        ci = jnp.zeros((1, LANES), F32)
        for g in range(SUBLANES):
            xs_ref[c, g:g + 1, :] = cr
            xs_ref[nt + c, g:g + 1, :] = ci
            cr, ci = (gr[c] * cr - gi[c] * ci + er[c][g:g + 1], gr[c] * ci + gi[c] * cr + ei[c][g:g + 1])
    xr = [xs_ref[c] for c in tiles]
    xi = [xs_ref[nt + c] for c in tiles]

    def fix(k, carry):
        pr, pi = carry
        rows = pl.ds(pl.multiple_of(k * SUBLANES, SUBLANES), SUBLANES)
        nr, ni = [], []
        for c in tiles:
            x_ref[c, rows, :] += pr[c] * xr[c] - pi[c] * xi[c]
            x_ref[nt + c, rows, :] += pr[c] * xi[c] + pi[c] * xr[c]
            nr.append(pr[c] * ar[c] - pi[c] * ai[c])
            ni.append(pr[c] * ai[c] + pi[c] * ar[c])
        return tuple(nr), tuple(ni)

    one = tuple(jnp.ones((SUBLANES, LANES), F32) for _ in tiles)
    lax.fori_loop(0, SSM_SEG, fix, (one, zero))

    x = jnp.concatenate(
        [jnp.concatenate([x_ref[c, seg_rows(g), :] for g in range(SUBLANES)], axis=0).astype(BF16)
         for c in range(2 * nt)], axis=1)
    y = yl_ref[...] + lax.dot_general(x, wct_ref[0], NT_DIMS, preferred_element_type=F32)
    for t in range(SSM_CHUNK):
        y_ref[pl.ds(t, SSM_R, stride=SSM_CHUNK), :] = y[:, t * LANES:(t + 1) * LANES]


def _ssm(u, m, ws, wct, apow, layer):
    nt = TOKENS // SSM_TM
    blk = lambda j, i: (layer * SSM_NB + j, 0, 0)
    return pl.pallas_call(
        _ssm_kernel,
        grid=(SSM_NB, nt),
        in_specs=[
            pl.BlockSpec((SSM_TM, LANES), lambda j, i: (i, j)),
            pl.BlockSpec((1, SSM_W, SSM_W), blk),
            pl.BlockSpec((1, SSM_W, SSM_S), blk),
            pl.BlockSpec((1, SSM_W, SSM_S), blk),
            pl.BlockSpec((1, 4, SSM_S // 2), blk),
        ],
        out_specs=pl.BlockSpec((SSM_TM, LANES), lambda j, i: (i, j)),
        out_shape=jax.ShapeDtypeStruct((TOKENS, D_SSM), F32),
        scratch_shapes=[pltpu.VMEM((SSM_S // LANES, SSM_R, LANES), F32),
                        pltpu.VMEM((SSM_S // LANES, SSM_R, LANES), F32),
                        pltpu.VMEM((SSM_S // LANES, SUBLANES, LANES), F32),
                        pltpu.VMEM((SSM_R, SSM_W), F32)],
        compiler_params=pltpu.CompilerParams(
            dimension_semantics=("arbitrary", "arbitrary"), vmem_limit_bytes=48 * 1024 * 1024),
        name="ssm_scan",
    )(u, m, ws, wct, apow)


def _split2(x):
    hi = x.astype(BF16)
    return hi, (x - hi.astype(F32)).astype(BF16)


def _dot_nt_f32(a, b):
    ah, al = _split2(a)
    bh, bl = _split2(b)
    d = lambda x, y: lax.dot_general(x, y, NT_DIMS, preferred_element_type=F32)
    return d(ah, bh) + d(ah, bl) + d(al, bh)


def _ssm_prep_kernel(lam_ref, b_ref, c_ref, d_ref, m_ref, ws_ref, wct_ref, a_ref):
    half = SSM_S // 2
    tc = SSM_CHUNK
    ldt, lre, lim = lam_ref[0, 0:1, :], lam_ref[0, 1:2, :], lam_ref[0, 2:3, :]
    dt = jnp.exp(ldt)
    tau = lax.broadcasted_iota(jnp.int32, (2 * SUBLANES, half), 0).astype(F32)
    pmag = jnp.exp(tau * (lre * dt))
    p_re = pmag * jnp.cos(tau * (lim * dt))
    p_im = pmag * jnp.sin(tau * (lim * dt))
    a_re, a_im = p_re[1:2], p_im[1:2]
    den = lre * lre + lim * lim
    nr = a_re - 1.0
    s_re = (nr * lre + a_im * lim) / den
    s_im = (a_im * lre - nr * lim) / den
    b_re, b_im = b_ref[0, 0], b_ref[0, 1]
    bb_re = s_re * b_re - s_im * b_im
    bb_im = s_re * b_im + s_im * b_re
    c_re, c_im = c_ref[0, 0], c_ref[0, 1]
    eye = (lax.broadcasted_iota(jnp.int32, (LANES, LANES), 0)
           == lax.broadcasted_iota(jnp.int32, (LANES, LANES), 1))

    m_ref[...] = jnp.zeros(m_ref.shape, BF16)
    for lag in range(tc):
        pr, pi = p_re[lag:lag + 1], p_im[lag:lag + 1]
        l_re = pr * bb_re - pi * bb_im
        l_im = pr * bb_im + pi * bb_re
        rows = slice((tc - 1 - lag) * LANES, (tc - lag) * LANES)
        ws_ref[0, rows, 0:half] = l_re.astype(BF16)
        ws_ref[0, rows, half:SSM_S] = l_im.astype(BF16)
        x = _dot_nt_f32(l_re, c_re) - _dot_nt_f32(l_im, c_im)
        if lag == 0:
            x = x + jnp.where(eye, d_ref[0], 0.0)
        x = x.astype(BF16)
        for s in range(tc - lag):
            m_ref[0, s * LANES:(s + 1) * LANES, (s + lag) * LANES:(s + lag + 1) * LANES] = x
    for t in range(tc):
        pr, pi = p_re[t + 1:t + 2], p_im[t + 1:t + 2]
        rows = slice(t * LANES, (t + 1) * LANES)
        wct_ref[0, rows, 0:half] = (c_re * pr - c_im * pi).astype(BF16)
        wct_ref[0, rows, half:SSM_S] = (-(c_re * pi + c_im * pr)).astype(BF16)

    qr, qi = p_re[tc:tc + 1], p_im[tc:tc + 1]
    a_ref[0, 0:1, :] = qr
    a_ref[0, 1:2, :] = qi
    for _ in range(SSM_SEG.bit_length() - 1):
        qr, qi = qr * qr - qi * qi, 2.0 * qr * qi
    a_ref[0, 2:3, :] = qr
    a_ref[0, 3:4, :] = qi


def _ssm_prep(lam, bexp, cexp, dexp):
    n = lam.shape[0]
    blk = lambda *shape: pl.BlockSpec((1,) + shape, lambda j: (j,) + (0,) * len(shape))
    big = jax.ShapeDtypeStruct((n, SSM_W, SSM_S), BF16)
    return pl.pallas_call(
        _ssm_prep_kernel,
        grid=(n,),
        in_specs=[blk(3, SSM_S // 2), blk(2, LANES, SSM_S // 2), blk(2, LANES, SSM_S // 2), blk(1, LANES)],
        out_specs=[blk(SSM_W, SSM_W), blk(SSM_W, SSM_S), blk(SSM_W, SSM_S), blk(4, SSM_S // 2)],
        out_shape=[big, big, big, jax.ShapeDtypeStruct((n, 4, SSM_S // 2), F32)],
        compiler_params=pltpu.CompilerParams(
            dimension_semantics=("arbitrary",), vmem_limit_bytes=48 * 1024 * 1024),
        name="ssm_prep",
    )(lam, bexp, cexp, dexp)


def _ssm_operands(log_dt, lam_re, lam_im, b_re, b_im, c_re, c_im, d_skip):
    nb, gb = SSM_NB, SSM_GB
    n = DEPTH * nb
    eye_g = jnp.eye(gb, dtype=F32)[None, :, None, :, None]
    ldt = jnp.broadcast_to(log_dt[..., None], (DEPTH, SSM_G, SSM_N))
    lam = jnp.stack([ldt, lam_re, lam_im], axis=1).reshape(DEPTH, 3, nb, gb * SSM_N)
    lam = lam.transpose(0, 2, 1, 3).reshape(n, 3, gb * SSM_N)

    def expand(w):
        return (w[:, :, :, None, :] * eye_g).reshape(n, LANES, gb * SSM_N)

    bt = lambda b: b.reshape(n, gb, SSM_N, SSM_P).transpose(0, 1, 3, 2)
    ct = lambda c: c.reshape(n, gb, SSM_P, SSM_N)
    bexp = jnp.stack([expand(bt(b_re)), expand(bt(b_im))], axis=1)
    cexp = jnp.stack([expand(ct(c_re)), expand(ct(c_im))], axis=1)
    return lam, bexp, cexp, d_skip.reshape(n, 1, LANES)


def _attn_kernel(k_ref, kb_ref, qt_ref, vt_ref, o_ref, rhs_ref, sa_ref, sb_ref, xa_ref, xb_ref,
                 acc_ref, m_ref):
    p = pl.program_id(1)
    row = lax.broadcasted_iota(jnp.int32, (LANES, TQ), 0)
    wide_row = lax.broadcasted_iota(jnp.int32, (LANES, QW), 0)
    in_head = []
    for hh in range(2):
        h = 2 * p + hh
        in_head.append((row >= hh * HEAD_DIM) & (row < (hh + 1) * HEAD_DIM))
        pick = (wide_row == h) | (wide_row == h + N_HEADS) | (wide_row == h + 2 * N_HEADS)
        rhs_ref[hh, LANES:2 * LANES, :] = jnp.where(pick, 1.0, 0.0).astype(BF16)

    key = lax.broadcasted_iota(jnp.int32, (TQ, QW), 0)
    qry = lax.broadcasted_iota(jnp.int32, (TQ, QW), 1)
    causal = key <= qry
    ones_row = jnp.where(lax.broadcasted_iota(jnp.int32, (V_PAD, TQ), 0) == 0, 1.0, 0.0).astype(BF16)
    both = (0, 1)
    full = slice(0, QW)
    upper = slice(TQ, QW)

    def scores(j, s_ref, x_ref, heads, lanes=full):
        rows = pl.ds(pl.multiple_of(j * TQ, TQ), TQ)
        lhs = jnp.concatenate([k_ref[rows, :], kb_ref[rows, :]], axis=1)
        for hh in heads:
            st = _dot(lhs, rhs_ref[hh, :, lanes])
            s_ref[hh, :, lanes] = st
            x_ref[hh, :, lanes] = jnp.max(st, axis=0, keepdims=True)

    def update(j, s_ref, x_ref, mask, heads, lanes=full):
        for hh in heads:
            st = s_ref[hh, :, lanes]
            if mask is None:
                blk_max = x_ref[hh, :, lanes]
            else:
                st = jnp.where(mask, st, NEG_BIG)
                blk_max = jnp.max(st, axis=0, keepdims=True)
            m_prev = m_ref[hh, :, lanes]
            m_next = jnp.maximum(m_prev, blk_max)
            alpha = jnp.exp2(m_prev - m_next)
            pt = jnp.exp2(st - m_next).astype(BF16)
            vt = jnp.concatenate([vt_ref[j, hh * HEAD_DIM:(hh + 1) * HEAD_DIM, :], ones_row], axis=0)
            acc_ref[hh, :, lanes] = alpha * acc_ref[hh, :, lanes] + _dot(vt, pt)
            m_ref[hh, :, lanes] = m_next

    def body(jj, carry):
        j = 2 * jj
        for hh in both:
            scores(j + 1, sb_ref, xb_ref, (hh,))
            update(j, sa_ref, xa_ref, None, (hh,))
        for hh in both:
            scores(j + 2, sa_ref, xa_ref, (hh,))
            update(j + 1, sb_ref, xb_ref, None, (hh,))
        return carry

    def tile(i, carry):
        for t in range(QW // TQ):
            qt = qt_ref[(QW // TQ) * i + t]
            for hh in both:
                rhs_ref[hh, 0:LANES, t * TQ:(t + 1) * TQ] = jnp.where(in_head[hh], qt, jnp.zeros_like(qt))
        m_ref[...] = jnp.full(m_ref.shape, NEG_BIG, F32)
        acc_ref[...] = jnp.zeros(acc_ref.shape, F32)

        scores(0, sa_ref, xa_ref, both)
        lax.fori_loop(0, i, body, 0)
        for hh in both:
            scores(2 * i + 1, sb_ref, xb_ref, (hh,), upper)
            update(2 * i, sa_ref, xa_ref, causal, (hh,))
        update(2 * i + 1, sb_ref, xb_ref, causal[:, 0:TQ], both, upper)

        for hh in both:
            acc = acc_ref[hh]
            out = (acc[0:HEAD_DIM] / acc[HEAD_DIM:HEAD_DIM + 1]).astype(BF16)
            for t in range(QW // TQ):
                o_ref[(QW // TQ) * i + t, hh * HEAD_DIM:(hh + 1) * HEAD_DIM, :] = out[:, t * TQ:(t + 1) * TQ]
        return carry

    lax.fori_loop(0, SEQ // QW, tile, 0)


def _attention(k, kb, qt, vt):
    nq = SEQ // TQ
    return pl.pallas_call(
        _attn_kernel,
        grid=(BATCH, N_HEADS // 2),
        in_specs=[
            pl.BlockSpec((SEQ, LANES), lambda b, p: (b, p)),
            pl.BlockSpec((SEQ, LANES), lambda b, p: (b, 0)),
            pl.BlockSpec((nq, LANES, TQ), lambda b, p: (b, p, 0)),
            pl.BlockSpec((nq, LANES, TQ), lambda b, p: (b, p, 0)),
        ],
        out_specs=pl.BlockSpec((nq, LANES, TQ), lambda b, p: (b, p, 0)),
        out_shape=jax.ShapeDtypeStruct((TOKENS // TQ, D_ATTN, TQ), BF16),
        scratch_shapes=[pltpu.VMEM((2, 2 * LANES, QW), BF16),
                        pltpu.VMEM((2, TQ, QW), F32),
                        pltpu.VMEM((2, TQ, QW), F32),
                        pltpu.VMEM((2, 1, QW), F32),
                        pltpu.VMEM((2, 1, QW), F32),
                        pltpu.VMEM((2, HEAD_DIM + V_PAD, QW), F32),
                        pltpu.VMEM((2, 1, QW), F32)],
        compiler_params=pltpu.CompilerParams(
            dimension_semantics=("arbitrary", "arbitrary"), vmem_limit_bytes=48 * 1024 * 1024),
        name="fox_attention",
    )(k, kb, qt, vt)


def _mlp_kernel(apply_final, h_ref, ys_ref, oa_ref, gw_ref, gb_ref, gs_ref, ga_ref, wo_ref,
                l2_ref, wu_ref, wd_ref, fg_ref, o_ref):
    g = jax.nn.gelu(ys_ref[...])
    gate = 1.0 / (1.0 + jnp.exp(-(_dot(g.astype(BF16), gw_ref[...]) + gb_ref[...])))
    n_ssm = _rms(g * gate, gs_ref[...]).astype(BF16)
    oa = jnp.concatenate([oa_ref[t] for t in range(TM_MLP // TQ)], axis=1).astype(F32)
    inv = lax.rsqrt(jnp.mean(oa * oa, axis=0, keepdims=True) + EPS)
    n_att = (oa * inv * ga_ref[...]).astype(BF16)
    h1 = (h_ref[...] + _dot(n_ssm, wo_ref[0:D_SSM, :])
          + lax.dot_general(n_att, wo_ref[D_SSM:D_MODEL, :], TN_DIMS, preferred_element_type=F32))
    xn = _rms(h1, l2_ref[...]).astype(BF16)
    o_ref[...] = h1
    for c in range(D_FF // TF_MLP):
        cols = slice(c * TF_MLP, (c + 1) * TF_MLP)
        a = jnp.maximum(_dot(xn, wu_ref[:, cols]), 0.0)
        o_ref[...] += _dot((a * a).astype(BF16), wd_ref[cols, :])
    if apply_final:
        o_ref[...] = _rms(o_ref[...], fg_ref[...])


def _mlp(h, y_ssm, o_att, glu_w, glu_b, gn_s, gn_a, w_out, ln2, w_up, w_down, final_g, apply_final):
    nm = TOKENS // TM_MLP
    row = lambda i: (i, 0)
    fixed = lambda *shape: pl.BlockSpec(shape, lambda i: (0, 0), pipeline_mode=pl.Buffered(1))
    return pl.pallas_call(
        functools.partial(_mlp_kernel, apply_final),
        grid=(nm,),
        in_specs=[
            pl.BlockSpec((TM_MLP, D_MODEL), row),
            pl.BlockSpec((TM_MLP, D_SSM), row),
            pl.BlockSpec((TM_MLP // TQ, D_ATTN, TQ), lambda i: (i, 0, 0)),
            fixed(D_SSM, D_SSM),
            fixed(1, D_SSM),
            fixed(1, D_SSM),
            fixed(D_ATTN, 1),
            fixed(D_MODEL, D_MODEL),
            fixed(1, D_MODEL),
            fixed(D_MODEL, D_FF),
            fixed(D_FF, D_MODEL),
            fixed(1, D_MODEL),
        ],
        out_specs=pl.BlockSpec((TM_MLP, D_MODEL), row),
        out_shape=jax.ShapeDtypeStruct((TOKENS, D_MODEL), F32),
        compiler_params=pltpu.CompilerParams(
            dimension_semantics=("arbitrary",), vmem_limit_bytes=56 * 1024 * 1024),
        name="mixer_out_mlp",
    )(h, y_ssm, o_att, glu_w, glu_b, gn_s, gn_a, w_out, ln2, w_up, w_down, final_g)


def kernel(x, ln1_g, w_in, ssm_log_dt, ssm_lambda_re, ssm_lambda_im, ssm_b_re, ssm_b_im, ssm_c_re, ssm_c_im, ssm_d, glu_w, glu_b, fgate_b, gn_ssm_g, gn_attn_g, w_out, ln2_g, w_up, w_down, final_g):
    assert TM_MLP % TQ == 0 and TM_IN % TQ == 0
    h = x.reshape(TOKENS, D_MODEL)
    mats = _ssm_prep(*_ssm_operands(ssm_log_dt, ssm_lambda_re, ssm_lambda_im, ssm_b_re, ssm_b_im,
                                    ssm_c_re, ssm_c_im, ssm_d))
    for l in range(DEPTH):
        w = w_in[l]
        w_u = w[:, :D_SSM].astype(BF16)
        w_gt = w[:, 4 * D_SSM:].T
        w_qt = jnp.concatenate([w[:, D_SSM:D_SSM + D_ATTN].T, w_gt, w_gt, w_gt,
                                jnp.zeros((GATE_ROWS - 3 * N_HEADS, D_MODEL), F32)], axis=0).astype(BF16)
        w_k = w[:, D_SSM + D_ATTN:D_SSM + 2 * D_ATTN].astype(BF16)
        w_vt = w[:, D_SSM + 2 * D_ATTN:D_SSM + 3 * D_ATTN].T.astype(BF16)
        b_gate = jnp.pad(jnp.concatenate([fgate_b[l]] * 3), (0, GATE_ROWS - 3 * N_HEADS))
        b_gate = jnp.broadcast_to(b_gate[:, None], (GATE_ROWS, CS_BLK))

        u, k, kb, qt, vt = _inproj(h, ln1_g[l].reshape(1, D_MODEL), w_u, w_k, w_qt, w_vt, b_gate)

        y_ssm = _ssm(u, *mats, l)
        o_att = _attention(k, kb, qt, vt)

        h = _mlp(h, y_ssm, o_att, glu_w[l].astype(BF16), glu_b[l].reshape(1, D_SSM),
                 gn_ssm_g[l].reshape(1, D_SSM), gn_attn_g[l].reshape(D_ATTN, 1),
                 w_out[l].astype(BF16), ln2_g[l].reshape(1, D_MODEL), w_up[l].astype(BF16),
                 w_down[l].astype(BF16), final_g.reshape(1, D_MODEL), l == DEPTH - 1)
    return h.reshape(BATCH, SEQ, D_MODEL)
```

```python
import functools
import math

import jax
import jax.numpy as jnp
from jax import lax
from jax.experimental import pallas as pl
from jax.experimental.pallas import tpu as pltpu

F32 = jnp.float32
BF16 = jnp.bfloat16

D_MODEL = 1024
BATCH = 4
SEQ = 4096
DEPTH = 2
TOKENS = BATCH * SEQ
D_SSM = 512
D_ATTN = 512
SSM_P = 16
SSM_G = 32
SSM_N = 64
HEAD_DIM = 64
N_HEADS = 8
D_FF = 4096
EPS = 1e-6
LOG2E = 1.4426950408889634

LANES = 128
SUBLANES = 8
MXU_N = 256
VMEM_LIMIT = 48 * 2 ** 20

SSM_CHUNK = SUBLANES
SSM_GB = LANES // SSM_P
SSM_NB = D_SSM // LANES
SSM_W = SSM_CHUNK * LANES
SSM_S = 2 * SSM_GB * SSM_N
SSM_TM = SEQ
SSM_R = SSM_TM // SSM_CHUNK
SSM_SEG = SSM_R // SUBLANES

TM_IN = 2048
TQ = 512
QW = 2 * TQ
TM_MLP = 512
TF_MLP = 1024
CS_BLK = LANES
GATE_ROWS = 32
V_PAD = 16
NEG_BIG = -1e30

NT_DIMS = (((1,), (1,)), ((), ()))
TN_DIMS = (((0,), (0,)), ((), ()))


def _rms(x, g):
    return x * lax.rsqrt(jnp.mean(x * x, axis=-1, keepdims=True) + EPS) * g


def _dot(a, b):
    return jnp.dot(a, b, preferred_element_type=F32)


def _split3(x):
    hi = x.astype(BF16)
    r1 = x - hi.astype(F32)
    mid = r1.astype(BF16)
    low = (r1 - mid.astype(F32)).astype(BF16)
    return hi, mid, low


def _inproj_kernel(x_ref, g_ref, wu_ref, wk_ref, wqt_ref, wvt_ref, fb_ref,
                   u_ref, k_ref, kb_ref, qt_ref, vt_ref, carry_ref):
    i = pl.program_id(0)

    @pl.when(i % (SEQ // TM_IN) == 0)
    def _():
        carry_ref[...] = jnp.zeros(carry_ref.shape, F32)

    qscale = LOG2E / math.sqrt(HEAD_DIM)
    r = lax.broadcasted_iota(jnp.int32, (CS_BLK, CS_BLK), 0)
    c = lax.broadcasted_iota(jnp.int32, (CS_BLK, CS_BLK), 1)
    triu = jnp.where(r <= c, 1.0, 0.0).astype(BF16)
    row = lax.broadcasted_iota(jnp.int32, (GATE_ROWS, CS_BLK), 0)
    carry = carry_ref[...]
    for t in range(TM_IN // TQ):
        tok = slice(t * TQ, (t + 1) * TQ)
        xn = _rms(x_ref[tok, :], g_ref[...]).astype(BF16)
        u_ref[tok, :] = _dot(xn, wu_ref[...])
        k_ref[tok, :] = _dot(xn, wk_ref[...]).astype(BF16)
        qf = lax.dot_general(wqt_ref[...], xn, NT_DIMS, preferred_element_type=F32)
        qt_ref[t] = (qf[0:D_ATTN] * qscale).astype(BF16)
        vt_ref[t] = lax.dot_general(wvt_ref[...], xn, NT_DIMS, preferred_element_type=F32).astype(BF16)

        for b in range(TQ // CS_BLK):
            f = qf[D_ATTN:D_ATTN + GATE_ROWS, b * CS_BLK:(b + 1) * CS_BLK] + fb_ref[...]
            lf = jnp.minimum(f, 0.0) - jnp.log1p(jnp.exp(-jnp.abs(f)))
            hi, mid, low = _split3(lf)
            local = _dot(hi, triu) + _dot(mid, triu) + _dot(low, triu)
            cs = local + carry
            carry = carry + jnp.broadcast_to(local[:, CS_BLK - 1:CS_BLK], (GATE_ROWS, CS_BLK))
            p0, p1, p2 = _split3(cs * (-LOG2E))
            piece = jnp.where(row < N_HEADS, p0, jnp.where(row < 2 * N_HEADS, p1, p2)).astype(F32)
            piece = jnp.where(row < 3 * N_HEADS, piece, 0.0)
            wide = jnp.concatenate([piece, jnp.zeros((LANES - GATE_ROWS, CS_BLK), F32)], axis=0)
            lo = t * TQ + b * CS_BLK
            kb_ref[lo:lo + CS_BLK, :] = wide.T.astype(BF16)
    carry_ref[...] = carry


def _inproj(h, g, wu, wk, wqt, wvt, fb):
    n = TOKENS // TM_IN
    fixed = lambda i: (0, 0)
    return pl.pallas_call(
        _inproj_kernel,
        grid=(n,),
        in_specs=[
            pl.BlockSpec((TM_IN, D_MODEL), lambda i: (i, 0)),
            pl.BlockSpec((1, D_MODEL), fixed),
            pl.BlockSpec((D_MODEL, D_SSM), fixed),
            pl.BlockSpec((D_MODEL, D_ATTN), fixed),
            pl.BlockSpec((D_ATTN + GATE_ROWS, D_MODEL), fixed),
            pl.BlockSpec((D_ATTN, D_MODEL), fixed),
            pl.BlockSpec((GATE_ROWS, CS_BLK), fixed),
        ],
        out_specs=[
            pl.BlockSpec((TM_IN, D_SSM), lambda i: (i, 0)),
            pl.BlockSpec((TM_IN, D_ATTN), lambda i: (i, 0)),
            pl.BlockSpec((TM_IN, LANES), lambda i: (i, 0)),
            pl.BlockSpec((TM_IN // TQ, D_ATTN, TQ), lambda i: (i, 0, 0)),
            pl.BlockSpec((TM_IN // TQ, D_ATTN, TQ), lambda i: (i, 0, 0)),
        ],
        out_shape=[
            jax.ShapeDtypeStruct((TOKENS, D_SSM), F32),
            jax.ShapeDtypeStruct((TOKENS, D_ATTN), BF16),
            jax.ShapeDtypeStruct((TOKENS, LANES), BF16),
            jax.ShapeDtypeStruct((TOKENS // TQ, D_ATTN, TQ), BF16),
            jax.ShapeDtypeStruct((TOKENS // TQ, D_ATTN, TQ), BF16),
        ],
        scratch_shapes=[pltpu.VMEM((GATE_ROWS, CS_BLK), F32)],
        compiler_params=pltpu.CompilerParams(
            dimension_semantics=("arbitrary",), vmem_limit_bytes=VMEM_LIMIT),
        name="inproj",
    )(h, g, wu, wk, wqt, wvt, fb)


def _ssm_kernel(u_ref, m_ref, ws_ref, wct_ref, a_ref, y_ref, s_ref, x_ref, xs_ref, yl_ref):
    nt = SSM_S // 2 // LANES
    tiles = range(nt)
    z = jnp.concatenate(
        [u_ref[pl.ds(s, SSM_R, stride=SSM_CHUNK), :].astype(BF16) for s in range(SSM_CHUNK)], axis=1)
    s = _dot(z, ws_ref[0])
    seg_rows = lambda g: pl.ds(g, SSM_SEG, stride=SUBLANES)
    for c in range(2 * nt):
        for g in range(SUBLANES):
            s_ref[c, seg_rows(g), :] = s[g * SSM_SEG:(g + 1) * SSM_SEG, c * LANES:(c + 1) * LANES]
    for c in range(SSM_W // MXU_N):
        hi = (c + 1) * MXU_N
        yl_ref[:, c * MXU_N:hi] = _dot(z[:, 0:hi], m_ref[0, 0:hi, c * MXU_N:hi])

    lane_tiles = lambda row: [a_ref[0, row:row + 1, c * LANES:(c + 1) * LANES] for c in tiles]
    full = lambda rows: [jnp.broadcast_to(r, (SUBLANES, LANES)) for r in rows]
    ar, ai = full(lane_tiles(0)), full(lane_tiles(1))
    gr, gi = lane_tiles(2), lane_tiles(3)

    def local(k, carry):
        cr, ci = carry
        rows = pl.ds(pl.multiple_of(k * SUBLANES, SUBLANES), SUBLANES)
        nr, ni = [], []
        for c in tiles:
            x_ref[c, rows, :] = cr[c]
            x_ref[nt + c, rows, :] = ci[c]
            nr.append(ar[c] * cr[c] - ai[c] * ci[c] + s_ref[c, rows, :])
            ni.append(ar[c] * ci[c] + ai[c] * cr[c] + s_ref[nt + c, rows, :])
        return tuple(nr), tuple(ni)

    zero = tuple(jnp.zeros((SUBLANES, LANES), F32) for _ in tiles)
    er, ei = lax.fori_loop(0, SSM_SEG, local, (zero, zero))

    for c in tiles:
        cr = jnp.zeros((1, LANES), F32)
        ci = jnp.zeros((1, LANES), F32)
        for g in range(SUBLANES):
            xs_ref[c, g:g + 1, :] = cr
            xs_ref[nt + c, g:g + 1, :] = ci
            cr, ci = (gr[c] * cr - gi[c] * ci + er[c][g:g + 1], gr[c] * ci + gi[c] * cr + ei[c][g:g + 1])
    xr = [xs_ref[c] for c in tiles]
    xi = [xs_ref[nt + c] for c in tiles]

    def fix(k, carry):
        pr, pi = carry
        rows = pl.ds(pl.multiple_of(k * SUBLANES, SUBLANES), SUBLANES)
        nr, ni = [], []
        for c in tiles:
            x_ref[c, rows, :] += pr[c] * xr[c] - pi[c] * xi[c]
            x_ref[nt + c, rows, :] += pr[c] * xi[c] + pi[c] * xr[c]
            nr.append(pr[c] * ar[c] - pi[c] * ai[c])
            ni.append(pr[c] * ai[c] + pi[c] * ar[c])
        return tuple(nr), tuple(ni)

    one = tuple(jnp.ones((SUBLANES, LANES), F32) for _ in tiles)
    lax.fori_loop(0, SSM_SEG, fix, (one, zero))

    x = jnp.concatenate(
        [jnp.concatenate([x_ref[c, seg_rows(g), :] for g in range(SUBLANES)], axis=0).astype(BF16)
         for c in range(2 * nt)], axis=1)
    y = yl_ref[...] + lax.dot_general(x, wct_ref[0], NT_DIMS, preferred_element_type=F32)
    for t in range(SSM_CHUNK):
        y_ref[pl.ds(t, SSM_R, stride=SSM_CHUNK), :] = y[:, t * LANES:(t + 1) * LANES]


def _ssm(u, m, ws, wct, apow, layer):
    nt = TOKENS // SSM_TM
    blk = lambda j, i: (layer * SSM_NB + j, 0, 0)
    return pl.pallas_call(
        _ssm_kernel,
        grid=(SSM_NB, nt),
        in_specs=[
            pl.BlockSpec((SSM_TM, LANES), lambda j, i: (i, j)),
            pl.BlockSpec((1, SSM_W, SSM_W), blk),
            pl.BlockSpec((1, SSM_W, SSM_S), blk),
            pl.BlockSpec((1, SSM_W, SSM_S), blk),
            pl.BlockSpec((1, 4, SSM_S // 2), blk),
        ],
        out_specs=pl.BlockSpec((SSM_TM, LANES), lambda j, i: (i, j)),
        out_shape=jax.ShapeDtypeStruct((TOKENS, D_SSM), F32),
        scratch_shapes=[pltpu.VMEM((SSM_S // LANES, SSM_R, LANES), F32),
                        pltpu.VMEM((SSM_S // LANES, SSM_R, LANES), F32),
                        pltpu.VMEM((SSM_S // LANES, SUBLANES, LANES), F32),
                        pltpu.VMEM((SSM_R, SSM_W), F32)],
        compiler_params=pltpu.CompilerParams(
            dimension_semantics=("arbitrary", "arbitrary"), vmem_limit_bytes=VMEM_LIMIT),
        name="ssm_scan",
    )(u, m, ws, wct, apow)


def _split2(x):
    hi = x.astype(BF16)
    return hi, (x - hi.astype(F32)).astype(BF16)


def _dot_nt_f32(a, b):
    ah, al = _split2(a)
    bh, bl = _split2(b)
    d = lambda x, y: lax.dot_general(x, y, NT_DIMS, preferred_element_type=F32)
    return d(ah, bh) + d(ah, bl) + d(al, bh)


def _ssm_prep_kernel(lam_ref, b_ref, c_ref, d_ref, m_ref, ws_ref, wct_ref, a_ref):
    half = SSM_S // 2
    tc = SSM_CHUNK
    ldt, lre, lim = lam_ref[0, 0:1, :], lam_ref[0, 1:2, :], lam_ref[0, 2:3, :]
    dt = jnp.exp(ldt)
    tau = lax.broadcasted_iota(jnp.int32, (2 * SUBLANES, half), 0).astype(F32)
    pmag = jnp.exp(tau * (lre * dt))
    p_re = pmag * jnp.cos(tau * (lim * dt))
    p_im = pmag * jnp.sin(tau * (lim * dt))
    a_re, a_im = p_re[1:2], p_im[1:2]
    den = lre * lre + lim * lim
    nr = a_re - 1.0
    s_re = (nr * lre + a_im * lim) / den
    s_im = (a_im * lre - nr * lim) / den
    b_re, b_im = b_ref[0, 0], b_ref[0, 1]
    bb_re = s_re * b_re - s_im * b_im
    bb_im = s_re * b_im + s_im * b_re
    c_re, c_im = c_ref[0, 0], c_ref[0, 1]
    eye = (lax.broadcasted_iota(jnp.int32, (LANES, LANES), 0)
           == lax.broadcasted_iota(jnp.int32, (LANES, LANES), 1))

    m_ref[...] = jnp.zeros(m_ref.shape, BF16)
    for lag in range(tc):
        pr, pi = p_re[lag:lag + 1], p_im[lag:lag + 1]
        l_re = pr * bb_re - pi * bb_im
        l_im = pr * bb_im + pi * bb_re
        rows = slice((tc - 1 - lag) * LANES, (tc - lag) * LANES)
        ws_ref[0, rows, 0:half] = l_re.astype(BF16)
        ws_ref[0, rows, half:SSM_S] = l_im.astype(BF16)
        x = _dot_nt_f32(l_re, c_re) - _dot_nt_f32(l_im, c_im)
        if lag == 0:
            x = x + jnp.where(eye, d_ref[0], 0.0)
        x = x.astype(BF16)
        for s in range(tc - lag):
            m_ref[0, s * LANES:(s + 1) * LANES, (s + lag) * LANES:(s + lag + 1) * LANES] = x
    for t in range(tc):
        pr, pi = p_re[t + 1:t + 2], p_im[t + 1:t + 2]
        rows = slice(t * LANES, (t + 1) * LANES)
        wct_ref[0, rows, 0:half] = (c_re * pr - c_im * pi).astype(BF16)
        wct_ref[0, rows, half:SSM_S] = (-(c_re * pi + c_im * pr)).astype(BF16)

    qr, qi = p_re[tc:tc + 1], p_im[tc:tc + 1]
    a_ref[0, 0:1, :] = qr
    a_ref[0, 1:2, :] = qi
    for _ in range(SSM_SEG.bit_length() - 1):
        qr, qi = qr * qr - qi * qi, 2.0 * qr * qi
    a_ref[0, 2:3, :] = qr
    a_ref[0, 3:4, :] = qi


def _ssm_prep(lam, bexp, cexp, dexp):
    n = lam.shape[0]
    blk = lambda *shape: pl.BlockSpec((1,) + shape, lambda j: (j,) + (0,) * len(shape))
    big = jax.ShapeDtypeStruct((n, SSM_W, SSM_S), BF16)
    return pl.pallas_call(
        _ssm_prep_kernel,
        grid=(n,),
        in_specs=[blk(3, SSM_S // 2), blk(2, LANES, SSM_S // 2), blk(2, LANES, SSM_S // 2), blk(1, LANES)],
        out_specs=[blk(SSM_W, SSM_W), blk(SSM_W, SSM_S), blk(SSM_W, SSM_S), blk(4, SSM_S // 2)],
        out_shape=[big, big, big, jax.ShapeDtypeStruct((n, 4, SSM_S // 2), F32)],
        compiler_params=pltpu.CompilerParams(
            dimension_semantics=("arbitrary",), vmem_limit_bytes=VMEM_LIMIT),
        name="ssm_prep",
    )(lam, bexp, cexp, dexp)


def _ssm_operands(log_dt, lam_re, lam_im, b_re, b_im, c_re, c_im, d_skip):
    nb, gb = SSM_NB, SSM_GB
    n = DEPTH * nb
    eye_g = jnp.eye(gb, dtype=F32)[None, :, None, :, None]
    ldt = jnp.broadcast_to(log_dt[..., None], (DEPTH, SSM_G, SSM_N))
    lam = jnp.stack([ldt, lam_re, lam_im], axis=1).reshape(DEPTH, 3, nb, gb * SSM_N)
    lam = lam.transpose(0, 2, 1, 3).reshape(n, 3, gb * SSM_N)

    def expand(w):
        return (w[:, :, :, None, :] * eye_g).reshape(n, LANES, gb * SSM_N)

    bt = lambda b: b.reshape(n, gb, SSM_N, SSM_P).transpose(0, 1, 3, 2)
    ct = lambda c: c.reshape(n, gb, SSM_P, SSM_N)
    bexp = jnp.stack([expand(bt(b_re)), expand(bt(b_im))], axis=1)
    cexp = jnp.stack([expand(ct(c_re)), expand(ct(c_im))], axis=1)
    return lam, bexp, cexp, d_skip.reshape(n, 1, LANES)


def _attn_kernel(k_ref, kb_ref, qt_ref, vt_ref, o_ref, rhs_ref, sa_ref, sb_ref, xa_ref, xb_ref,
                 acc_ref, m_ref):
    p = pl.program_id(1)
    row = lax.broadcasted_iota(jnp.int32, (LANES, TQ), 0)
    wide_row = lax.broadcasted_iota(jnp.int32, (LANES, QW), 0)
    in_head = []
    for hh in range(2):
        h = 2 * p + hh
        in_head.append((row >= hh * HEAD_DIM) & (row < (hh + 1) * HEAD_DIM))
        pick = (wide_row == h) | (wide_row == h + N_HEADS) | (wide_row == h + 2 * N_HEADS)
        rhs_ref[hh, LANES:2 * LANES, :] = jnp.where(pick, 1.0, 0.0).astype(BF16)

    key = lax.broadcasted_iota(jnp.int32, (TQ, QW), 0)
    qry = lax.broadcasted_iota(jnp.int32, (TQ, QW), 1)
    causal = key <= qry
    ones_row = jnp.where(lax.broadcasted_iota(jnp.int32, (V_PAD, TQ), 0) == 0, 1.0, 0.0).astype(BF16)
    both = (0, 1)
    full = slice(0, QW)
    upper = slice(TQ, QW)

    def scores(j, s_ref, x_ref, heads, lanes=full):
        rows = pl.ds(pl.multiple_of(j * TQ, TQ), TQ)
        lhs = jnp.concatenate([k_ref[rows, :], kb_ref[rows, :]], axis=1)
        for hh in heads:
            st = _dot(lhs, rhs_ref[hh, :, lanes])
            s_ref[hh, :, lanes] = st
            x_ref[hh, :, lanes] = jnp.max(st, axis=0, keepdims=True)

    def update(j, s_ref, x_ref, mask, heads, lanes=full):
        for hh in heads:
            st = s_ref[hh, :, lanes]
            if mask is None:
                blk_max = x_ref[hh, :, lanes]
            else:
                st = jnp.where(mask, st, NEG_BIG)
                blk_max = jnp.max(st, axis=0, keepdims=True)
            m_prev = m_ref[hh, :, lanes]
            m_next = jnp.maximum(m_prev, blk_max)
            alpha = jnp.exp2(m_prev - m_next)
            pt = jnp.exp2(st - m_next).astype(BF16)
            vt = jnp.concatenate([vt_ref[j, hh * HEAD_DIM:(hh + 1) * HEAD_DIM, :], ones_row], axis=0)
            acc_ref[hh, :, lanes] = alpha * acc_ref[hh, :, lanes] + _dot(vt, pt)
            m_ref[hh, :, lanes] = m_next

    def body(jj, carry):
        j = 2 * jj
        for hh in both:
            scores(j + 1, sb_ref, xb_ref, (hh,))
            update(j, sa_ref, xa_ref, None, (hh,))
        for hh in both:
            scores(j + 2, sa_ref, xa_ref, (hh,))
            update(j + 1, sb_ref, xb_ref, None, (hh,))
        return carry

    def tile(i, carry):
        for t in range(QW // TQ):
            qt = qt_ref[(QW // TQ) * i + t]
            for hh in both:
                rhs_ref[hh, 0:LANES, t * TQ:(t + 1) * TQ] = jnp.where(in_head[hh], qt, jnp.zeros_like(qt))
        m_ref[...] = jnp.full(m_ref.shape, NEG_BIG, F32)
        acc_ref[...] = jnp.zeros(acc_ref.shape, F32)

        scores(0, sa_ref, xa_ref, both)
        lax.fori_loop(0, i, body, 0)
        for hh in both:
            scores(2 * i + 1, sb_ref, xb_ref, (hh,), upper)
            update(2 * i, sa_ref, xa_ref, causal, (hh,))
        update(2 * i + 1, sb_ref, xb_ref, causal[:, 0:TQ], both, upper)

        for hh in both:
            acc = acc_ref[hh]
            out = (acc[0:HEAD_DIM] / acc[HEAD_DIM:HEAD_DIM + 1]).astype(BF16)
            for t in range(QW // TQ):
                o_ref[(QW // TQ) * i + t, hh * HEAD_DIM:(hh + 1) * HEAD_DIM, :] = out[:, t * TQ:(t + 1) * TQ]
        return carry

    lax.fori_loop(0, SEQ // QW, tile, 0)


def _attention(k, kb, qt, vt):
    nq = SEQ // TQ
    return pl.pallas_call(
        _attn_kernel,
        grid=(BATCH, N_HEADS // 2),
        in_specs=[
            pl.BlockSpec((SEQ, LANES), lambda b, p: (b, p)),
            pl.BlockSpec((SEQ, LANES), lambda b, p: (b, 0)),
            pl.BlockSpec((nq, LANES, TQ), lambda b, p: (b, p, 0)),
            pl.BlockSpec((nq, LANES, TQ), lambda b, p: (b, p, 0)),
        ],
        out_specs=pl.BlockSpec((nq, LANES, TQ), lambda b, p: (b, p, 0)),
        out_shape=jax.ShapeDtypeStruct((TOKENS // TQ, D_ATTN, TQ), BF16),
        scratch_shapes=[pltpu.VMEM((2, 2 * LANES, QW), BF16),
                        pltpu.VMEM((2, TQ, QW), F32),
                        pltpu.VMEM((2, TQ, QW), F32),
                        pltpu.VMEM((2, 1, QW), F32),
                        pltpu.VMEM((2, 1, QW), F32),
                        pltpu.VMEM((2, HEAD_DIM + V_PAD, QW), F32),
                        pltpu.VMEM((2, 1, QW), F32)],
        compiler_params=pltpu.CompilerParams(
            dimension_semantics=("arbitrary", "arbitrary"), vmem_limit_bytes=VMEM_LIMIT),
        name="fox_attention",
    )(k, kb, qt, vt)


def _mlp_kernel(apply_final, h_ref, ys_ref, oa_ref, gw_ref, gb_ref, gs_ref, ga_ref, wo_ref,
                l2_ref, wu_ref, wd_ref, fg_ref, o_ref):
    g = jax.nn.gelu(ys_ref[...])
    gate = 1.0 / (1.0 + jnp.exp(-(_dot(g.astype(BF16), gw_ref[...]) + gb_ref[...])))
    n_ssm = _rms(g * gate, gs_ref[...]).astype(BF16)
    oa = jnp.concatenate([oa_ref[t] for t in range(TM_MLP // TQ)], axis=1).astype(F32)
    inv = lax.rsqrt(jnp.mean(oa * oa, axis=0, keepdims=True) + EPS)
    n_att = (oa * inv * ga_ref[...]).astype(BF16)
    h1 = (h_ref[...] + _dot(n_ssm, wo_ref[0:D_SSM, :])
          + lax.dot_general(n_att, wo_ref[D_SSM:D_MODEL, :], TN_DIMS, preferred_element_type=F32))
    xn = _rms(h1, l2_ref[...]).astype(BF16)
    o_ref[...] = h1
    for c in range(D_FF // TF_MLP):
        cols = slice(c * TF_MLP, (c + 1) * TF_MLP)
        a = jnp.maximum(_dot(xn, wu_ref[:, cols]), 0.0)
        o_ref[...] += _dot((a * a).astype(BF16), wd_ref[cols, :])
    if apply_final:
        o_ref[...] = _rms(o_ref[...], fg_ref[...])


def _mlp(h, y_ssm, o_att, glu_w, glu_b, gn_s, gn_a, w_out, ln2, w_up, w_down, final_g, apply_final):
    nm = TOKENS // TM_MLP
    row = lambda i: (i, 0)
    fixed = lambda *shape: pl.BlockSpec(shape, lambda i: (0, 0), pipeline_mode=pl.Buffered(1))
    return pl.pallas_call(
        functools.partial(_mlp_kernel, apply_final),
        grid=(nm,),
        in_specs=[
            pl.BlockSpec((TM_MLP, D_MODEL), row),
            pl.BlockSpec((TM_MLP, D_SSM), row),
            pl.BlockSpec((TM_MLP // TQ, D_ATTN, TQ), lambda i: (i, 0, 0)),
            fixed(D_SSM, D_SSM),
            fixed(1, D_SSM),
            fixed(1, D_SSM),
            fixed(D_ATTN, 1),
            fixed(D_MODEL, D_MODEL),
            fixed(1, D_MODEL),
            fixed(D_MODEL, D_FF),
            fixed(D_FF, D_MODEL),
            fixed(1, D_MODEL),
        ],
        out_specs=pl.BlockSpec((TM_MLP, D_MODEL), row),
        out_shape=jax.ShapeDtypeStruct((TOKENS, D_MODEL), F32),
        compiler_params=pltpu.CompilerParams(
            dimension_semantics=("arbitrary",), vmem_limit_bytes=VMEM_LIMIT),
        name="mixer_out_mlp",
    )(h, y_ssm, o_att, glu_w, glu_b, gn_s, gn_a, w_out, ln2, w_up, w_down, final_g)


def kernel(x, ln1_g, w_in, ssm_log_dt, ssm_lambda_re, ssm_lambda_im, ssm_b_re, ssm_b_im, ssm_c_re, ssm_c_im, ssm_d, glu_w, glu_b, fgate_b, gn_ssm_g, gn_attn_g, w_out, ln2_g, w_up, w_down, final_g):
    assert TM_MLP % TQ == 0 and TM_IN % TQ == 0
    h = x.reshape(TOKENS, D_MODEL)
    mats = _ssm_prep(*_ssm_operands(ssm_log_dt, ssm_lambda_re, ssm_lambda_im, ssm_b_re, ssm_b_im,
                                    ssm_c_re, ssm_c_im, ssm_d))
    for l in range(DEPTH):
        w = w_in[l]
        w_u = w[:, :D_SSM].astype(BF16)
        w_gt = w[:, 4 * D_SSM:].T
        w_qt = jnp.concatenate([w[:, D_SSM:D_SSM + D_ATTN].T, w_gt, w_gt, w_gt,
                                jnp.zeros((GATE_ROWS - 3 * N_HEADS, D_MODEL), F32)], axis=0).astype(BF16)
        w_k = w[:, D_SSM + D_ATTN:D_SSM + 2 * D_ATTN].astype(BF16)
        w_vt = w[:, D_SSM + 2 * D_ATTN:D_SSM + 3 * D_ATTN].T.astype(BF16)
        b_gate = jnp.pad(jnp.concatenate([fgate_b[l]] * 3), (0, GATE_ROWS - 3 * N_HEADS))
        b_gate = jnp.broadcast_to(b_gate[:, None], (GATE_ROWS, CS_BLK))

        u, k, kb, qt, vt = _inproj(h, ln1_g[l].reshape(1, D_MODEL), w_u, w_k, w_qt, w_vt, b_gate)

        y_ssm = _ssm(u, *mats, l)
        o_att = _attention(k, kb, qt, vt)

        h = _mlp(h, y_ssm, o_att, glu_w[l].astype(BF16), glu_b[l].reshape(1, D_SSM),
                 gn_ssm_g[l].reshape(1, D_SSM), gn_attn_g[l].reshape(D_ATTN, 1),
                 w_out[l].astype(BF16), ln2_g[l].reshape(1, D_MODEL), w_up[l].astype(BF16),
                 w_down[l].astype(BF16), final_g.reshape(1, D_MODEL), l == DEPTH - 1)
    return h.reshape(BATCH, SEQ, D_MODEL)
```

```python
import functools
import math

import jax
import jax.numpy as jnp
from jax import lax
from jax.experimental import pallas as pl
from jax.experimental.pallas import tpu as pltpu

F32 = jnp.float32
BF16 = jnp.bfloat16

D_MODEL = 1024
BATCH = 4
SEQ = 4096
DEPTH = 2
TOKENS = BATCH * SEQ
D_SSM = 512
D_ATTN = 512
SSM_P = 16
SSM_G = 32
SSM_N = 64
HEAD_DIM = 64
N_HEADS = 8
D_FF = 4096
EPS = 1e-6
LOG2E = 1.4426950408889634

LANES = 128
SUBLANES = 8
MXU_N = 256
VMEM_LIMIT = 48 * 2 ** 20

SSM_CHUNK = SUBLANES
SSM_GB = LANES // SSM_P
SSM_NB = D_SSM // LANES
SSM_W = SSM_CHUNK * LANES
SSM_S = 2 * SSM_GB * SSM_N
SSM_TM = SEQ
SSM_R = SSM_TM // SSM_CHUNK
SSM_SEG = SSM_R // SUBLANES

TM_IN = 2048
TQ = 512
QW = 2 * TQ
TM_MLP = 512
TF_MLP = 1024
CS_BLK = LANES
GATE_ROWS = 32
V_PAD = 16
NEG_BIG = -1e30

NT_DIMS = (((1,), (1,)), ((), ()))
TN_DIMS = (((0,), (0,)), ((), ()))


def _rms(x, g):
    return x * lax.rsqrt(jnp.mean(x * x, axis=-1, keepdims=True) + EPS) * g


def _dot(a, b):
    return jnp.dot(a, b, preferred_element_type=F32)


def _split3(x):
    hi = x.astype(BF16)
    r1 = x - hi.astype(F32)
    mid = r1.astype(BF16)
    low = (r1 - mid.astype(F32)).astype(BF16)
    return hi, mid, low


def _inproj_kernel(x_ref, g_ref, wu_ref, wk_ref, wqt_ref, wvt_ref, fb_ref,
                   u_ref, k_ref, kb_ref, qt_ref, vt_ref, carry_ref):
    i = pl.program_id(0)

    @pl.when(i % (SEQ // TM_IN) == 0)
    def _():
        carry_ref[...] = jnp.zeros(carry_ref.shape, F32)

    qscale = LOG2E / math.sqrt(HEAD_DIM)
    r = lax.broadcasted_iota(jnp.int32, (CS_BLK, CS_BLK), 0)
    c = lax.broadcasted_iota(jnp.int32, (CS_BLK, CS_BLK), 1)
    triu = jnp.where(r <= c, 1.0, 0.0).astype(BF16)
    row = lax.broadcasted_iota(jnp.int32, (GATE_ROWS, CS_BLK), 0)
    carry = carry_ref[...]
    for t in range(TM_IN // TQ):
        tok = slice(t * TQ, (t + 1) * TQ)
        xn = _rms(x_ref[tok, :], g_ref[...]).astype(BF16)
        u_ref[tok, :] = _dot(xn, wu_ref[...])
        k_ref[tok, :] = _dot(xn, wk_ref[...]).astype(BF16)
        qf = lax.dot_general(wqt_ref[...], xn, NT_DIMS, preferred_element_type=F32)
        qt_ref[t] = (qf[0:D_ATTN] * qscale).astype(BF16)
        vt_ref[t] = lax.dot_general(wvt_ref[...], xn, NT_DIMS, preferred_element_type=F32).astype(BF16)

        for b in range(TQ // CS_BLK):
            f = qf[D_ATTN:D_ATTN + GATE_ROWS, b * CS_BLK:(b + 1) * CS_BLK] + fb_ref[...]
            lf = jnp.minimum(f, 0.0) - jnp.log1p(jnp.exp(-jnp.abs(f)))
            hi, mid, low = _split3(lf)
            local = _dot(hi, triu) + _dot(mid, triu) + _dot(low, triu)
            cs = local + carry
            carry = carry + jnp.broadcast_to(local[:, CS_BLK - 1:CS_BLK], (GATE_ROWS, CS_BLK))
            p0, p1, p2 = _split3(cs * (-LOG2E))
            piece = jnp.where(row < N_HEADS, p0, jnp.where(row < 2 * N_HEADS, p1, p2)).astype(F32)
            piece = jnp.where(row < 3 * N_HEADS, piece, 0.0)
            wide = jnp.concatenate([piece, jnp.zeros((LANES - GATE_ROWS, CS_BLK), F32)], axis=0)
            lo = t * TQ + b * CS_BLK
            kb_ref[lo:lo + CS_BLK, :] = wide.T.astype(BF16)
    carry_ref[...] = carry


def _inproj(h, g, wu, wk, wqt, wvt, fb):
    n = TOKENS // TM_IN
    fixed = lambda i: (0, 0)
    return pl.pallas_call(
        _inproj_kernel,
        grid=(n,),
        in_specs=[
            pl.BlockSpec((TM_IN, D_MODEL), lambda i: (i, 0)),
            pl.BlockSpec((1, D_MODEL), fixed),
            pl.BlockSpec((D_MODEL, D_SSM), fixed),
            pl.BlockSpec((D_MODEL, D_ATTN), fixed),
            pl.BlockSpec((D_ATTN + GATE_ROWS, D_MODEL), fixed),
            pl.BlockSpec((D_ATTN, D_MODEL), fixed),
            pl.BlockSpec((GATE_ROWS, CS_BLK), fixed),
        ],
        out_specs=[
            pl.BlockSpec((TM_IN, D_SSM), lambda i: (i, 0)),
            pl.BlockSpec((TM_IN, D_ATTN), lambda i: (i, 0)),
            pl.BlockSpec((TM_IN, LANES), lambda i: (i, 0)),
            pl.BlockSpec((TM_IN // TQ, D_ATTN, TQ), lambda i: (i, 0, 0)),
            pl.BlockSpec((TM_IN // TQ, D_ATTN, TQ), lambda i: (i, 0, 0)),
        ],
        out_shape=[
            jax.ShapeDtypeStruct((TOKENS, D_SSM), F32),
            jax.ShapeDtypeStruct((TOKENS, D_ATTN), BF16),
            jax.ShapeDtypeStruct((TOKENS, LANES), BF16),
            jax.ShapeDtypeStruct((TOKENS // TQ, D_ATTN, TQ), BF16),
            jax.ShapeDtypeStruct((TOKENS // TQ, D_ATTN, TQ), BF16),
        ],
        scratch_shapes=[pltpu.VMEM((GATE_ROWS, CS_BLK), F32)],
        compiler_params=pltpu.CompilerParams(
            dimension_semantics=("arbitrary",), vmem_limit_bytes=VMEM_LIMIT),
        name="inproj",
    )(h, g, wu, wk, wqt, wvt, fb)


def _ssm_kernel(u_ref, m_ref, ws_ref, wct_ref, a_ref, y_ref, s_ref, x_ref, xs_ref, yl_ref):
    nt = SSM_S // 2 // LANES
    tiles = range(nt)
    z = jnp.concatenate(
        [u_ref[pl.ds(s, SSM_R, stride=SSM_CHUNK), :].astype(BF16) for s in range(SSM_CHUNK)], axis=1)
    s = _dot(z, ws_ref[0])
    seg_rows = lambda g: pl.ds(g, SSM_SEG, stride=SUBLANES)
    for c in range(2 * nt):
        for g in range(SUBLANES):
            s_ref[c, seg_rows(g), :] = s[g * SSM_SEG:(g + 1) * SSM_SEG, c * LANES:(c + 1) * LANES]
    for c in range(SSM_W // MXU_N):
        hi = (c + 1) * MXU_N
        yl_ref[:, c * MXU_N:hi] = _dot(z[:, 0:hi], m_ref[0, 0:hi, c * MXU_N:hi])

    lane_tiles = lambda row: [a_ref[0, row:row + 1, c * LANES:(c + 1) * LANES] for c in tiles]
    full = lambda rows: [jnp.broadcast_to(r, (SUBLANES, LANES)) for r in rows]
    ar, ai = full(lane_tiles(0)), full(lane_tiles(1))
    gr, gi = lane_tiles(2), lane_tiles(3)

    def local(k, carry):
        cr, ci = carry
        rows = pl.ds(pl.multiple_of(k * SUBLANES, SUBLANES), SUBLANES)
        nr, ni = [], []
        for c in tiles:
            x_ref[c, rows, :] = cr[c]
            x_ref[nt + c, rows, :] = ci[c]
            nr.append(ar[c] * cr[c] - ai[c] * ci[c] + s_ref[c, rows, :])
            ni.append(ar[c] * ci[c] + ai[c] * cr[c] + s_ref[nt + c, rows, :])
        return tuple(nr), tuple(ni)

    zero = tuple(jnp.zeros((SUBLANES, LANES), F32) for _ in tiles)
    er, ei = lax.fori_loop(0, SSM_SEG, local, (zero, zero))

    for c in tiles:
        cr = jnp.zeros((1, LANES), F32)
        ci = jnp.zeros((1, LANES), F32)
        for g in range(SUBLANES):
            xs_ref[c, g:g + 1, :] = cr
            xs_ref[nt + c, g:g + 1, :] = ci
            cr, ci = (gr[c] * cr - gi[c] * ci + er[c][g:g + 1], gr[c] * ci + gi[c] * cr + ei[c][g:g + 1])
    xr = [xs_ref[c] for c in tiles]
    xi = [xs_ref[nt + c] for c in tiles]

    def fix(k, carry):
        pr, pi = carry
        rows = pl.ds(pl.multiple_of(k * SUBLANES, SUBLANES), SUBLANES)
        nr, ni = [], []
        for c in tiles:
            x_ref[c, rows, :] += pr[c] * xr[c] - pi[c] * xi[c]
            x_ref[nt + c, rows, :] += pr[c] * xi[c] + pi[c] * xr[c]
            nr.append(pr[c] * ar[c] - pi[c] * ai[c])
            ni.append(pr[c] * ai[c] + pi[c] * ar[c])
        return tuple(nr), tuple(ni)

    one = tuple(jnp.ones((SUBLANES, LANES), F32) for _ in tiles)
    lax.fori_loop(0, SSM_SEG, fix, (one, zero))

    x = jnp.concatenate(
        [jnp.concatenate([x_ref[c, seg_rows(g), :] for g in range(SUBLANES)], axis=0).astype(BF16)
         for c in range(2 * nt)], axis=1)
    y = yl_ref[...] + lax.dot_general(x, wct_ref[0], NT_DIMS, preferred_element_type=F32)
    for t in range(SSM_CHUNK):
        y_ref[pl.ds(t, SSM_R, stride=SSM_CHUNK), :] = y[:, t * LANES:(t + 1) * LANES]


def _ssm(u, m, ws, wct, apow, layer):
    nt = TOKENS // SSM_TM
    blk = lambda j, i: (layer * SSM_NB + j, 0, 0)
    return pl.pallas_call(
        _ssm_kernel,
        grid=(SSM_NB, nt),
        in_specs=[
            pl.BlockSpec((SSM_TM, LANES), lambda j, i: (i, j)),
            pl.BlockSpec((1, SSM_W, SSM_W), blk),
            pl.BlockSpec((1, SSM_W, SSM_S), blk),
            pl.BlockSpec((1, SSM_W, SSM_S), blk),
            pl.BlockSpec((1, 4, SSM_S // 2), blk),
        ],
        out_specs=pl.BlockSpec((SSM_TM, LANES), lambda j, i: (i, j)),
        out_shape=jax.ShapeDtypeStruct((TOKENS, D_SSM), F32),
        scratch_shapes=[pltpu.VMEM((SSM_S // LANES, SSM_R, LANES), F32),
                        pltpu.VMEM((SSM_S // LANES, SSM_R, LANES), F32),
                        pltpu.VMEM((SSM_S // LANES, SUBLANES, LANES), F32),
                        pltpu.VMEM((SSM_R, SSM_W), F32)],
        compiler_params=pltpu.CompilerParams(
            dimension_semantics=("arbitrary", "arbitrary"), vmem_limit_bytes=VMEM_LIMIT),
        name="ssm_scan",
    )(u, m, ws, wct, apow)


def _split2(x):
    hi = x.astype(BF16)
    return hi, (x - hi.astype(F32)).astype(BF16)


def _dot_nt_f32(a, b):
    ah, al = _split2(a)
    bh, bl = _split2(b)
    d = lambda x, y: lax.dot_general(x, y, NT_DIMS, preferred_element_type=F32)
    return d(ah, bh) + d(ah, bl) + d(al, bh)


def _ssm_prep_kernel(lam_ref, b_ref, c_ref, d_ref, m_ref, ws_ref, wct_ref, a_ref):
    half = SSM_S // 2
    tc = SSM_CHUNK
    ldt, lre, lim = lam_ref[0, 0:1, :], lam_ref[0, 1:2, :], lam_ref[0, 2:3, :]
    dt = jnp.exp(ldt)
    tau = lax.broadcasted_iota(jnp.int32, (2 * SUBLANES, half), 0).astype(F32)
    pmag = jnp.exp(tau * (lre * dt))
    p_re = pmag * jnp.cos(tau * (lim * dt))
    p_im = pmag * jnp.sin(tau * (lim * dt))
    a_re, a_im = p_re[1:2], p_im[1:2]
    den = lre * lre + lim * lim
    nr = a_re - 1.0
    s_re = (nr * lre + a_im * lim) / den
    s_im = (a_im * lre - nr * lim) / den
    b_re, b_im = b_ref[0, 0], b_ref[0, 1]
    bb_re = s_re * b_re - s_im * b_im
    bb_im = s_re * b_im + s_im * b_re
    c_re, c_im = c_ref[0, 0], c_ref[0, 1]
    eye = (lax.broadcasted_iota(jnp.int32, (LANES, LANES), 0)
           == lax.broadcasted_iota(jnp.int32, (LANES, LANES), 1))

    m_ref[...] = jnp.zeros(m_ref.shape, BF16)
    for lag in range(tc):
        pr, pi = p_re[lag:lag + 1], p_im[lag:lag + 1]
        l_re = pr * bb_re - pi * bb_im
        l_im = pr * bb_im + pi * bb_re
        rows = slice((tc - 1 - lag) * LANES, (tc - lag) * LANES)
        ws_ref[0, rows, 0:half] = l_re.astype(BF16)
        ws_ref[0, rows, half:SSM_S] = l_im.astype(BF16)
        x = _dot_nt_f32(l_re, c_re) - _dot_nt_f32(l_im, c_im)
        if lag == 0:
            x = x + jnp.where(eye, d_ref[0], 0.0)
        x = x.astype(BF16)
        for s in range(tc - lag):
            m_ref[0, s * LANES:(s + 1) * LANES, (s + lag) * LANES:(s + lag + 1) * LANES] = x
    for t in range(tc):
        pr, pi = p_re[t + 1:t + 2], p_im[t + 1:t + 2]
        rows = slice(t * LANES, (t + 1) * LANES)
        wct_ref[0, rows, 0:half] = (c_re * pr - c_im * pi).astype(BF16)
        wct_ref[0, rows, half:SSM_S] = (-(c_re * pi + c_im * pr)).astype(BF16)

    qr, qi = p_re[tc:tc + 1], p_im[tc:tc + 1]
    a_ref[0, 0:1, :] = qr
    a_ref[0, 1:2, :] = qi
    for _ in range(SSM_SEG.bit_length() - 1):
        qr, qi = qr * qr - qi * qi, 2.0 * qr * qi
    a_ref[0, 2:3, :] = qr
    a_ref[0, 3:4, :] = qi


def _ssm_prep(lam, bexp, cexp, dexp):
    n = lam.shape[0]
    blk = lambda *shape: pl.BlockSpec((1,) + shape, lambda j: (j,) + (0,) * len(shape))
    big = jax.ShapeDtypeStruct((n, SSM_W, SSM_S), BF16)
    return pl.pallas_call(
        _ssm_prep_kernel,
        grid=(n,),
        in_specs=[blk(3, SSM_S // 2), blk(2, LANES, SSM_S // 2), blk(2, LANES, SSM_S // 2), blk(1, LANES)],
        out_specs=[blk(SSM_W, SSM_W), blk(SSM_W, SSM_S), blk(SSM_W, SSM_S), blk(4, SSM_S // 2)],
        out_shape=[big, big, big, jax.ShapeDtypeStruct((n, 4, SSM_S // 2), F32)],
        compiler_params=pltpu.CompilerParams(
            dimension_semantics=("arbitrary",), vmem_limit_bytes=VMEM_LIMIT),
        name="ssm_prep",
    )(lam, bexp, cexp, dexp)


def _ssm_operands(log_dt, lam_re, lam_im, b_re, b_im, c_re, c_im, d_skip):
    nb, gb = SSM_NB, SSM_GB
    n = DEPTH * nb
    eye_g = jnp.eye(gb, dtype=F32)[None, :, None, :, None]
    ldt = jnp.broadcast_to(log_dt[..., None], (DEPTH, SSM_G, SSM_N))
    lam = jnp.stack([ldt, lam_re, lam_im], axis=1).reshape(DEPTH, 3, nb, gb * SSM_N)
    lam = lam.transpose(0, 2, 1, 3).reshape(n, 3, gb * SSM_N)

    def expand(w):
        return (w[:, :, :, None, :] * eye_g).reshape(n, LANES, gb * SSM_N)

    bt = lambda b: b.reshape(n, gb, SSM_N, SSM_P).transpose(0, 1, 3, 2)
    ct = lambda c: c.reshape(n, gb, SSM_P, SSM_N)
    bexp = jnp.stack([expand(bt(b_re)), expand(bt(b_im))], axis=1)
    cexp = jnp.stack([expand(ct(c_re)), expand(ct(c_im))], axis=1)
    return lam, bexp, cexp, d_skip.reshape(n, 1, LANES)


def _attn_kernel(k_ref, kb_ref, qt_ref, vt_ref, o_ref, rhs_ref, sa_ref, sb_ref, xa_ref, xb_ref,
                 acc_ref, m_ref):
    p = pl.program_id(1)
    row = lax.broadcasted_iota(jnp.int32, (LANES, TQ), 0)
    wide_row = lax.broadcasted_iota(jnp.int32, (LANES, QW), 0)
    in_head = []
    for hh in range(2):
        h = 2 * p + hh
        in_head.append((row >= hh * HEAD_DIM) & (row < (hh + 1) * HEAD_DIM))
        pick = (wide_row == h) | (wide_row == h + N_HEADS) | (wide_row == h + 2 * N_HEADS)
        rhs_ref[hh, LANES:2 * LANES, :] = jnp.where(pick, 1.0, 0.0).astype(BF16)

    key = lax.broadcasted_iota(jnp.int32, (TQ, QW), 0)
    qry = lax.broadcasted_iota(jnp.int32, (TQ, QW), 1)
    causal = key <= qry
    ones_row = jnp.where(lax.broadcasted_iota(jnp.int32, (V_PAD, TQ), 0) == 0, 1.0, 0.0).astype(BF16)
    both = (0, 1)
    full = slice(0, QW)
    upper = slice(TQ, QW)

    def scores(j, s_ref, x_ref, heads, lanes=full):
        rows = pl.ds(pl.multiple_of(j * TQ, TQ), TQ)
        lhs = jnp.concatenate([k_ref[rows, :], kb_ref[rows, :]], axis=1)
        for hh in heads:
            st = _dot(lhs, rhs_ref[hh, :, lanes])
            s_ref[hh, :, lanes] = st
            x_ref[hh, :, lanes] = jnp.max(st, axis=0, keepdims=True)

    def update(j, s_ref, x_ref, mask, heads, lanes=full):
        for hh in heads:
            st = s_ref[hh, :, lanes]
            if mask is None:
                blk_max = x_ref[hh, :, lanes]
            else:
                st = jnp.where(mask, st, NEG_BIG)
                blk_max = jnp.max(st, axis=0, keepdims=True)
            m_prev = m_ref[hh, :, lanes]
            m_next = jnp.maximum(m_prev, blk_max)
            alpha = jnp.exp2(m_prev - m_next)
            pt = jnp.exp2(st - m_next).astype(BF16)
            vt = jnp.concatenate([vt_ref[j, hh * HEAD_DIM:(hh + 1) * HEAD_DIM, :], ones_row], axis=0)
            acc_ref[hh, :, lanes] = alpha * acc_ref[hh, :, lanes] + _dot(vt, pt)
            m_ref[hh, :, lanes] = m_next

    def body(jj, carry):
        j = 2 * jj
        for hh in both:
            scores(j + 1, sb_ref, xb_ref, (hh,))
            update(j, sa_ref, xa_ref, None, (hh,))
        for hh in both:
            scores(j + 2, sa_ref, xa_ref, (hh,))
            update(j + 1, sb_ref, xb_ref, None, (hh,))
        return carry

    def tile(i, carry):
        for t in range(QW // TQ):
            qt = qt_ref[(QW // TQ) * i + t]
            for hh in both:
                rhs_ref[hh, 0:LANES, t * TQ:(t + 1) * TQ] = jnp.where(in_head[hh], qt, jnp.zeros_like(qt))
        m_ref[...] = jnp.full(m_ref.shape, NEG_BIG, F32)
        acc_ref[...] = jnp.zeros(acc_ref.shape, F32)

        scores(0, sa_ref, xa_ref, both)
        lax.fori_loop(0, i, body, 0)
        tri = causal[:, 0:TQ]
        for hh in both:
            scores(2 * i + 1, sb_ref, xb_ref, (hh,), upper)
            update(2 * i, sa_ref, xa_ref, tri, (hh,), slice(0, TQ))
            update(2 * i, sa_ref, xa_ref, None, (hh,), upper)
        update(2 * i + 1, sb_ref, xb_ref, tri, both, upper)

        for hh in both:
            acc = acc_ref[hh]
            out = (acc[0:HEAD_DIM] / acc[HEAD_DIM:HEAD_DIM + 1]).astype(BF16)
            for t in range(QW // TQ):
                o_ref[(QW // TQ) * i + t, hh * HEAD_DIM:(hh + 1) * HEAD_DIM, :] = out[:, t * TQ:(t + 1) * TQ]
        return carry

    lax.fori_loop(0, SEQ // QW, tile, 0)


def _attention(k, kb, qt, vt):
    nq = SEQ // TQ
    return pl.pallas_call(
        _attn_kernel,
        grid=(BATCH, N_HEADS // 2),
        in_specs=[
            pl.BlockSpec((SEQ, LANES), lambda b, p: (b, p)),
            pl.BlockSpec((SEQ, LANES), lambda b, p: (b, 0)),
            pl.BlockSpec((nq, LANES, TQ), lambda b, p: (b, p, 0)),
            pl.BlockSpec((nq, LANES, TQ), lambda b, p: (b, p, 0)),
        ],
        out_specs=pl.BlockSpec((nq, LANES, TQ), lambda b, p: (b, p, 0)),
        out_shape=jax.ShapeDtypeStruct((TOKENS // TQ, D_ATTN, TQ), BF16),
        scratch_shapes=[pltpu.VMEM((2, 2 * LANES, QW), BF16),
                        pltpu.VMEM((2, TQ, QW), F32),
                        pltpu.VMEM((2, TQ, QW), F32),
                        pltpu.VMEM((2, 1, QW), F32),
                        pltpu.VMEM((2, 1, QW), F32),
                        pltpu.VMEM((2, HEAD_DIM + V_PAD, QW), F32),
                        pltpu.VMEM((2, 1, QW), F32)],
        compiler_params=pltpu.CompilerParams(
            dimension_semantics=("arbitrary", "arbitrary"), vmem_limit_bytes=VMEM_LIMIT),
        name="fox_attention",
    )(k, kb, qt, vt)


def _mlp_kernel(apply_final, h_ref, ys_ref, oa_ref, gw_ref, gb_ref, gs_ref, ga_ref, wo_ref,
                l2_ref, wu_ref, wd_ref, fg_ref, o_ref):
    g = jax.nn.gelu(ys_ref[...])
    gate = 1.0 / (1.0 + jnp.exp(-(_dot(g.astype(BF16), gw_ref[...]) + gb_ref[...])))
    n_ssm = _rms(g * gate, gs_ref[...]).astype(BF16)
    oa = jnp.concatenate([oa_ref[t] for t in range(TM_MLP // TQ)], axis=1).astype(F32)
    inv = lax.rsqrt(jnp.mean(oa * oa, axis=0, keepdims=True) + EPS)
    n_att = (oa * inv * ga_ref[...]).astype(BF16)
    h1 = (h_ref[...] + _dot(n_ssm, wo_ref[0:D_SSM, :])
          + lax.dot_general(n_att, wo_ref[D_SSM:D_MODEL, :], TN_DIMS, preferred_element_type=F32))
    xn = _rms(h1, l2_ref[...]).astype(BF16)
    o_ref[...] = h1
    for c in range(D_FF // TF_MLP):
        cols = slice(c * TF_MLP, (c + 1) * TF_MLP)
        a = jnp.maximum(_dot(xn, wu_ref[:, cols]), 0.0)
        o_ref[...] += _dot((a * a).astype(BF16), wd_ref[cols, :])
    if apply_final:
        o_ref[...] = _rms(o_ref[...], fg_ref[...])


def _mlp(h, y_ssm, o_att, glu_w, glu_b, gn_s, gn_a, w_out, ln2, w_up, w_down, final_g, apply_final):
    nm = TOKENS // TM_MLP
    row = lambda i: (i, 0)
    fixed = lambda *shape: pl.BlockSpec(shape, lambda i: (0, 0), pipeline_mode=pl.Buffered(1))
    return pl.pallas_call(
        functools.partial(_mlp_kernel, apply_final),
        grid=(nm,),
        in_specs=[
            pl.BlockSpec((TM_MLP, D_MODEL), row),
            pl.BlockSpec((TM_MLP, D_SSM), row),
            pl.BlockSpec((TM_MLP // TQ, D_ATTN, TQ), lambda i: (i, 0, 0)),
            fixed(D_SSM, D_SSM),
            fixed(1, D_SSM),
            fixed(1, D_SSM),
            fixed(D_ATTN, 1),
            fixed(D_MODEL, D_MODEL),
            fixed(1, D_MODEL),
            fixed(D_MODEL, D_FF),
            fixed(D_FF, D_MODEL),
            fixed(1, D_MODEL),
        ],
        out_specs=pl.BlockSpec((TM_MLP, D_MODEL), row),
        out_shape=jax.ShapeDtypeStruct((TOKENS, D_MODEL), F32),
        compiler_params=pltpu.CompilerParams(
            dimension_semantics=("arbitrary",), vmem_limit_bytes=VMEM_LIMIT),
        name="mixer_out_mlp",
    )(h, y_ssm, o_att, glu_w, glu_b, gn_s, gn_a, w_out, ln2, w_up, w_down, final_g)


def kernel(x, ln1_g, w_in, ssm_log_dt, ssm_lambda_re, ssm_lambda_im, ssm_b_re, ssm_b_im, ssm_c_re, ssm_c_im, ssm_d, glu_w, glu_b, fgate_b, gn_ssm_g, gn_attn_g, w_out, ln2_g, w_up, w_down, final_g):
    assert TM_MLP % TQ == 0 and TM_IN % TQ == 0
    h = x.reshape(TOKENS, D_MODEL)
    mats = _ssm_prep(*_ssm_operands(ssm_log_dt, ssm_lambda_re, ssm_lambda_im, ssm_b_re, ssm_b_im,
                                    ssm_c_re, ssm_c_im, ssm_d))
    for l in range(DEPTH):
        w = w_in[l]
        w_u = w[:, :D_SSM].astype(BF16)
        w_gt = w[:, 4 * D_SSM:].T
        w_qt = jnp.concatenate([w[:, D_SSM:D_SSM + D_ATTN].T, w_gt, w_gt, w_gt,
                                jnp.zeros((GATE_ROWS - 3 * N_HEADS, D_MODEL), F32)], axis=0).astype(BF16)
        w_k = w[:, D_SSM + D_ATTN:D_SSM + 2 * D_ATTN].astype(BF16)
        w_vt = w[:, D_SSM + 2 * D_ATTN:D_SSM + 3 * D_ATTN].T.astype(BF16)
        b_gate = jnp.pad(jnp.concatenate([fgate_b[l]] * 3), (0, GATE_ROWS - 3 * N_HEADS))
        b_gate = jnp.broadcast_to(b_gate[:, None], (GATE_ROWS, CS_BLK))

        u, k, kb, qt, vt = _inproj(h, ln1_g[l].reshape(1, D_MODEL), w_u, w_k, w_qt, w_vt, b_gate)

        y_ssm = _ssm(u, *mats, l)
        o_att = _attention(k, kb, qt, vt)

        h = _mlp(h, y_ssm, o_att, glu_w[l].astype(BF16), glu_b[l].reshape(1, D_SSM),
                 gn_ssm_g[l].reshape(1, D_SSM), gn_attn_g[l].reshape(D_ATTN, 1),
                 w_out[l].astype(BF16), ln2_g[l].reshape(1, D_MODEL), w_up[l].astype(BF16),
                 w_down[l].astype(BF16), final_g.reshape(1, D_MODEL), l == DEPTH - 1)
    return h.reshape(BATCH, SEQ, D_MODEL)
```

```python
import functools
import math

import jax
import jax.numpy as jnp
from jax import lax
from jax.experimental import pallas as pl
from jax.experimental.pallas import tpu as pltpu

F32 = jnp.float32
BF16 = jnp.bfloat16

D_MODEL = 1024
BATCH = 4
SEQ = 4096
DEPTH = 2
TOKENS = BATCH * SEQ
D_SSM = 512
D_ATTN = 512
SSM_P = 16
SSM_G = 32
SSM_N = 64
HEAD_DIM = 64
N_HEADS = 8
D_FF = 4096
EPS = 1e-6
LOG2E = 1.4426950408889634

LANES = 128
SUBLANES = 8
MXU_N = 256
VMEM_LIMIT = 48 * 2 ** 20

SSM_CHUNK = SUBLANES
SSM_GB = LANES // SSM_P
SSM_NB = D_SSM // LANES
SSM_W = SSM_CHUNK * LANES
SSM_S = 2 * SSM_GB * SSM_N
SSM_TM = SEQ
SSM_R = SSM_TM // SSM_CHUNK
SSM_SEG = SSM_R // SUBLANES

TM_IN = 2048
TQ = 512
QW = 2 * TQ
TM_MLP = 512
TF_MLP = 1024
CS_BLK = LANES
GATE_ROWS = 32
V_PAD = 16
NEG_BIG = -1e30

NT_DIMS = (((1,), (1,)), ((), ()))
TN_DIMS = (((0,), (0,)), ((), ()))


def _rms(x, g):
    return x * lax.rsqrt(jnp.mean(x * x, axis=-1, keepdims=True) + EPS) * g


def _dot(a, b):
    return jnp.dot(a, b, preferred_element_type=F32)


def _split3(x):
    hi = x.astype(BF16)
    r1 = x - hi.astype(F32)
    mid = r1.astype(BF16)
    low = (r1 - mid.astype(F32)).astype(BF16)
    return hi, mid, low


def _inproj_kernel(x_ref, g_ref, wu_ref, wk_ref, wqt_ref, wvt_ref, fb_ref,
                   u_ref, k_ref, kb_ref, qt_ref, vt_ref, carry_ref):
    i = pl.program_id(0)

    @pl.when(i % (SEQ // TM_IN) == 0)
    def _():
        carry_ref[...] = jnp.zeros(carry_ref.shape, F32)

    qscale = LOG2E / math.sqrt(HEAD_DIM)
    r = lax.broadcasted_iota(jnp.int32, (CS_BLK, CS_BLK), 0)
    c = lax.broadcasted_iota(jnp.int32, (CS_BLK, CS_BLK), 1)
    triu = jnp.where(r <= c, 1.0, 0.0).astype(BF16)
    row = lax.broadcasted_iota(jnp.int32, (GATE_ROWS, CS_BLK), 0)
    carry = carry_ref[...]
    for t in range(TM_IN // TQ):
        tok = slice(t * TQ, (t + 1) * TQ)
        xn = _rms(x_ref[tok, :], g_ref[...]).astype(BF16)
        u = _dot(xn, wu_ref[...])
        for c in range(SSM_NB):
            u_ref[c, tok, :] = u[:, c * LANES:(c + 1) * LANES]
        k_ref[tok, :] = _dot(xn, wk_ref[...]).astype(BF16)
        qf = lax.dot_general(wqt_ref[...], xn, NT_DIMS, preferred_element_type=F32)
        qt_ref[t] = (qf[0:D_ATTN] * qscale).astype(BF16)
        vt_ref[t] = lax.dot_general(wvt_ref[...], xn, NT_DIMS, preferred_element_type=F32).astype(BF16)

        for b in range(TQ // CS_BLK):
            f = qf[D_ATTN:D_ATTN + GATE_ROWS, b * CS_BLK:(b + 1) * CS_BLK] + fb_ref[...]
            lf = jnp.minimum(f, 0.0) - jnp.log1p(jnp.exp(-jnp.abs(f)))
            hi, mid, low = _split3(lf)
            local = _dot(hi, triu) + _dot(mid, triu) + _dot(low, triu)
            cs = local + carry
            carry = carry + jnp.broadcast_to(local[:, CS_BLK - 1:CS_BLK], (GATE_ROWS, CS_BLK))
            p0, p1, p2 = _split3(cs * (-LOG2E))
            piece = jnp.where(row < N_HEADS, p0, jnp.where(row < 2 * N_HEADS, p1, p2)).astype(F32)
            piece = jnp.where(row < 3 * N_HEADS, piece, 0.0)
            wide = jnp.concatenate([piece, jnp.zeros((LANES - GATE_ROWS, CS_BLK), F32)], axis=0)
            lo = t * TQ + b * CS_BLK
            kb_ref[lo:lo + CS_BLK, :] = wide.T.astype(BF16)
    carry_ref[...] = carry


def _inproj(h, g, wu, wk, wqt, wvt, fb):
    n = TOKENS // TM_IN
    fixed = lambda i: (0, 0)
    return pl.pallas_call(
        _inproj_kernel,
        grid=(n,),
        in_specs=[
            pl.BlockSpec((TM_IN, D_MODEL), lambda i: (i, 0)),
            pl.BlockSpec((1, D_MODEL), fixed),
            pl.BlockSpec((D_MODEL, D_SSM), fixed),
            pl.BlockSpec((D_MODEL, D_ATTN), fixed),
            pl.BlockSpec((D_ATTN + GATE_ROWS, D_MODEL), fixed),
            pl.BlockSpec((D_ATTN, D_MODEL), fixed),
            pl.BlockSpec((GATE_ROWS, CS_BLK), fixed),
        ],
        out_specs=[
            pl.BlockSpec((SSM_NB, TM_IN, LANES), lambda i: (0, i, 0)),
            pl.BlockSpec((TM_IN, D_ATTN), lambda i: (i, 0)),
            pl.BlockSpec((TM_IN, LANES), lambda i: (i, 0)),
            pl.BlockSpec((TM_IN // TQ, D_ATTN, TQ), lambda i: (i, 0, 0)),
            pl.BlockSpec((TM_IN // TQ, D_ATTN, TQ), lambda i: (i, 0, 0)),
        ],
        out_shape=[
            jax.ShapeDtypeStruct((SSM_NB, TOKENS, LANES), F32),
            jax.ShapeDtypeStruct((TOKENS, D_ATTN), BF16),
            jax.ShapeDtypeStruct((TOKENS, LANES), BF16),
            jax.ShapeDtypeStruct((TOKENS // TQ, D_ATTN, TQ), BF16),
            jax.ShapeDtypeStruct((TOKENS // TQ, D_ATTN, TQ), BF16),
        ],
        scratch_shapes=[pltpu.VMEM((GATE_ROWS, CS_BLK), F32)],
        compiler_params=pltpu.CompilerParams(
            dimension_semantics=("arbitrary",), vmem_limit_bytes=VMEM_LIMIT),
        name="inproj",
    )(h, g, wu, wk, wqt, wvt, fb)


def _ssm_kernel(u_ref, m_ref, ws_ref, wct_ref, a_ref, y_ref, s_ref, x_ref, xs_ref, yl_ref):
    nt = SSM_S // 2 // LANES
    tiles = range(nt)
    z = jnp.concatenate(
        [u_ref[pl.ds(s, SSM_R, stride=SSM_CHUNK), :].astype(BF16) for s in range(SSM_CHUNK)], axis=1)
    s = _dot(z, ws_ref[0])
    seg_rows = lambda g: pl.ds(g, SSM_SEG, stride=SUBLANES)
    for c in range(2 * nt):
        for g in range(SUBLANES):
            s_ref[c, seg_rows(g), :] = s[g * SSM_SEG:(g + 1) * SSM_SEG, c * LANES:(c + 1) * LANES]
    for c in range(SSM_W // MXU_N):
        hi = (c + 1) * MXU_N
        yl_ref[:, c * MXU_N:hi] = _dot(z[:, 0:hi], m_ref[0, 0:hi, c * MXU_N:hi])

    lane_tiles = lambda row: [a_ref[0, row:row + 1, c * LANES:(c + 1) * LANES] for c in tiles]
    full = lambda rows: [jnp.broadcast_to(r, (SUBLANES, LANES)) for r in rows]
    ar, ai = full(lane_tiles(0)), full(lane_tiles(1))
    gr, gi = lane_tiles(2), lane_tiles(3)

    def local(k, carry):
        cr, ci = carry
        rows = pl.ds(pl.multiple_of(k * SUBLANES, SUBLANES), SUBLANES)
        nr, ni = [], []
        for c in tiles:
            x_ref[c, rows, :] = cr[c]
            x_ref[nt + c, rows, :] = ci[c]
            nr.append(ar[c] * cr[c] - ai[c] * ci[c] + s_ref[c, rows, :])
            ni.append(ar[c] * ci[c] + ai[c] * cr[c] + s_ref[nt + c, rows, :])
        return tuple(nr), tuple(ni)

    zero = tuple(jnp.zeros((SUBLANES, LANES), F32) for _ in tiles)
    er, ei = lax.fori_loop(0, SSM_SEG, local, (zero, zero))

    for c in tiles:
        cr = jnp.zeros((1, LANES), F32)
        ci = jnp.zeros((1, LANES), F32)
        for g in range(SUBLANES):
            xs_ref[c, g:g + 1, :] = cr
            xs_ref[nt + c, g:g + 1, :] = ci
            cr, ci = (gr[c] * cr - gi[c] * ci + er[c][g:g + 1], gr[c] * ci + gi[c] * cr + ei[c][g:g + 1])
    xr = [xs_ref[c] for c in tiles]
    xi = [xs_ref[nt + c] for c in tiles]

    def fix(k, carry):
        pr, pi = carry
        rows = pl.ds(pl.multiple_of(k * SUBLANES, SUBLANES), SUBLANES)
        nr, ni = [], []
        for c in tiles:
            x_ref[c, rows, :] += pr[c] * xr[c] - pi[c] * xi[c]
            x_ref[nt + c, rows, :] += pr[c] * xi[c] + pi[c] * xr[c]
            nr.append(pr[c] * ar[c] - pi[c] * ai[c])
            ni.append(pr[c] * ai[c] + pi[c] * ar[c])
        return tuple(nr), tuple(ni)

    one = tuple(jnp.ones((SUBLANES, LANES), F32) for _ in tiles)
    lax.fori_loop(0, SSM_SEG, fix, (one, zero))

    x = jnp.concatenate(
        [jnp.concatenate([x_ref[c, seg_rows(g), :] for g in range(SUBLANES)], axis=0).astype(BF16)
         for c in range(2 * nt)], axis=1)
    y = yl_ref[...] + lax.dot_general(x, wct_ref[0], NT_DIMS, preferred_element_type=F32)
    for t in range(SSM_CHUNK):
        y_ref[pl.ds(t, SSM_R, stride=SSM_CHUNK), :] = y[:, t * LANES:(t + 1) * LANES]


def _ssm(u, m, ws, wct, apow, layer):
    nt = TOKENS // SSM_TM
    blk = lambda j, i: (layer * SSM_NB + j, 0, 0)
    return pl.pallas_call(
        _ssm_kernel,
        grid=(SSM_NB, nt),
        in_specs=[
            pl.BlockSpec((None, SSM_TM, LANES), lambda j, i: (j, i, 0)),
            pl.BlockSpec((1, SSM_W, SSM_W), blk),
            pl.BlockSpec((1, SSM_W, SSM_S), blk),
            pl.BlockSpec((1, SSM_W, SSM_S), blk),
            pl.BlockSpec((1, 4, SSM_S // 2), blk),
        ],
        out_specs=pl.BlockSpec((None, SSM_TM, LANES), lambda j, i: (j, i, 0)),
        out_shape=jax.ShapeDtypeStruct((SSM_NB, TOKENS, LANES), F32),
        scratch_shapes=[pltpu.VMEM((SSM_S // LANES, SSM_R, LANES), F32),
                        pltpu.VMEM((SSM_S // LANES, SSM_R, LANES), F32),
                        pltpu.VMEM((SSM_S // LANES, SUBLANES, LANES), F32),
                        pltpu.VMEM((SSM_R, SSM_W), F32)],
        compiler_params=pltpu.CompilerParams(
            dimension_semantics=("arbitrary", "arbitrary"), vmem_limit_bytes=VMEM_LIMIT),
        name="ssm_scan",
    )(u, m, ws, wct, apow)


def _split2(x):
    hi = x.astype(BF16)
    return hi, (x - hi.astype(F32)).astype(BF16)


def _dot_nt_f32(a, b):
    ah, al = _split2(a)
    bh, bl = _split2(b)
    d = lambda x, y: lax.dot_general(x, y, NT_DIMS, preferred_element_type=F32)
    return d(ah, bh) + d(ah, bl) + d(al, bh)


def _ssm_prep_kernel(lam_ref, b_ref, c_ref, d_ref, m_ref, ws_ref, wct_ref, a_ref):
    half = SSM_S // 2
    tc = SSM_CHUNK
    ldt, lre, lim = lam_ref[0, 0:1, :], lam_ref[0, 1:2, :], lam_ref[0, 2:3, :]
    dt = jnp.exp(ldt)
    tau = lax.broadcasted_iota(jnp.int32, (2 * SUBLANES, half), 0).astype(F32)
    pmag = jnp.exp(tau * (lre * dt))
    p_re = pmag * jnp.cos(tau * (lim * dt))
    p_im = pmag * jnp.sin(tau * (lim * dt))
    a_re, a_im = p_re[1:2], p_im[1:2]
    den = lre * lre + lim * lim
    nr = a_re - 1.0
    s_re = (nr * lre + a_im * lim) / den
    s_im = (a_im * lre - nr * lim) / den
    b_re, b_im = b_ref[0, 0], b_ref[0, 1]
    bb_re = s_re * b_re - s_im * b_im
    bb_im = s_re * b_im + s_im * b_re
    c_re, c_im = c_ref[0, 0], c_ref[0, 1]
    eye = (lax.broadcasted_iota(jnp.int32, (LANES, LANES), 0)
           == lax.broadcasted_iota(jnp.int32, (LANES, LANES), 1))

    m_ref[...] = jnp.zeros(m_ref.shape, BF16)
    for lag in range(tc):
        pr, pi = p_re[lag:lag + 1], p_im[lag:lag + 1]
        l_re = pr * bb_re - pi * bb_im
        l_im = pr * bb_im + pi * bb_re
        rows = slice((tc - 1 - lag) * LANES, (tc - lag) * LANES)
        ws_ref[0, rows, 0:half] = l_re.astype(BF16)
        ws_ref[0, rows, half:SSM_S] = l_im.astype(BF16)
        x = _dot_nt_f32(l_re, c_re) - _dot_nt_f32(l_im, c_im)
        if lag == 0:
            x = x + jnp.where(eye, d_ref[0], 0.0)
        x = x.astype(BF16)
        for s in range(tc - lag):
            m_ref[0, s * LANES:(s + 1) * LANES, (s + lag) * LANES:(s + lag + 1) * LANES] = x
    for t in range(tc):
        pr, pi = p_re[t + 1:t + 2], p_im[t + 1:t + 2]
        rows = slice(t * LANES, (t + 1) * LANES)
        wct_ref[0, rows, 0:half] = (c_re * pr - c_im * pi).astype(BF16)
        wct_ref[0, rows, half:SSM_S] = (-(c_re * pi + c_im * pr)).astype(BF16)

    qr, qi = p_re[tc:tc + 1], p_im[tc:tc + 1]
    a_ref[0, 0:1, :] = qr
    a_ref[0, 1:2, :] = qi
    for _ in range(SSM_SEG.bit_length() - 1):
        qr, qi = qr * qr - qi * qi, 2.0 * qr * qi
    a_ref[0, 2:3, :] = qr
    a_ref[0, 3:4, :] = qi


def _ssm_prep(lam, bexp, cexp, dexp):
    n = lam.shape[0]
    blk = lambda *shape: pl.BlockSpec((1,) + shape, lambda j: (j,) + (0,) * len(shape))
    big = jax.ShapeDtypeStruct((n, SSM_W, SSM_S), BF16)
    return pl.pallas_call(
        _ssm_prep_kernel,
        grid=(n,),
        in_specs=[blk(3, SSM_S // 2), blk(2, LANES, SSM_S // 2), blk(2, LANES, SSM_S // 2), blk(1, LANES)],
        out_specs=[blk(SSM_W, SSM_W), blk(SSM_W, SSM_S), blk(SSM_W, SSM_S), blk(4, SSM_S // 2)],
        out_shape=[big, big, big, jax.ShapeDtypeStruct((n, 4, SSM_S // 2), F32)],
        compiler_params=pltpu.CompilerParams(
            dimension_semantics=("arbitrary",), vmem_limit_bytes=VMEM_LIMIT),
        name="ssm_prep",
    )(lam, bexp, cexp, dexp)


def _ssm_operands(log_dt, lam_re, lam_im, b_re, b_im, c_re, c_im, d_skip):
    nb, gb = SSM_NB, SSM_GB
    n = DEPTH * nb
    eye_g = jnp.eye(gb, dtype=F32)[None, :, None, :, None]
    ldt = jnp.broadcast_to(log_dt[..., None], (DEPTH, SSM_G, SSM_N))
    lam = jnp.stack([ldt, lam_re, lam_im], axis=1).reshape(DEPTH, 3, nb, gb * SSM_N)
    lam = lam.transpose(0, 2, 1, 3).reshape(n, 3, gb * SSM_N)

    def expand(w):
        return (w[:, :, :, None, :] * eye_g).reshape(n, LANES, gb * SSM_N)

    bt = lambda b: b.reshape(n, gb, SSM_N, SSM_P).transpose(0, 1, 3, 2)
    ct = lambda c: c.reshape(n, gb, SSM_P, SSM_N)
    bexp = jnp.stack([expand(bt(b_re)), expand(bt(b_im))], axis=1)
    cexp = jnp.stack([expand(ct(c_re)), expand(ct(c_im))], axis=1)
    return lam, bexp, cexp, d_skip.reshape(n, 1, LANES)


def _attn_kernel(k_ref, kb_ref, qt_ref, vt_ref, o_ref, rhs_ref, sa_ref, sb_ref, xa_ref, xb_ref,
                 acc_ref, m_ref):
    p = pl.program_id(1)
    row = lax.broadcasted_iota(jnp.int32, (LANES, TQ), 0)
    wide_row = lax.broadcasted_iota(jnp.int32, (LANES, QW), 0)
    in_head = []
    for hh in range(2):
        h = 2 * p + hh
        in_head.append((row >= hh * HEAD_DIM) & (row < (hh + 1) * HEAD_DIM))
        pick = (wide_row == h) | (wide_row == h + N_HEADS) | (wide_row == h + 2 * N_HEADS)
        rhs_ref[hh, LANES:2 * LANES, :] = jnp.where(pick, 1.0, 0.0).astype(BF16)

    key = lax.broadcasted_iota(jnp.int32, (TQ, QW), 0)
    qry = lax.broadcasted_iota(jnp.int32, (TQ, QW), 1)
    causal = key <= qry
    ones_row = jnp.where(lax.broadcasted_iota(jnp.int32, (V_PAD, TQ), 0) == 0, 1.0, 0.0).astype(BF16)
    both = (0, 1)
    full = slice(0, QW)
    upper = slice(TQ, QW)

    def scores(j, s_ref, x_ref, heads, lanes=full):
        rows = pl.ds(pl.multiple_of(j * TQ, TQ), TQ)
        lhs = jnp.concatenate([k_ref[rows, :], kb_ref[rows, :]], axis=1)
        for hh in heads:
            st = _dot(lhs, rhs_ref[hh, :, lanes])
            s_ref[hh, :, lanes] = st
            x_ref[hh, :, lanes] = jnp.max(st, axis=0, keepdims=True)

    def update(j, s_ref, x_ref, mask, heads, lanes=full):
        for hh in heads:
            st = s_ref[hh, :, lanes]
            if mask is None:
                blk_max = x_ref[hh, :, lanes]
            else:
                st = jnp.where(mask, st, NEG_BIG)
                blk_max = jnp.max(st, axis=0, keepdims=True)
            m_prev = m_ref[hh, :, lanes]
            m_next = jnp.maximum(m_prev, blk_max)
            alpha = jnp.exp2(m_prev - m_next)
            pt = jnp.exp2(st - m_next).astype(BF16)
            vt = jnp.concatenate([vt_ref[j, hh * HEAD_DIM:(hh + 1) * HEAD_DIM, :], ones_row], axis=0)
            acc_ref[hh, :, lanes] = alpha * acc_ref[hh, :, lanes] + _dot(vt, pt)
            m_ref[hh, :, lanes] = m_next

    def body(jj, carry):
        j = 2 * jj
        for hh in both:
            scores(j + 1, sb_ref, xb_ref, (hh,))
            update(j, sa_ref, xa_ref, None, (hh,))
        for hh in both:
            scores(j + 2, sa_ref, xa_ref, (hh,))
            update(j + 1, sb_ref, xb_ref, None, (hh,))
        return carry

    def tile(i, carry):
        for t in range(QW // TQ):
            qt = qt_ref[(QW // TQ) * i + t]
            for hh in both:
                rhs_ref[hh, 0:LANES, t * TQ:(t + 1) * TQ] = jnp.where(in_head[hh], qt, jnp.zeros_like(qt))
        m_ref[...] = jnp.full(m_ref.shape, NEG_BIG, F32)
        acc_ref[...] = jnp.zeros(acc_ref.shape, F32)

        scores(0, sa_ref, xa_ref, both)
        lax.fori_loop(0, i, body, 0)
        tri = causal[:, 0:TQ]
        for hh in both:
            scores(2 * i + 1, sb_ref, xb_ref, (hh,), upper)
            update(2 * i, sa_ref, xa_ref, tri, (hh,), slice(0, TQ))
            update(2 * i, sa_ref, xa_ref, None, (hh,), upper)
        update(2 * i + 1, sb_ref, xb_ref, tri, both, upper)

        for hh in both:
            acc = acc_ref[hh]
            out = (acc[0:HEAD_DIM] / acc[HEAD_DIM:HEAD_DIM + 1]).astype(BF16)
            for t in range(QW // TQ):
                o_ref[(QW // TQ) * i + t, hh * HEAD_DIM:(hh + 1) * HEAD_DIM, :] = out[:, t * TQ:(t + 1) * TQ]
        return carry

    lax.fori_loop(0, SEQ // QW, tile, 0)


def _attention(k, kb, qt, vt):
    nq = SEQ // TQ
    return pl.pallas_call(
        _attn_kernel,
        grid=(BATCH, N_HEADS // 2),
        in_specs=[
            pl.BlockSpec((SEQ, LANES), lambda b, p: (b, p)),
            pl.BlockSpec((SEQ, LANES), lambda b, p: (b, 0)),
            pl.BlockSpec((nq, LANES, TQ), lambda b, p: (b, p, 0)),
            pl.BlockSpec((nq, LANES, TQ), lambda b, p: (b, p, 0)),
        ],
        out_specs=pl.BlockSpec((nq, LANES, TQ), lambda b, p: (b, p, 0)),
        out_shape=jax.ShapeDtypeStruct((TOKENS // TQ, D_ATTN, TQ), BF16),
        scratch_shapes=[pltpu.VMEM((2, 2 * LANES, QW), BF16),
                        pltpu.VMEM((2, TQ, QW), F32),
                        pltpu.VMEM((2, TQ, QW), F32),
                        pltpu.VMEM((2, 1, QW), F32),
                        pltpu.VMEM((2, 1, QW), F32),
                        pltpu.VMEM((2, HEAD_DIM + V_PAD, QW), F32),
                        pltpu.VMEM((2, 1, QW), F32)],
        compiler_params=pltpu.CompilerParams(
            dimension_semantics=("arbitrary", "arbitrary"), vmem_limit_bytes=VMEM_LIMIT),
        name="fox_attention",
    )(k, kb, qt, vt)


def _mlp_kernel(apply_final, h_ref, ys_ref, oa_ref, gw_ref, gb_ref, gs_ref, ga_ref, wo_ref,
                l2_ref, wu_ref, wd_ref, fg_ref, o_ref):
    g = jax.nn.gelu(jnp.concatenate([ys_ref[c] for c in range(SSM_NB)], axis=1))
    gate = 1.0 / (1.0 + jnp.exp(-(_dot(g.astype(BF16), gw_ref[...]) + gb_ref[...])))
    n_ssm = _rms(g * gate, gs_ref[...]).astype(BF16)
    oa = jnp.concatenate([oa_ref[t] for t in range(TM_MLP // TQ)], axis=1).astype(F32)
    inv = lax.rsqrt(jnp.mean(oa * oa, axis=0, keepdims=True) + EPS)
    n_att = (oa * inv * ga_ref[...]).astype(BF16)
    h1 = (h_ref[...] + _dot(n_ssm, wo_ref[0:D_SSM, :])
          + lax.dot_general(n_att, wo_ref[D_SSM:D_MODEL, :], TN_DIMS, preferred_element_type=F32))
    xn = _rms(h1, l2_ref[...]).astype(BF16)
    o_ref[...] = h1
    for c in range(D_FF // TF_MLP):
        cols = slice(c * TF_MLP, (c + 1) * TF_MLP)
        a = jnp.maximum(_dot(xn, wu_ref[:, cols]), 0.0)
        o_ref[...] += _dot((a * a).astype(BF16), wd_ref[cols, :])
    if apply_final:
        o_ref[...] = _rms(o_ref[...], fg_ref[...])


def _mlp(h, y_ssm, o_att, glu_w, glu_b, gn_s, gn_a, w_out, ln2, w_up, w_down, final_g, apply_final):
    nm = TOKENS // TM_MLP
    row = lambda i: (i, 0)
    fixed = lambda *shape: pl.BlockSpec(shape, lambda i: (0, 0), pipeline_mode=pl.Buffered(1))
    return pl.pallas_call(
        functools.partial(_mlp_kernel, apply_final),
        grid=(nm,),
        in_specs=[
            pl.BlockSpec((TM_MLP, D_MODEL), row),
            pl.BlockSpec((SSM_NB, TM_MLP, LANES), lambda i: (0, i, 0)),
            pl.BlockSpec((TM_MLP // TQ, D_ATTN, TQ), lambda i: (i, 0, 0)),
            fixed(D_SSM, D_SSM),
            fixed(1, D_SSM),
            fixed(1, D_SSM),
            fixed(D_ATTN, 1),
            fixed(D_MODEL, D_MODEL),
            fixed(1, D_MODEL),
            fixed(D_MODEL, D_FF),
            fixed(D_FF, D_MODEL),
            fixed(1, D_MODEL),
        ],
        out_specs=pl.BlockSpec((TM_MLP, D_MODEL), row),
        out_shape=jax.ShapeDtypeStruct((TOKENS, D_MODEL), F32),
        compiler_params=pltpu.CompilerParams(
            dimension_semantics=("arbitrary",), vmem_limit_bytes=VMEM_LIMIT),
        name="mixer_out_mlp",
    )(h, y_ssm, o_att, glu_w, glu_b, gn_s, gn_a, w_out, ln2, w_up, w_down, final_g)


def kernel(x, ln1_g, w_in, ssm_log_dt, ssm_lambda_re, ssm_lambda_im, ssm_b_re, ssm_b_im, ssm_c_re, ssm_c_im, ssm_d, glu_w, glu_b, fgate_b, gn_ssm_g, gn_attn_g, w_out, ln2_g, w_up, w_down, final_g):
    assert TM_MLP % TQ == 0 and TM_IN % TQ == 0
    h = x.reshape(TOKENS, D_MODEL)
    mats = _ssm_prep(*_ssm_operands(ssm_log_dt, ssm_lambda_re, ssm_lambda_im, ssm_b_re, ssm_b_im,
                                    ssm_c_re, ssm_c_im, ssm_d))
    for l in range(DEPTH):
        w = w_in[l]
        w_u = w[:, :D_SSM].astype(BF16)
        w_gt = w[:, 4 * D_SSM:].T
        w_qt = jnp.concatenate([w[:, D_SSM:D_SSM + D_ATTN].T, w_gt, w_gt, w_gt,
                                jnp.zeros((GATE_ROWS - 3 * N_HEADS, D_MODEL), F32)], axis=0).astype(BF16)
        w_k = w[:, D_SSM + D_ATTN:D_SSM + 2 * D_ATTN].astype(BF16)
        w_vt = w[:, D_SSM + 2 * D_ATTN:D_SSM + 3 * D_ATTN].T.astype(BF16)
        b_gate = jnp.pad(jnp.concatenate([fgate_b[l]] * 3), (0, GATE_ROWS - 3 * N_HEADS))
        b_gate = jnp.broadcast_to(b_gate[:, None], (GATE_ROWS, CS_BLK))

        u, k, kb, qt, vt = _inproj(h, ln1_g[l].reshape(1, D_MODEL), w_u, w_k, w_qt, w_vt, b_gate)

        y_ssm = _ssm(u, *mats, l)
        o_att = _attention(k, kb, qt, vt)

        h = _mlp(h, y_ssm, o_att, glu_w[l].astype(BF16), glu_b[l].reshape(1, D_SSM),
                 gn_ssm_g[l].reshape(1, D_SSM), gn_attn_g[l].reshape(D_ATTN, 1),
                 w_out[l].astype(BF16), ln2_g[l].reshape(1, D_MODEL), w_up[l].astype(BF16),
                 w_down[l].astype(BF16), final_g.reshape(1, D_MODEL), l == DEPTH - 1)
    return h.reshape(BATCH, SEQ, D_MODEL)
```

```python
import functools
import math

import jax
import jax.numpy as jnp
from jax import lax
from jax.experimental import pallas as pl
from jax.experimental.pallas import tpu as pltpu

F32 = jnp.float32
BF16 = jnp.bfloat16

D_MODEL = 1024
BATCH = 4
SEQ = 4096
DEPTH = 2
TOKENS = BATCH * SEQ
D_SSM = 512
D_ATTN = 512
SSM_P = 16
SSM_G = 32
SSM_N = 64
HEAD_DIM = 64
N_HEADS = 8
D_FF = 4096
EPS = 1e-6
LOG2E = 1.4426950408889634

LANES = 128
SUBLANES = 8
MXU_N = 256
VMEM_LIMIT = 48 * 2 ** 20

SSM_CHUNK = SUBLANES
SSM_GB = LANES // SSM_P
SSM_NB = D_SSM // LANES
SSM_W = SSM_CHUNK * LANES
SSM_S = 2 * SSM_GB * SSM_N
SSM_TM = SEQ
SSM_R = SSM_TM // SSM_CHUNK
SSM_SEG = SSM_R // SUBLANES

TM_IN = 2048
TQ = 512
QW = 2 * TQ
TM_MLP = 512
TF_MLP = 1024
CS_BLK = LANES
GATE_ROWS = 32
V_PAD = 16
NEG_BIG = -1e30

NT_DIMS = (((1,), (1,)), ((), ()))
TN_DIMS = (((0,), (0,)), ((), ()))


def _rms(x, g):
    return x * lax.rsqrt(jnp.mean(x * x, axis=-1, keepdims=True) + EPS) * g


def _dot(a, b):
    return jnp.dot(a, b, preferred_element_type=F32)


def _split3(x):
    hi = x.astype(BF16)
    r1 = x - hi.astype(F32)
    mid = r1.astype(BF16)
    low = (r1 - mid.astype(F32)).astype(BF16)
    return hi, mid, low


def _inproj_kernel(x_ref, g_ref, wu_ref, wk_ref, wqt_ref, wvt_ref, fb_ref,
                   u_ref, k_ref, kb_ref, qt_ref, vt_ref, carry_ref):
    i = pl.program_id(0)

    @pl.when(i % (SEQ // TM_IN) == 0)
    def _():
        carry_ref[...] = jnp.zeros(carry_ref.shape, F32)

    qscale = LOG2E / math.sqrt(HEAD_DIM)
    r = lax.broadcasted_iota(jnp.int32, (CS_BLK, CS_BLK), 0)
    c = lax.broadcasted_iota(jnp.int32, (CS_BLK, CS_BLK), 1)
    triu = jnp.where(r <= c, 1.0, 0.0).astype(BF16)
    row = lax.broadcasted_iota(jnp.int32, (GATE_ROWS, CS_BLK), 0)
    carry = carry_ref[...]
    for t in range(TM_IN // TQ):
        tok = slice(t * TQ, (t + 1) * TQ)
        xn = _rms(x_ref[tok, :], g_ref[...]).astype(BF16)
        u_ref[tok, :] = _dot(xn, wu_ref[...])
        k_ref[tok, :] = _dot(xn, wk_ref[...]).astype(BF16)
        qf = lax.dot_general(wqt_ref[...], xn, NT_DIMS, preferred_element_type=F32)
        qt_ref[t] = (qf[0:D_ATTN] * qscale).astype(BF16)
        vt_ref[t] = lax.dot_general(wvt_ref[...], xn, NT_DIMS, preferred_element_type=F32).astype(BF16)

        for b in range(TQ // CS_BLK):
            f = qf[D_ATTN:D_ATTN + GATE_ROWS, b * CS_BLK:(b + 1) * CS_BLK] + fb_ref[...]
            lf = jnp.minimum(f, 0.0) - jnp.log1p(jnp.exp(-jnp.abs(f)))
            hi, mid, low = _split3(lf)
            local = _dot(hi, triu) + _dot(mid, triu) + _dot(low, triu)
            cs = local + carry
            carry = carry + jnp.broadcast_to(local[:, CS_BLK - 1:CS_BLK], (GATE_ROWS, CS_BLK))
            p0, p1, p2 = _split3(cs * (-LOG2E))
            piece = jnp.where(row < N_HEADS, p0, jnp.where(row < 2 * N_HEADS, p1, p2)).astype(F32)
            piece = jnp.where(row < 3 * N_HEADS, piece, 0.0)
            wide = jnp.concatenate([piece, jnp.zeros((LANES - GATE_ROWS, CS_BLK), F32)], axis=0)
            lo = t * TQ + b * CS_BLK
            kb_ref[lo:lo + CS_BLK, :] = wide.T.astype(BF16)
    carry_ref[...] = carry


def _inproj(h, g, wu, wk, wqt, wvt, fb):
    n = TOKENS // TM_IN
    fixed = lambda i: (0, 0)
    return pl.pallas_call(
        _inproj_kernel,
        grid=(n,),
        in_specs=[
            pl.BlockSpec((TM_IN, D_MODEL), lambda i: (i, 0)),
            pl.BlockSpec((1, D_MODEL), fixed),
            pl.BlockSpec((D_MODEL, D_SSM), fixed),
            pl.BlockSpec((D_MODEL, D_ATTN), fixed),
            pl.BlockSpec((D_ATTN + GATE_ROWS, D_MODEL), fixed),
            pl.BlockSpec((D_ATTN, D_MODEL), fixed),
            pl.BlockSpec((GATE_ROWS, CS_BLK), fixed),
        ],
        out_specs=[
            pl.BlockSpec((TM_IN, D_SSM), lambda i: (i, 0)),
            pl.BlockSpec((TM_IN, D_ATTN), lambda i: (i, 0)),
            pl.BlockSpec((TM_IN, LANES), lambda i: (i, 0)),
            pl.BlockSpec((TM_IN // TQ, D_ATTN, TQ), lambda i: (i, 0, 0)),
            pl.BlockSpec((TM_IN // TQ, D_ATTN, TQ), lambda i: (i, 0, 0)),
        ],
        out_shape=[
            jax.ShapeDtypeStruct((TOKENS, D_SSM), F32),
            jax.ShapeDtypeStruct((TOKENS, D_ATTN), BF16),
            jax.ShapeDtypeStruct((TOKENS, LANES), BF16),
            jax.ShapeDtypeStruct((TOKENS // TQ, D_ATTN, TQ), BF16),
            jax.ShapeDtypeStruct((TOKENS // TQ, D_ATTN, TQ), BF16),
        ],
        scratch_shapes=[pltpu.VMEM((GATE_ROWS, CS_BLK), F32)],
        compiler_params=pltpu.CompilerParams(
            dimension_semantics=("arbitrary",), vmem_limit_bytes=VMEM_LIMIT),
        name="inproj",
    )(h, g, wu, wk, wqt, wvt, fb)


def _ssm_kernel(u_ref, m_ref, ws_ref, wct_ref, a_ref, y_ref, s_ref, x_ref, xs_ref, yl_ref):
    nt = SSM_S // 2 // LANES
    tiles = range(nt)
    z = jnp.concatenate(
        [u_ref[pl.ds(s, SSM_R, stride=SSM_CHUNK), :].astype(BF16) for s in range(SSM_CHUNK)], axis=1)
    s = _dot(z, ws_ref[0])
    seg_rows = lambda g: pl.ds(g, SSM_SEG, stride=SUBLANES)
    for c in range(2 * nt):
        for g in range(SUBLANES):
            s_ref[c, seg_rows(g), :] = s[g * SSM_SEG:(g + 1) * SSM_SEG, c * LANES:(c + 1) * LANES]
    for c in range(SSM_W // MXU_N):
        hi = (c + 1) * MXU_N
        yl_ref[:, c * MXU_N:hi] = _dot(z[:, 0:hi], m_ref[0, 0:hi, c * MXU_N:hi])

    lane_tiles = lambda row: [a_ref[0, row:row + 1, c * LANES:(c + 1) * LANES] for c in tiles]
    full = lambda rows: [jnp.broadcast_to(r, (SUBLANES, LANES)) for r in rows]
    ar, ai = full(lane_tiles(0)), full(lane_tiles(1))
    gr, gi = lane_tiles(2), lane_tiles(3)

    def local(k, carry):
        cr, ci = carry
        rows = pl.ds(pl.multiple_of(k * SUBLANES, SUBLANES), SUBLANES)
        nr, ni = [], []
        for c in tiles:
            x_ref[c, rows, :] = cr[c]
            x_ref[nt + c, rows, :] = ci[c]
            nr.append(ar[c] * cr[c] - ai[c] * ci[c] + s_ref[c, rows, :])
            ni.append(ar[c] * ci[c] + ai[c] * cr[c] + s_ref[nt + c, rows, :])
        return tuple(nr), tuple(ni)

    zero = tuple(jnp.zeros((SUBLANES, LANES), F32) for _ in tiles)
    er, ei = lax.fori_loop(0, SSM_SEG, local, (zero, zero), unroll=8)

    for c in tiles:
        cr = jnp.zeros((1, LANES), F32)
        ci = jnp.zeros((1, LANES), F32)
        for g in range(SUBLANES):
            xs_ref[c, g:g + 1, :] = cr
            xs_ref[nt + c, g:g + 1, :] = ci
            cr, ci = (gr[c] * cr - gi[c] * ci + er[c][g:g + 1], gr[c] * ci + gi[c] * cr + ei[c][g:g + 1])
    xr = [xs_ref[c] for c in tiles]
    xi = [xs_ref[nt + c] for c in tiles]

    def fix(k, carry):
        pr, pi = carry
        rows = pl.ds(pl.multiple_of(k * SUBLANES, SUBLANES), SUBLANES)
        nr, ni = [], []
        for c in tiles:
            x_ref[c, rows, :] += pr[c] * xr[c] - pi[c] * xi[c]
            x_ref[nt + c, rows, :] += pr[c] * xi[c] + pi[c] * xr[c]
            nr.append(pr[c] * ar[c] - pi[c] * ai[c])
            ni.append(pr[c] * ai[c] + pi[c] * ar[c])
        return tuple(nr), tuple(ni)

    one = tuple(jnp.ones((SUBLANES, LANES), F32) for _ in tiles)
    lax.fori_loop(0, SSM_SEG, fix, (one, zero), unroll=8)

    x = jnp.concatenate(
        [jnp.concatenate([x_ref[c, seg_rows(g), :] for g in range(SUBLANES)], axis=0).astype(BF16)
         for c in range(2 * nt)], axis=1)
    y = yl_ref[...] + lax.dot_general(x, wct_ref[0], NT_DIMS, preferred_element_type=F32)
    for t in range(SSM_CHUNK):
        y_ref[pl.ds(t, SSM_R, stride=SSM_CHUNK), :] = y[:, t * LANES:(t + 1) * LANES]


def _ssm(u, m, ws, wct, apow, layer):
    nt = TOKENS // SSM_TM
    blk = lambda j, i: (layer * SSM_NB + j, 0, 0)
    return pl.pallas_call(
        _ssm_kernel,
        grid=(SSM_NB, nt),
        in_specs=[
            pl.BlockSpec((SSM_TM, LANES), lambda j, i: (i, j)),
            pl.BlockSpec((1, SSM_W, SSM_W), blk),
            pl.BlockSpec((1, SSM_W, SSM_S), blk),
            pl.BlockSpec((1, SSM_W, SSM_S), blk),
            pl.BlockSpec((1, 4, SSM_S // 2), blk),
        ],
        out_specs=pl.BlockSpec((SSM_TM, LANES), lambda j, i: (i, j)),
        out_shape=jax.ShapeDtypeStruct((TOKENS, D_SSM), F32),
        scratch_shapes=[pltpu.VMEM((SSM_S // LANES, SSM_R, LANES), F32),
                        pltpu.VMEM((SSM_S // LANES, SSM_R, LANES), F32),
                        pltpu.VMEM((SSM_S // LANES, SUBLANES, LANES), F32),
                        pltpu.VMEM((SSM_R, SSM_W), F32)],
        compiler_params=pltpu.CompilerParams(
            dimension_semantics=("arbitrary", "arbitrary"), vmem_limit_bytes=VMEM_LIMIT),
        name="ssm_scan",
    )(u, m, ws, wct, apow)


def _split2(x):
    hi = x.astype(BF16)
    return hi, (x - hi.astype(F32)).astype(BF16)


def _dot_nt_f32(a, b):
    ah, al = _split2(a)
    bh, bl = _split2(b)
    d = lambda x, y: lax.dot_general(x, y, NT_DIMS, preferred_element_type=F32)
    return d(ah, bh) + d(ah, bl) + d(al, bh)


def _ssm_prep_kernel(lam_ref, b_ref, c_ref, d_ref, m_ref, ws_ref, wct_ref, a_ref):
    half = SSM_S // 2
    tc = SSM_CHUNK
    ldt, lre, lim = lam_ref[0, 0:1, :], lam_ref[0, 1:2, :], lam_ref[0, 2:3, :]
    dt = jnp.exp(ldt)
    tau = lax.broadcasted_iota(jnp.int32, (2 * SUBLANES, half), 0).astype(F32)
    pmag = jnp.exp(tau * (lre * dt))
    p_re = pmag * jnp.cos(tau * (lim * dt))
    p_im = pmag * jnp.sin(tau * (lim * dt))
    a_re, a_im = p_re[1:2], p_im[1:2]
    den = lre * lre + lim * lim
    nr = a_re - 1.0
    s_re = (nr * lre + a_im * lim) / den
    s_im = (a_im * lre - nr * lim) / den
    b_re, b_im = b_ref[0, 0], b_ref[0, 1]
    bb_re = s_re * b_re - s_im * b_im
    bb_im = s_re * b_im + s_im * b_re
    c_re, c_im = c_ref[0, 0], c_ref[0, 1]
    eye = (lax.broadcasted_iota(jnp.int32, (LANES, LANES), 0)
           == lax.broadcasted_iota(jnp.int32, (LANES, LANES), 1))

    m_ref[...] = jnp.zeros(m_ref.shape, BF16)
    for lag in range(tc):
        pr, pi = p_re[lag:lag + 1], p_im[lag:lag + 1]
        l_re = pr * bb_re - pi * bb_im
        l_im = pr * bb_im + pi * bb_re
        rows = slice((tc - 1 - lag) * LANES, (tc - lag) * LANES)
        ws_ref[0, rows, 0:half] = l_re.astype(BF16)
        ws_ref[0, rows, half:SSM_S] = l_im.astype(BF16)
        x = _dot_nt_f32(l_re, c_re) - _dot_nt_f32(l_im, c_im)
        if lag == 0:
            x = x + jnp.where(eye, d_ref[0], 0.0)
        x = x.astype(BF16)
        for s in range(tc - lag):
            m_ref[0, s * LANES:(s + 1) * LANES, (s + lag) * LANES:(s + lag + 1) * LANES] = x
    for t in range(tc):
        pr, pi = p_re[t + 1:t + 2], p_im[t + 1:t + 2]
        rows = slice(t * LANES, (t + 1) * LANES)
        wct_ref[0, rows, 0:half] = (c_re * pr - c_im * pi).astype(BF16)
        wct_ref[0, rows, half:SSM_S] = (-(c_re * pi + c_im * pr)).astype(BF16)

    qr, qi = p_re[tc:tc + 1], p_im[tc:tc + 1]
    a_ref[0, 0:1, :] = qr
    a_ref[0, 1:2, :] = qi
    for _ in range(SSM_SEG.bit_length() - 1):
        qr, qi = qr * qr - qi * qi, 2.0 * qr * qi
    a_ref[0, 2:3, :] = qr
    a_ref[0, 3:4, :] = qi


def _ssm_prep(lam, bexp, cexp, dexp):
    n = lam.shape[0]
    blk = lambda *shape: pl.BlockSpec((1,) + shape, lambda j: (j,) + (0,) * len(shape))
    big = jax.ShapeDtypeStruct((n, SSM_W, SSM_S), BF16)
    return pl.pallas_call(
        _ssm_prep_kernel,
        grid=(n,),
        in_specs=[blk(3, SSM_S // 2), blk(2, LANES, SSM_S // 2), blk(2, LANES, SSM_S // 2), blk(1, LANES)],
        out_specs=[blk(SSM_W, SSM_W), blk(SSM_W, SSM_S), blk(SSM_W, SSM_S), blk(4, SSM_S // 2)],
        out_shape=[big, big, big, jax.ShapeDtypeStruct((n, 4, SSM_S // 2), F32)],
        compiler_params=pltpu.CompilerParams(
            dimension_semantics=("arbitrary",), vmem_limit_bytes=VMEM_LIMIT),
        name="ssm_prep",
    )(lam, bexp, cexp, dexp)


def _ssm_operands(log_dt, lam_re, lam_im, b_re, b_im, c_re, c_im, d_skip):
    nb, gb = SSM_NB, SSM_GB
    n = DEPTH * nb
    eye_g = jnp.eye(gb, dtype=F32)[None, :, None, :, None]
    ldt = jnp.broadcast_to(log_dt[..., None], (DEPTH, SSM_G, SSM_N))
    lam = jnp.stack([ldt, lam_re, lam_im], axis=1).reshape(DEPTH, 3, nb, gb * SSM_N)
    lam = lam.transpose(0, 2, 1, 3).reshape(n, 3, gb * SSM_N)

    def expand(w):
        return (w[:, :, :, None, :] * eye_g).reshape(n, LANES, gb * SSM_N)

    bt = lambda b: b.reshape(n, gb, SSM_N, SSM_P).transpose(0, 1, 3, 2)
    ct = lambda c: c.reshape(n, gb, SSM_P, SSM_N)
    bexp = jnp.stack([expand(bt(b_re)), expand(bt(b_im))], axis=1)
    cexp = jnp.stack([expand(ct(c_re)), expand(ct(c_im))], axis=1)
    return lam, bexp, cexp, d_skip.reshape(n, 1, LANES)


def _attn_kernel(k_ref, kb_ref, qt_ref, vt_ref, o_ref, rhs_ref, sa_ref, sb_ref, xa_ref, xb_ref,
                 acc_ref, m_ref):
    p = pl.program_id(1)
    row = lax.broadcasted_iota(jnp.int32, (LANES, TQ), 0)
    wide_row = lax.broadcasted_iota(jnp.int32, (LANES, QW), 0)
    in_head = []
    for hh in range(2):
        h = 2 * p + hh
        in_head.append((row >= hh * HEAD_DIM) & (row < (hh + 1) * HEAD_DIM))
        pick = (wide_row == h) | (wide_row == h + N_HEADS) | (wide_row == h + 2 * N_HEADS)
        rhs_ref[hh, LANES:2 * LANES, :] = jnp.where(pick, 1.0, 0.0).astype(BF16)

    key = lax.broadcasted_iota(jnp.int32, (TQ, QW), 0)
    qry = lax.broadcasted_iota(jnp.int32, (TQ, QW), 1)
    causal = key <= qry
    ones_row = jnp.where(lax.broadcasted_iota(jnp.int32, (V_PAD, TQ), 0) == 0, 1.0, 0.0).astype(BF16)
    both = (0, 1)
    full = slice(0, QW)
    upper = slice(TQ, QW)

    def scores(j, s_ref, x_ref, heads, lanes=full):
        rows = pl.ds(pl.multiple_of(j * TQ, TQ), TQ)
        lhs = jnp.concatenate([k_ref[rows, :], kb_ref[rows, :]], axis=1)
        for hh in heads:
            st = _dot(lhs, rhs_ref[hh, :, lanes])
            s_ref[hh, :, lanes] = st
            x_ref[hh, :, lanes] = jnp.max(st, axis=0, keepdims=True)

    def update(j, s_ref, x_ref, mask, heads, lanes=full):
        for hh in heads:
            st = s_ref[hh, :, lanes]
            if mask is None:
                blk_max = x_ref[hh, :, lanes]
            else:
                st = jnp.where(mask, st, NEG_BIG)
                blk_max = jnp.max(st, axis=0, keepdims=True)
            m_prev = m_ref[hh, :, lanes]
            m_next = jnp.maximum(m_prev, blk_max)
            alpha = jnp.exp2(m_prev - m_next)
            pt = jnp.exp2(st - m_next).astype(BF16)
            vt = jnp.concatenate([vt_ref[j, hh * HEAD_DIM:(hh + 1) * HEAD_DIM, :], ones_row], axis=0)
            acc_ref[hh, :, lanes] = alpha * acc_ref[hh, :, lanes] + _dot(vt, pt)
            m_ref[hh, :, lanes] = m_next

    def body(jj, carry):
        j = 2 * jj
        for hh in both:
            scores(j + 1, sb_ref, xb_ref, (hh,))
            update(j, sa_ref, xa_ref, None, (hh,))
        for hh in both:
            scores(j + 2, sa_ref, xa_ref, (hh,))
            update(j + 1, sb_ref, xb_ref, None, (hh,))
        return carry

    def tile(i, carry):
        for t in range(QW // TQ):
            qt = qt_ref[(QW // TQ) * i + t]
            for hh in both:
                rhs_ref[hh, 0:LANES, t * TQ:(t + 1) * TQ] = jnp.where(in_head[hh], qt, jnp.zeros_like(qt))
        m_ref[...] = jnp.full(m_ref.shape, NEG_BIG, F32)
        acc_ref[...] = jnp.zeros(acc_ref.shape, F32)

        scores(0, sa_ref, xa_ref, both)
        lax.fori_loop(0, i, body, 0)
        tri = causal[:, 0:TQ]
        for hh in both:
            scores(2 * i + 1, sb_ref, xb_ref, (hh,), upper)
            update(2 * i, sa_ref, xa_ref, tri, (hh,), slice(0, TQ))
            update(2 * i, sa_ref, xa_ref, None, (hh,), upper)
        update(2 * i + 1, sb_ref, xb_ref, tri, both, upper)

        for hh in both:
            acc = acc_ref[hh]
            out = (acc[0:HEAD_DIM] / acc[HEAD_DIM:HEAD_DIM + 1]).astype(BF16)
            for t in range(QW // TQ):
                o_ref[(QW // TQ) * i + t, hh * HEAD_DIM:(hh + 1) * HEAD_DIM, :] = out[:, t * TQ:(t + 1) * TQ]
        return carry

    lax.fori_loop(0, SEQ // QW, tile, 0)


def _attention(k, kb, qt, vt):
    nq = SEQ // TQ
    return pl.pallas_call(
        _attn_kernel,
        grid=(BATCH, N_HEADS // 2),
        in_specs=[
            pl.BlockSpec((SEQ, LANES), lambda b, p: (b, p)),
            pl.BlockSpec((SEQ, LANES), lambda b, p: (b, 0)),
            pl.BlockSpec((nq, LANES, TQ), lambda b, p: (b, p, 0)),
            pl.BlockSpec((nq, LANES, TQ), lambda b, p: (b, p, 0)),
        ],
        out_specs=pl.BlockSpec((nq, LANES, TQ), lambda b, p: (b, p, 0)),
        out_shape=jax.ShapeDtypeStruct((TOKENS // TQ, D_ATTN, TQ), BF16),
        scratch_shapes=[pltpu.VMEM((2, 2 * LANES, QW), BF16),
                        pltpu.VMEM((2, TQ, QW), F32),
                        pltpu.VMEM((2, TQ, QW), F32),
                        pltpu.VMEM((2, 1, QW), F32),
                        pltpu.VMEM((2, 1, QW), F32),
                        pltpu.VMEM((2, HEAD_DIM + V_PAD, QW), F32),
                        pltpu.VMEM((2, 1, QW), F32)],
        compiler_params=pltpu.CompilerParams(
            dimension_semantics=("arbitrary", "arbitrary"), vmem_limit_bytes=VMEM_LIMIT),
        name="fox_attention",
    )(k, kb, qt, vt)


def _mlp_kernel(apply_final, h_ref, ys_ref, oa_ref, gw_ref, gb_ref, gs_ref, ga_ref, wo_ref,
                l2_ref, wu_ref, wd_ref, fg_ref, o_ref):
    g = jax.nn.gelu(ys_ref[...])
    gate = 1.0 / (1.0 + jnp.exp(-(_dot(g.astype(BF16), gw_ref[...]) + gb_ref[...])))
    n_ssm = _rms(g * gate, gs_ref[...]).astype(BF16)
    oa = jnp.concatenate([oa_ref[t] for t in range(TM_MLP // TQ)], axis=1).astype(F32)
    inv = lax.rsqrt(jnp.mean(oa * oa, axis=0, keepdims=True) + EPS)
    n_att = (oa * inv * ga_ref[...]).astype(BF16)
    h1 = (h_ref[...] + _dot(n_ssm, wo_ref[0:D_SSM, :])
          + lax.dot_general(n_att, wo_ref[D_SSM:D_MODEL, :], TN_DIMS, preferred_element_type=F32))
    xn = _rms(h1, l2_ref[...]).astype(BF16)
    o_ref[...] = h1
    for c in range(D_FF // TF_MLP):
        cols = slice(c * TF_MLP, (c + 1) * TF_MLP)
        a = jnp.maximum(_dot(xn, wu_ref[:, cols]), 0.0)
        o_ref[...] += _dot((a * a).astype(BF16), wd_ref[cols, :])
    if apply_final:
        o_ref[...] = _rms(o_ref[...], fg_ref[...])


def _mlp(h, y_ssm, o_att, glu_w, glu_b, gn_s, gn_a, w_out, ln2, w_up, w_down, final_g, apply_final):
    nm = TOKENS // TM_MLP
    row = lambda i: (i, 0)
    fixed = lambda *shape: pl.BlockSpec(shape, lambda i: (0, 0), pipeline_mode=pl.Buffered(1))
    return pl.pallas_call(
        functools.partial(_mlp_kernel, apply_final),
        grid=(nm,),
        in_specs=[
            pl.BlockSpec((TM_MLP, D_MODEL), row),
            pl.BlockSpec((TM_MLP, D_SSM), row),
            pl.BlockSpec((TM_MLP // TQ, D_ATTN, TQ), lambda i: (i, 0, 0)),
            fixed(D_SSM, D_SSM),
            fixed(1, D_SSM),
            fixed(1, D_SSM),
            fixed(D_ATTN, 1),
            fixed(D_MODEL, D_MODEL),
            fixed(1, D_MODEL),
            fixed(D_MODEL, D_FF),
            fixed(D_FF, D_MODEL),
            fixed(1, D_MODEL),
        ],
        out_specs=pl.BlockSpec((TM_MLP, D_MODEL), row),
        out_shape=jax.ShapeDtypeStruct((TOKENS, D_MODEL), F32),
        compiler_params=pltpu.CompilerParams(
            dimension_semantics=("arbitrary",), vmem_limit_bytes=VMEM_LIMIT),
        name="mixer_out_mlp",
    )(h, y_ssm, o_att, glu_w, glu_b, gn_s, gn_a, w_out, ln2, w_up, w_down, final_g)


def kernel(x, ln1_g, w_in, ssm_log_dt, ssm_lambda_re, ssm_lambda_im, ssm_b_re, ssm_b_im, ssm_c_re, ssm_c_im, ssm_d, glu_w, glu_b, fgate_b, gn_ssm_g, gn_attn_g, w_out, ln2_g, w_up, w_down, final_g):
    assert TM_MLP % TQ == 0 and TM_IN % TQ == 0
    h = x.reshape(TOKENS, D_MODEL)
    mats = _ssm_prep(*_ssm_operands(ssm_log_dt, ssm_lambda_re, ssm_lambda_im, ssm_b_re, ssm_b_im,
                                    ssm_c_re, ssm_c_im, ssm_d))
    for l in range(DEPTH):
        w = w_in[l]
        w_u = w[:, :D_SSM].astype(BF16)
        w_gt = w[:, 4 * D_SSM:].T
        w_qt = jnp.concatenate([w[:, D_SSM:D_SSM + D_ATTN].T, w_gt, w_gt, w_gt,
                                jnp.zeros((GATE_ROWS - 3 * N_HEADS, D_MODEL), F32)], axis=0).astype(BF16)
        w_k = w[:, D_SSM + D_ATTN:D_SSM + 2 * D_ATTN].astype(BF16)
        w_vt = w[:, D_SSM + 2 * D_ATTN:D_SSM + 3 * D_ATTN].T.astype(BF16)
        b_gate = jnp.pad(jnp.concatenate([fgate_b[l]] * 3), (0, GATE_ROWS - 3 * N_HEADS))
        b_gate = jnp.broadcast_to(b_gate[:, None], (GATE_ROWS, CS_BLK))

        u, k, kb, qt, vt = _inproj(h, ln1_g[l].reshape(1, D_MODEL), w_u, w_k, w_qt, w_vt, b_gate)

        y_ssm = _ssm(u, *mats, l)
        o_att = _attention(k, kb, qt, vt)

        h = _mlp(h, y_ssm, o_att, glu_w[l].astype(BF16), glu_b[l].reshape(1, D_SSM),
                 gn_ssm_g[l].reshape(1, D_SSM), gn_attn_g[l].reshape(D_ATTN, 1),
                 w_out[l].astype(BF16), ln2_g[l].reshape(1, D_MODEL), w_up[l].astype(BF16),
                 w_down[l].astype(BF16), final_g.reshape(1, D_MODEL), l == DEPTH - 1)
    return h.reshape(BATCH, SEQ, D_MODEL)
```

```python
import functools
import math

import jax
import jax.numpy as jnp
from jax import lax
from jax.experimental import pallas as pl
from jax.experimental.pallas import tpu as pltpu

F32 = jnp.float32
BF16 = jnp.bfloat16

D_MODEL = 1024
BATCH = 4
SEQ = 4096
DEPTH = 2
TOKENS = BATCH * SEQ
D_SSM = 512
D_ATTN = 512
SSM_P = 16
SSM_G = 32
SSM_N = 64
HEAD_DIM = 64
N_HEADS = 8
D_FF = 4096
EPS = 1e-6
LOG2E = 1.4426950408889634

LANES = 128
SUBLANES = 8
MXU_N = 256
VMEM_LIMIT = 48 * 2 ** 20

SSM_CHUNK = SUBLANES
SSM_GB = LANES // SSM_P
SSM_NB = D_SSM // LANES
SSM_W = SSM_CHUNK * LANES
SSM_S = 2 * SSM_GB * SSM_N
SSM_TM = SEQ
SSM_R = SSM_TM // SSM_CHUNK
SSM_SEG = SSM_R // SUBLANES

TM_IN = 2048
TQ = 512
QW = 2 * TQ
TM_MLP = 512
TF_MLP = 1024
CS_BLK = LANES
GATE_ROWS = 32
V_PAD = 16
NEG_BIG = -1e30

NT_DIMS = (((1,), (1,)), ((), ()))
TN_DIMS = (((0,), (0,)), ((), ()))


def _rms(x, g):
    return x * lax.rsqrt(jnp.mean(x * x, axis=-1, keepdims=True) + EPS) * g


def _dot(a, b):
    return jnp.dot(a, b, preferred_element_type=F32)


def _split3(x):
    hi = x.astype(BF16)
    r1 = x - hi.astype(F32)
    mid = r1.astype(BF16)
    low = (r1 - mid.astype(F32)).astype(BF16)
    return hi, mid, low


def _inproj_kernel(x_ref, g_ref, wu_ref, wk_ref, wqt_ref, wvt_ref, fb_ref,
                   u_ref, k_ref, kb_ref, qt_ref, vt_ref, carry_ref):
    i = pl.program_id(0)

    @pl.when(i % (SEQ // TM_IN) == 0)
    def _():
        carry_ref[...] = jnp.zeros(carry_ref.shape, F32)

    qscale = LOG2E / math.sqrt(HEAD_DIM)
    r = lax.broadcasted_iota(jnp.int32, (CS_BLK, CS_BLK), 0)
    c = lax.broadcasted_iota(jnp.int32, (CS_BLK, CS_BLK), 1)
    triu = jnp.where(r <= c, 1.0, 0.0).astype(BF16)
    row = lax.broadcasted_iota(jnp.int32, (GATE_ROWS, CS_BLK), 0)
    carry = carry_ref[...]
    for t in range(TM_IN // TQ):
        tok = slice(t * TQ, (t + 1) * TQ)
        xn = _rms(x_ref[tok, :], g_ref[...]).astype(BF16)
        u_ref[tok, :] = _dot(xn, wu_ref[...])
        k_ref[tok, :] = _dot(xn, wk_ref[...]).astype(BF16)
        qf = lax.dot_general(wqt_ref[...], xn, NT_DIMS, preferred_element_type=F32)
        qt_ref[t] = (qf[0:D_ATTN] * qscale).astype(BF16)
        vt_ref[t] = lax.dot_general(wvt_ref[...], xn, NT_DIMS, preferred_element_type=F32).astype(BF16)

        for b in range(TQ // CS_BLK):
            f = qf[D_ATTN:D_ATTN + GATE_ROWS, b * CS_BLK:(b + 1) * CS_BLK] + fb_ref[...]
            lf = jnp.minimum(f, 0.0) - jnp.log1p(jnp.exp(-jnp.abs(f)))
            hi, mid, low = _split3(lf)
            local = _dot(hi, triu) + _dot(mid, triu) + _dot(low, triu)
            cs = local + carry
            carry = carry + jnp.broadcast_to(local[:, CS_BLK - 1:CS_BLK], (GATE_ROWS, CS_BLK))
            p0, p1, p2 = _split3(cs * (-LOG2E))
            piece = jnp.where(row < N_HEADS, p0, jnp.where(row < 2 * N_HEADS, p1, p2)).astype(F32)
            piece = jnp.where(row < 3 * N_HEADS, piece, 0.0)
            wide = jnp.concatenate([piece, jnp.zeros((LANES - GATE_ROWS, CS_BLK), F32)], axis=0)
            lo = t * TQ + b * CS_BLK
            kb_ref[lo:lo + CS_BLK, :] = wide.T.astype(BF16)
    carry_ref[...] = carry


def _inproj(h, g, wu, wk, wqt, wvt, fb):
    n = TOKENS // TM_IN
    fixed = lambda i: (0, 0)
    return pl.pallas_call(
        _inproj_kernel,
        grid=(n,),
        in_specs=[
            pl.BlockSpec((TM_IN, D_MODEL), lambda i: (i, 0)),
            pl.BlockSpec((1, D_MODEL), fixed),
            pl.BlockSpec((D_MODEL, D_SSM), fixed),
            pl.BlockSpec((D_MODEL, D_ATTN), fixed),
            pl.BlockSpec((D_ATTN + GATE_ROWS, D_MODEL), fixed),
            pl.BlockSpec((D_ATTN, D_MODEL), fixed),
            pl.BlockSpec((GATE_ROWS, CS_BLK), fixed),
        ],
        out_specs=[
            pl.BlockSpec((TM_IN, D_SSM), lambda i: (i, 0)),
            pl.BlockSpec((TM_IN, D_ATTN), lambda i: (i, 0)),
            pl.BlockSpec((TM_IN, LANES), lambda i: (i, 0)),
            pl.BlockSpec((TM_IN // TQ, D_ATTN, TQ), lambda i: (i, 0, 0)),
            pl.BlockSpec((TM_IN // TQ, D_ATTN, TQ), lambda i: (i, 0, 0)),
        ],
        out_shape=[
            jax.ShapeDtypeStruct((TOKENS, D_SSM), F32),
            jax.ShapeDtypeStruct((TOKENS, D_ATTN), BF16),
            jax.ShapeDtypeStruct((TOKENS, LANES), BF16),
            jax.ShapeDtypeStruct((TOKENS // TQ, D_ATTN, TQ), BF16),
            jax.ShapeDtypeStruct((TOKENS // TQ, D_ATTN, TQ), BF16),
        ],
        scratch_shapes=[pltpu.VMEM((GATE_ROWS, CS_BLK), F32)],
        compiler_params=pltpu.CompilerParams(
            dimension_semantics=("arbitrary",), vmem_limit_bytes=VMEM_LIMIT),
        name="inproj",
    )(h, g, wu, wk, wqt, wvt, fb)


def _ssm_kernel(u_ref, m_ref, ws_ref, wct_ref, a_ref, y_ref, s_ref, x_ref, xs_ref, yl_ref):
    nt = SSM_S // 2 // LANES
    tiles = range(nt)
    z = jnp.concatenate(
        [u_ref[pl.ds(s, SSM_R, stride=SSM_CHUNK), :].astype(BF16) for s in range(SSM_CHUNK)], axis=1)
    s = _dot(z, ws_ref[0])
    seg_rows = lambda g: pl.ds(g, SSM_SEG, stride=SUBLANES)
    for c in range(2 * nt):
        for g in range(SUBLANES):
            s_ref[c, seg_rows(g), :] = s[g * SSM_SEG:(g + 1) * SSM_SEG, c * LANES:(c + 1) * LANES]
    for c in range(SSM_W // MXU_N):
        hi = (c + 1) * MXU_N
        yl_ref[:, c * MXU_N:hi] = _dot(z[:, 0:hi], m_ref[0, 0:hi, c * MXU_N:hi])

    lane_tiles = lambda row: [a_ref[0, row:row + 1, c * LANES:(c + 1) * LANES] for c in tiles]
    full = lambda rows: [jnp.broadcast_to(r, (SUBLANES, LANES)) for r in rows]
    ar, ai = full(lane_tiles(0)), full(lane_tiles(1))
    gr, gi = lane_tiles(2), lane_tiles(3)

    def local(k, carry):
        cr, ci = carry
        rows = pl.ds(pl.multiple_of(k * SUBLANES, SUBLANES), SUBLANES)
        nr, ni = [], []
        for c in tiles:
            x_ref[c, rows, :] = cr[c]
            x_ref[nt + c, rows, :] = ci[c]
            nr.append(ar[c] * cr[c] - ai[c] * ci[c] + s_ref[c, rows, :])
            ni.append(ar[c] * ci[c] + ai[c] * cr[c] + s_ref[nt + c, rows, :])
        return tuple(nr), tuple(ni)

    zero = tuple(jnp.zeros((SUBLANES, LANES), F32) for _ in tiles)
    er, ei = lax.fori_loop(0, SSM_SEG, local, (zero, zero), unroll=8)

    for c in tiles:
        cr = jnp.zeros((1, LANES), F32)
        ci = jnp.zeros((1, LANES), F32)
        for g in range(SUBLANES):
            xs_ref[c, g:g + 1, :] = cr
            xs_ref[nt + c, g:g + 1, :] = ci
            cr, ci = (gr[c] * cr - gi[c] * ci + er[c][g:g + 1], gr[c] * ci + gi[c] * cr + ei[c][g:g + 1])
    xr = tuple(xs_ref[c] for c in tiles)
    xi = tuple(xs_ref[nt + c] for c in tiles)

    def fix(k, carry):
        cr, ci = carry
        rows = pl.ds(pl.multiple_of(k * SUBLANES, SUBLANES), SUBLANES)
        nr, ni = [], []
        for c in tiles:
            x_ref[c, rows, :] += cr[c]
            x_ref[nt + c, rows, :] += ci[c]
            nr.append(ar[c] * cr[c] - ai[c] * ci[c])
            ni.append(ar[c] * ci[c] + ai[c] * cr[c])
        return tuple(nr), tuple(ni)

    lax.fori_loop(0, SSM_SEG, fix, (xr, xi), unroll=8)

    x = jnp.concatenate(
        [jnp.concatenate([x_ref[c, seg_rows(g), :] for g in range(SUBLANES)], axis=0).astype(BF16)
         for c in range(2 * nt)], axis=1)
    y = yl_ref[...] + lax.dot_general(x, wct_ref[0], NT_DIMS, preferred_element_type=F32)
    for t in range(SSM_CHUNK):
        y_ref[pl.ds(t, SSM_R, stride=SSM_CHUNK), :] = y[:, t * LANES:(t + 1) * LANES]


def _ssm(u, m, ws, wct, apow, layer):
    nt = TOKENS // SSM_TM
    blk = lambda j, i: (layer * SSM_NB + j, 0, 0)
    return pl.pallas_call(
        _ssm_kernel,
        grid=(SSM_NB, nt),
        in_specs=[
            pl.BlockSpec((SSM_TM, LANES), lambda j, i: (i, j)),
            pl.BlockSpec((1, SSM_W, SSM_W), blk),
            pl.BlockSpec((1, SSM_W, SSM_S), blk),
            pl.BlockSpec((1, SSM_W, SSM_S), blk),
            pl.BlockSpec((1, 4, SSM_S // 2), blk),
        ],
        out_specs=pl.BlockSpec((SSM_TM, LANES), lambda j, i: (i, j)),
        out_shape=jax.ShapeDtypeStruct((TOKENS, D_SSM), F32),
        scratch_shapes=[pltpu.VMEM((SSM_S // LANES, SSM_R, LANES), F32),
                        pltpu.VMEM((SSM_S // LANES, SSM_R, LANES), F32),
                        pltpu.VMEM((SSM_S // LANES, SUBLANES, LANES), F32),
                        pltpu.VMEM((SSM_R, SSM_W), F32)],
        compiler_params=pltpu.CompilerParams(
            dimension_semantics=("arbitrary", "arbitrary"), vmem_limit_bytes=VMEM_LIMIT),
        name="ssm_scan",
    )(u, m, ws, wct, apow)


def _split2(x):
    hi = x.astype(BF16)
    return hi, (x - hi.astype(F32)).astype(BF16)


def _dot_nt_f32(a, b):
    ah, al = _split2(a)
    bh, bl = _split2(b)
    d = lambda x, y: lax.dot_general(x, y, NT_DIMS, preferred_element_type=F32)
    return d(ah, bh) + d(ah, bl) + d(al, bh)


def _ssm_prep_kernel(lam_ref, b_ref, c_ref, d_ref, m_ref, ws_ref, wct_ref, a_ref):
    half = SSM_S // 2
    tc = SSM_CHUNK
    ldt, lre, lim = lam_ref[0, 0:1, :], lam_ref[0, 1:2, :], lam_ref[0, 2:3, :]
    dt = jnp.exp(ldt)
    tau = lax.broadcasted_iota(jnp.int32, (2 * SUBLANES, half), 0).astype(F32)
    pmag = jnp.exp(tau * (lre * dt))
    p_re = pmag * jnp.cos(tau * (lim * dt))
    p_im = pmag * jnp.sin(tau * (lim * dt))
    a_re, a_im = p_re[1:2], p_im[1:2]
    den = lre * lre + lim * lim
    nr = a_re - 1.0
    s_re = (nr * lre + a_im * lim) / den
    s_im = (a_im * lre - nr * lim) / den
    b_re, b_im = b_ref[0, 0], b_ref[0, 1]
    bb_re = s_re * b_re - s_im * b_im
    bb_im = s_re * b_im + s_im * b_re
    c_re, c_im = c_ref[0, 0], c_ref[0, 1]
    eye = (lax.broadcasted_iota(jnp.int32, (LANES, LANES), 0)
           == lax.broadcasted_iota(jnp.int32, (LANES, LANES), 1))

    m_ref[...] = jnp.zeros(m_ref.shape, BF16)
    for lag in range(tc):
        pr, pi = p_re[lag:lag + 1], p_im[lag:lag + 1]
        l_re = pr * bb_re - pi * bb_im
        l_im = pr * bb_im + pi * bb_re
        rows = slice((tc - 1 - lag) * LANES, (tc - lag) * LANES)
        ws_ref[0, rows, 0:half] = l_re.astype(BF16)
        ws_ref[0, rows, half:SSM_S] = l_im.astype(BF16)
        x = _dot_nt_f32(l_re, c_re) - _dot_nt_f32(l_im, c_im)
        if lag == 0:
            x = x + jnp.where(eye, d_ref[0], 0.0)
        x = x.astype(BF16)
        for s in range(tc - lag):
            m_ref[0, s * LANES:(s + 1) * LANES, (s + lag) * LANES:(s + lag + 1) * LANES] = x
    for t in range(tc):
        pr, pi = p_re[t + 1:t + 2], p_im[t + 1:t + 2]
        rows = slice(t * LANES, (t + 1) * LANES)
        wct_ref[0, rows, 0:half] = (c_re * pr - c_im * pi).astype(BF16)
        wct_ref[0, rows, half:SSM_S] = (-(c_re * pi + c_im * pr)).astype(BF16)

    qr, qi = p_re[tc:tc + 1], p_im[tc:tc + 1]
    a_ref[0, 0:1, :] = qr
    a_ref[0, 1:2, :] = qi
    for _ in range(SSM_SEG.bit_length() - 1):
        qr, qi = qr * qr - qi * qi, 2.0 * qr * qi
    a_ref[0, 2:3, :] = qr
    a_ref[0, 3:4, :] = qi


def _ssm_prep(lam, bexp, cexp, dexp):
    n = lam.shape[0]
    blk = lambda *shape: pl.BlockSpec((1,) + shape, lambda j: (j,) + (0,) * len(shape))
    big = jax.ShapeDtypeStruct((n, SSM_W, SSM_S), BF16)
    return pl.pallas_call(
        _ssm_prep_kernel,
        grid=(n,),
        in_specs=[blk(3, SSM_S // 2), blk(2, LANES, SSM_S // 2), blk(2, LANES, SSM_S // 2), blk(1, LANES)],
        out_specs=[blk(SSM_W, SSM_W), blk(SSM_W, SSM_S), blk(SSM_W, SSM_S), blk(4, SSM_S // 2)],
        out_shape=[big, big, big, jax.ShapeDtypeStruct((n, 4, SSM_S // 2), F32)],
        compiler_params=pltpu.CompilerParams(
            dimension_semantics=("arbitrary",), vmem_limit_bytes=VMEM_LIMIT),
        name="ssm_prep",
    )(lam, bexp, cexp, dexp)


def _ssm_operands(log_dt, lam_re, lam_im, b_re, b_im, c_re, c_im, d_skip):
    nb, gb = SSM_NB, SSM_GB
    n = DEPTH * nb
    eye_g = jnp.eye(gb, dtype=F32)[None, :, None, :, None]
    ldt = jnp.broadcast_to(log_dt[..., None], (DEPTH, SSM_G, SSM_N))
    lam = jnp.stack([ldt, lam_re, lam_im], axis=1).reshape(DEPTH, 3, nb, gb * SSM_N)
    lam = lam.transpose(0, 2, 1, 3).reshape(n, 3, gb * SSM_N)

    def expand(w):
        return (w[:, :, :, None, :] * eye_g).reshape(n, LANES, gb * SSM_N)

    bt = lambda b: b.reshape(n, gb, SSM_N, SSM_P).transpose(0, 1, 3, 2)
    ct = lambda c: c.reshape(n, gb, SSM_P, SSM_N)
    bexp = jnp.stack([expand(bt(b_re)), expand(bt(b_im))], axis=1)
    cexp = jnp.stack([expand(ct(c_re)), expand(ct(c_im))], axis=1)
    return lam, bexp, cexp, d_skip.reshape(n, 1, LANES)


def _attn_kernel(k_ref, kb_ref, qt_ref, vt_ref, o_ref, rhs_ref, sa_ref, sb_ref, xa_ref, xb_ref,
                 acc_ref, m_ref):
    p = pl.program_id(1)
    row = lax.broadcasted_iota(jnp.int32, (LANES, TQ), 0)
    wide_row = lax.broadcasted_iota(jnp.int32, (LANES, QW), 0)
    in_head = []
    for hh in range(2):
        h = 2 * p + hh
        in_head.append((row >= hh * HEAD_DIM) & (row < (hh + 1) * HEAD_DIM))
        pick = (wide_row == h) | (wide_row == h + N_HEADS) | (wide_row == h + 2 * N_HEADS)
        rhs_ref[hh, LANES:2 * LANES, :] = jnp.where(pick, 1.0, 0.0).astype(BF16)

    key = lax.broadcasted_iota(jnp.int32, (TQ, QW), 0)
    qry = lax.broadcasted_iota(jnp.int32, (TQ, QW), 1)
    causal = key <= qry
    ones_row = jnp.where(lax.broadcasted_iota(jnp.int32, (V_PAD, TQ), 0) == 0, 1.0, 0.0).astype(BF16)
    both = (0, 1)
    full = slice(0, QW)
    upper = slice(TQ, QW)

    def scores(j, s_ref, x_ref, heads, lanes=full):
        rows = pl.ds(pl.multiple_of(j * TQ, TQ), TQ)
        lhs = jnp.concatenate([k_ref[rows, :], kb_ref[rows, :]], axis=1)
        for hh in heads:
            st = _dot(lhs, rhs_ref[hh, :, lanes])
            s_ref[hh, :, lanes] = st
            x_ref[hh, :, lanes] = jnp.max(st, axis=0, keepdims=True)

    def update(j, s_ref, x_ref, mask, heads, lanes=full):
        for hh in heads:
            st = s_ref[hh, :, lanes]
            if mask is None:
                blk_max = x_ref[hh, :, lanes]
            else:
                st = jnp.where(mask, st, NEG_BIG)
                blk_max = jnp.max(st, axis=0, keepdims=True)
            m_prev = m_ref[hh, :, lanes]
            m_next = jnp.maximum(m_prev, blk_max)
            alpha = jnp.exp2(m_prev - m_next)
            pt = jnp.exp2(st - m_next).astype(BF16)
            vt = jnp.concatenate([vt_ref[j, hh * HEAD_DIM:(hh + 1) * HEAD_DIM, :], ones_row], axis=0)
            acc_ref[hh, :, lanes] = alpha * acc_ref[hh, :, lanes] + _dot(vt, pt)
            m_ref[hh, :, lanes] = m_next

    def body(jj, carry):
        j = 2 * jj
        for hh in both:
            scores(j + 1, sb_ref, xb_ref, (hh,))
            update(j, sa_ref, xa_ref, None, (hh,))
        for hh in both:
            scores(j + 2, sa_ref, xa_ref, (hh,))
            update(j + 1, sb_ref, xb_ref, None, (hh,))
        return carry

    def tile(i, carry):
        for t in range(QW // TQ):
            qt = qt_ref[(QW // TQ) * i + t]
            for hh in both:
                rhs_ref[hh, 0:LANES, t * TQ:(t + 1) * TQ] = jnp.where(in_head[hh], qt, jnp.zeros_like(qt))
        m_ref[...] = jnp.full(m_ref.shape, NEG_BIG, F32)
        acc_ref[...] = jnp.zeros(acc_ref.shape, F32)

        scores(0, sa_ref, xa_ref, both)
        lax.fori_loop(0, i, body, 0)
        tri = causal[:, 0:TQ]
        for hh in both:
            scores(2 * i + 1, sb_ref, xb_ref, (hh,), upper)
            update(2 * i, sa_ref, xa_ref, tri, (hh,), slice(0, TQ))
            update(2 * i, sa_ref, xa_ref, None, (hh,), upper)
        update(2 * i + 1, sb_ref, xb_ref, tri, both, upper)

        for hh in both:
            acc = acc_ref[hh]
            out = (acc[0:HEAD_DIM] / acc[HEAD_DIM:HEAD_DIM + 1]).astype(BF16)
            for t in range(QW // TQ):
                o_ref[(QW // TQ) * i + t, hh * HEAD_DIM:(hh + 1) * HEAD_DIM, :] = out[:, t * TQ:(t + 1) * TQ]
        return carry

    lax.fori_loop(0, SEQ // QW, tile, 0)


def _attention(k, kb, qt, vt):
    nq = SEQ // TQ
    return pl.pallas_call(
        _attn_kernel,
        grid=(BATCH, N_HEADS // 2),
        in_specs=[
            pl.BlockSpec((SEQ, LANES), lambda b, p: (b, p)),
            pl.BlockSpec((SEQ, LANES), lambda b, p: (b, 0)),
            pl.BlockSpec((nq, LANES, TQ), lambda b, p: (b, p, 0)),
            pl.BlockSpec((nq, LANES, TQ), lambda b, p: (b, p, 0)),
        ],
        out_specs=pl.BlockSpec((nq, LANES, TQ), lambda b, p: (b, p, 0)),
        out_shape=jax.ShapeDtypeStruct((TOKENS // TQ, D_ATTN, TQ), BF16),
        scratch_shapes=[pltpu.VMEM((2, 2 * LANES, QW), BF16),
                        pltpu.VMEM((2, TQ, QW), F32),
                        pltpu.VMEM((2, TQ, QW), F32),
                        pltpu.VMEM((2, 1, QW), F32),
                        pltpu.VMEM((2, 1, QW), F32),
                        pltpu.VMEM((2, HEAD_DIM + V_PAD, QW), F32),
                        pltpu.VMEM((2, 1, QW), F32)],
        compiler_params=pltpu.CompilerParams(
            dimension_semantics=("arbitrary", "arbitrary"), vmem_limit_bytes=VMEM_LIMIT),
        name="fox_attention",
    )(k, kb, qt, vt)


def _mlp_kernel(apply_final, h_ref, ys_ref, oa_ref, gw_ref, gb_ref, gs_ref, ga_ref, wo_ref,
                l2_ref, wu_ref, wd_ref, fg_ref, o_ref):
    g = jax.nn.gelu(ys_ref[...])
    gate = 1.0 / (1.0 + jnp.exp(-(_dot(g.astype(BF16), gw_ref[...]) + gb_ref[...])))
    n_ssm = _rms(g * gate, gs_ref[...]).astype(BF16)
    oa = jnp.concatenate([oa_ref[t] for t in range(TM_MLP // TQ)], axis=1).astype(F32)
    inv = lax.rsqrt(jnp.mean(oa * oa, axis=0, keepdims=True) + EPS)
    n_att = (oa * inv * ga_ref[...]).astype(BF16)
    h1 = (h_ref[...] + _dot(n_ssm, wo_ref[0:D_SSM, :])
          + lax.dot_general(n_att, wo_ref[D_SSM:D_MODEL, :], TN_DIMS, preferred_element_type=F32))
    xn = _rms(h1, l2_ref[...]).astype(BF16)
    o_ref[...] = h1
    for c in range(D_FF // TF_MLP):
        cols = slice(c * TF_MLP, (c + 1) * TF_MLP)
        a = jnp.maximum(_dot(xn, wu_ref[:, cols]), 0.0)
        o_ref[...] += _dot((a * a).astype(BF16), wd_ref[cols, :])
    if apply_final:
        o_ref[...] = _rms(o_ref[...], fg_ref[...])


def _mlp(h, y_ssm, o_att, glu_w, glu_b, gn_s, gn_a, w_out, ln2, w_up, w_down, final_g, apply_final):
    nm = TOKENS // TM_MLP
    row = lambda i: (i, 0)
    fixed = lambda *shape: pl.BlockSpec(shape, lambda i: (0, 0), pipeline_mode=pl.Buffered(1))
    return pl.pallas_call(
        functools.partial(_mlp_kernel, apply_final),
        grid=(nm,),
        in_specs=[
            pl.BlockSpec((TM_MLP, D_MODEL), row),
            pl.BlockSpec((TM_MLP, D_SSM), row),
            pl.BlockSpec((TM_MLP // TQ, D_ATTN, TQ), lambda i: (i, 0, 0)),
            fixed(D_SSM, D_SSM),
            fixed(1, D_SSM),
            fixed(1, D_SSM),
            fixed(D_ATTN, 1),
            fixed(D_MODEL, D_MODEL),
            fixed(1, D_MODEL),
            fixed(D_MODEL, D_FF),
            fixed(D_FF, D_MODEL),
            fixed(1, D_MODEL),
        ],
        out_specs=pl.BlockSpec((TM_MLP, D_MODEL), row),
        out_shape=jax.ShapeDtypeStruct((TOKENS, D_MODEL), F32),
        compiler_params=pltpu.CompilerParams(
            dimension_semantics=("arbitrary",), vmem_limit_bytes=VMEM_LIMIT),
        name="mixer_out_mlp",
    )(h, y_ssm, o_att, glu_w, glu_b, gn_s, gn_a, w_out, ln2, w_up, w_down, final_g)


def kernel(x, ln1_g, w_in, ssm_log_dt, ssm_lambda_re, ssm_lambda_im, ssm_b_re, ssm_b_im, ssm_c_re, ssm_c_im, ssm_d, glu_w, glu_b, fgate_b, gn_ssm_g, gn_attn_g, w_out, ln2_g, w_up, w_down, final_g):
    assert TM_MLP % TQ == 0 and TM_IN % TQ == 0
    h = x.reshape(TOKENS, D_MODEL)
    mats = _ssm_prep(*_ssm_operands(ssm_log_dt, ssm_lambda_re, ssm_lambda_im, ssm_b_re, ssm_b_im,
                                    ssm_c_re, ssm_c_im, ssm_d))
    for l in range(DEPTH):
        w = w_in[l]
        w_u = w[:, :D_SSM].astype(BF16)
        w_gt = w[:, 4 * D_SSM:].T
        w_qt = jnp.concatenate([w[:, D_SSM:D_SSM + D_ATTN].T, w_gt, w_gt, w_gt,
                                jnp.zeros((GATE_ROWS - 3 * N_HEADS, D_MODEL), F32)], axis=0).astype(BF16)
        w_k = w[:, D_SSM + D_ATTN:D_SSM + 2 * D_ATTN].astype(BF16)
        w_vt = w[:, D_SSM + 2 * D_ATTN:D_SSM + 3 * D_ATTN].T.astype(BF16)
        b_gate = jnp.pad(jnp.concatenate([fgate_b[l]] * 3), (0, GATE_ROWS - 3 * N_HEADS))
        b_gate = jnp.broadcast_to(b_gate[:, None], (GATE_ROWS, CS_BLK))

        u, k, kb, qt, vt = _inproj(h, ln1_g[l].reshape(1, D_MODEL), w_u, w_k, w_qt, w_vt, b_gate)

        y_ssm = _ssm(u, *mats, l)
        o_att = _attention(k, kb, qt, vt)

        h = _mlp(h, y_ssm, o_att, glu_w[l].astype(BF16), glu_b[l].reshape(1, D_SSM),
                 gn_ssm_g[l].reshape(1, D_SSM), gn_attn_g[l].reshape(D_ATTN, 1),
                 w_out[l].astype(BF16), ln2_g[l].reshape(1, D_MODEL), w_up[l].astype(BF16),
                 w_down[l].astype(BF16), final_g.reshape(1, D_MODEL), l == DEPTH - 1)
    return h.reshape(BATCH, SEQ, D_MODEL)
```

```python
import functools
import math

import jax
import jax.numpy as jnp
from jax import lax
from jax.experimental import pallas as pl
from jax.experimental.pallas import tpu as pltpu

F32 = jnp.float32
BF16 = jnp.bfloat16

D_MODEL = 1024
BATCH = 4
SEQ = 4096
DEPTH = 2
TOKENS = BATCH * SEQ
D_SSM = 512
D_ATTN = 512
SSM_P = 16
SSM_G = 32
SSM_N = 64
HEAD_DIM = 64
N_HEADS = 8
D_FF = 4096
EPS = 1e-6
LOG2E = 1.4426950408889634

LANES = 128
SUBLANES = 8
MXU_N = 256
VMEM_LIMIT = 48 * 2 ** 20

SSM_CHUNK = SUBLANES
SSM_GB = LANES // SSM_P
SSM_NB = D_SSM // LANES
SSM_W = SSM_CHUNK * LANES
SSM_S = 2 * SSM_GB * SSM_N
SSM_TM = SEQ
SSM_R = SSM_TM // SSM_CHUNK
SSM_SEG = SSM_R // SUBLANES

TM_IN = 2048
TQ = 512
QW = 2 * TQ
TM_MLP = 512
TF_MLP = 1024
CS_BLK = LANES
GATE_ROWS = 32
V_PAD = 16
NEG_BIG = -1e30

NT_DIMS = (((1,), (1,)), ((), ()))
TN_DIMS = (((0,), (0,)), ((), ()))


def _rms(x, g):
    return x * lax.rsqrt(jnp.mean(x * x, axis=-1, keepdims=True) + EPS) * g


def _dot(a, b):
    return jnp.dot(a, b, preferred_element_type=F32)


def _split3(x):
    hi = x.astype(BF16)
    r1 = x - hi.astype(F32)
    mid = r1.astype(BF16)
    low = (r1 - mid.astype(F32)).astype(BF16)
    return hi, mid, low


def _inproj_kernel(x_ref, g_ref, wu_ref, wk_ref, wqt_ref, wvt_ref, fb_ref,
                   u_ref, k_ref, kb_ref, qt_ref, vt_ref, carry_ref):
    i = pl.program_id(0)

    @pl.when(i % (SEQ // TM_IN) == 0)
    def _():
        carry_ref[...] = jnp.zeros(carry_ref.shape, F32)

    qscale = LOG2E / math.sqrt(HEAD_DIM)
    r = lax.broadcasted_iota(jnp.int32, (CS_BLK, CS_BLK), 0)
    c = lax.broadcasted_iota(jnp.int32, (CS_BLK, CS_BLK), 1)
    triu = jnp.where(r <= c, 1.0, 0.0).astype(BF16)
    row = lax.broadcasted_iota(jnp.int32, (GATE_ROWS, CS_BLK), 0)
    carry = carry_ref[...]
    for t in range(TM_IN // TQ):
        tok = slice(t * TQ, (t + 1) * TQ)
        xn = _rms(x_ref[tok, :], g_ref[...]).astype(BF16)
        u_ref[tok, :] = _dot(xn, wu_ref[...])
        k_ref[tok, :] = _dot(xn, wk_ref[...]).astype(BF16)
        qf = lax.dot_general(wqt_ref[...], xn, NT_DIMS, preferred_element_type=F32)
        qt_ref[t] = (qf[0:D_ATTN] * qscale).astype(BF16)
        vt_ref[t] = lax.dot_general(wvt_ref[...], xn, NT_DIMS, preferred_element_type=F32).astype(BF16)

        for b in range(TQ // CS_BLK):
            f = qf[D_ATTN:D_ATTN + GATE_ROWS, b * CS_BLK:(b + 1) * CS_BLK] + fb_ref[...]
            lf = jnp.minimum(f, 0.0) - jnp.log1p(jnp.exp(-jnp.abs(f)))
            hi, mid, low = _split3(lf)
            local = _dot(hi, triu) + _dot(mid, triu) + _dot(low, triu)
            cs = local + carry
            carry = carry + jnp.broadcast_to(local[:, CS_BLK - 1:CS_BLK], (GATE_ROWS, CS_BLK))
            p0, p1, p2 = _split3(cs * (-LOG2E))
            piece = jnp.where(row < N_HEADS, p0, jnp.where(row < 2 * N_HEADS, p1, p2)).astype(F32)
            piece = jnp.where(row < 3 * N_HEADS, piece, 0.0)
            wide = jnp.concatenate([piece, jnp.zeros((LANES - GATE_ROWS, CS_BLK), F32)], axis=0)
            lo = t * TQ + b * CS_BLK
            kb_ref[lo:lo + CS_BLK, :] = wide.T.astype(BF16)
    carry_ref[...] = carry


def _inproj(h, g, wu, wk, wqt, wvt, fb):
    n = TOKENS // TM_IN
    fixed = lambda i: (0, 0)
    return pl.pallas_call(
        _inproj_kernel,
        grid=(n,),
        in_specs=[
            pl.BlockSpec((TM_IN, D_MODEL), lambda i: (i, 0)),
            pl.BlockSpec((1, D_MODEL), fixed),
            pl.BlockSpec((D_MODEL, D_SSM), fixed),
            pl.BlockSpec((D_MODEL, D_ATTN), fixed),
            pl.BlockSpec((D_ATTN + GATE_ROWS, D_MODEL), fixed),
            pl.BlockSpec((D_ATTN, D_MODEL), fixed),
            pl.BlockSpec((GATE_ROWS, CS_BLK), fixed),
        ],
        out_specs=[
            pl.BlockSpec((TM_IN, D_SSM), lambda i: (i, 0)),
            pl.BlockSpec((TM_IN, D_ATTN), lambda i: (i, 0)),
            pl.BlockSpec((TM_IN, LANES), lambda i: (i, 0)),
            pl.BlockSpec((TM_IN // TQ, D_ATTN, TQ), lambda i: (i, 0, 0)),
            pl.BlockSpec((TM_IN // TQ, D_ATTN, TQ), lambda i: (i, 0, 0)),
        ],
        out_shape=[
            jax.ShapeDtypeStruct((TOKENS, D_SSM), F32),
            jax.ShapeDtypeStruct((TOKENS, D_ATTN), BF16),
            jax.ShapeDtypeStruct((TOKENS, LANES), BF16),
            jax.ShapeDtypeStruct((TOKENS // TQ, D_ATTN, TQ), BF16),
            jax.ShapeDtypeStruct((TOKENS // TQ, D_ATTN, TQ), BF16),
        ],
        scratch_shapes=[pltpu.VMEM((GATE_ROWS, CS_BLK), F32)],
        compiler_params=pltpu.CompilerParams(
            dimension_semantics=("arbitrary",), vmem_limit_bytes=VMEM_LIMIT),
        name="inproj",
    )(h, g, wu, wk, wqt, wvt, fb)


def _ssm_kernel(u_ref, m_ref, ws_ref, wct_ref, a_ref, y_ref, s_ref, x_ref, xs_ref, yl_ref):
    nt = SSM_S // 2 // LANES
    tiles = range(nt)
    z = jnp.concatenate(
        [u_ref[pl.ds(s, SSM_R, stride=SSM_CHUNK), :].astype(BF16) for s in range(SSM_CHUNK)], axis=1)
    s = _dot(z, ws_ref[0])
    seg_rows = lambda g: pl.ds(g, SSM_SEG, stride=SUBLANES)
    for c in range(2 * nt):
        for g in range(SUBLANES):
            s_ref[c, seg_rows(g), :] = s[g * SSM_SEG:(g + 1) * SSM_SEG, c * LANES:(c + 1) * LANES]
    for c in range(SSM_W // MXU_N):
        hi = (c + 1) * MXU_N
        yl_ref[:, c * MXU_N:hi] = _dot(z[:, 0:hi], m_ref[0, 0:hi, c * MXU_N:hi])

    lane_tiles = lambda row: [a_ref[0, row:row + 1, c * LANES:(c + 1) * LANES] for c in tiles]
    full = lambda rows: [jnp.broadcast_to(r, (SUBLANES, LANES)) for r in rows]
    ar, ai = full(lane_tiles(0)), full(lane_tiles(1))
    gr, gi = lane_tiles(2), lane_tiles(3)

    def local(k, carry):
        cr, ci = carry
        rows = pl.ds(pl.multiple_of(k * SUBLANES, SUBLANES), SUBLANES)
        nr, ni = [], []
        for c in tiles:
            x_ref[c, rows, :] = cr[c]
            x_ref[nt + c, rows, :] = ci[c]
            nr.append(ar[c] * cr[c] - ai[c] * ci[c] + s_ref[c, rows, :])
            ni.append(ar[c] * ci[c] + ai[c] * cr[c] + s_ref[nt + c, rows, :])
        return tuple(nr), tuple(ni)

    zero = tuple(jnp.zeros((SUBLANES, LANES), F32) for _ in tiles)
    er, ei = lax.fori_loop(0, SSM_SEG, local, (zero, zero), unroll=8)

    for c in tiles:
        cr = jnp.zeros((1, LANES), F32)
        ci = jnp.zeros((1, LANES), F32)
        for g in range(SUBLANES):
            xs_ref[c, g:g + 1, :] = cr
            xs_ref[nt + c, g:g + 1, :] = ci
            cr, ci = (gr[c] * cr - gi[c] * ci + er[c][g:g + 1], gr[c] * ci + gi[c] * cr + ei[c][g:g + 1])
    xr = tuple(xs_ref[c] for c in tiles)
    xi = tuple(xs_ref[nt + c] for c in tiles)

    def fix(k, carry):
        cr, ci = carry
        rows = pl.ds(pl.multiple_of(k * SUBLANES, SUBLANES), SUBLANES)
        nr, ni = [], []
        for c in tiles:
            x_ref[c, rows, :] += cr[c]
            x_ref[nt + c, rows, :] += ci[c]
            nr.append(ar[c] * cr[c] - ai[c] * ci[c])
            ni.append(ar[c] * ci[c] + ai[c] * cr[c])
        return tuple(nr), tuple(ni)

    lax.fori_loop(0, SSM_SEG, fix, (xr, xi), unroll=8)

    x = jnp.concatenate(
        [jnp.concatenate([x_ref[c, seg_rows(g), :] for g in range(SUBLANES)], axis=0).astype(BF16)
         for c in range(2 * nt)], axis=1)
    y = yl_ref[...] + lax.dot_general(x, wct_ref[0], NT_DIMS, preferred_element_type=F32)
    for t in range(SSM_CHUNK):
        y_ref[pl.ds(t, SSM_R, stride=SSM_CHUNK), :] = y[:, t * LANES:(t + 1) * LANES]


def _ssm(u, m, ws, wct, apow, layer):
    nt = TOKENS // SSM_TM
    blk = lambda j, i: (layer * SSM_NB + j, 0, 0)
    return pl.pallas_call(
        _ssm_kernel,
        grid=(SSM_NB, nt),
        in_specs=[
            pl.BlockSpec((SSM_TM, LANES), lambda j, i: (i, j)),
            pl.BlockSpec((1, SSM_W, SSM_W), blk),
            pl.BlockSpec((1, SSM_W, SSM_S), blk),
            pl.BlockSpec((1, SSM_W, SSM_S), blk),
            pl.BlockSpec((1, 4, SSM_S // 2), blk),
        ],
        out_specs=pl.BlockSpec((SSM_TM, LANES), lambda j, i: (i, j)),
        out_shape=jax.ShapeDtypeStruct((TOKENS, D_SSM), F32),
        scratch_shapes=[pltpu.VMEM((SSM_S // LANES, SSM_R, LANES), F32),
                        pltpu.VMEM((SSM_S // LANES, SSM_R, LANES), F32),
                        pltpu.VMEM((SSM_S // LANES, SUBLANES, LANES), F32),
                        pltpu.VMEM((SSM_R, SSM_W), F32)],
        compiler_params=pltpu.CompilerParams(
            dimension_semantics=("arbitrary", "arbitrary"), vmem_limit_bytes=VMEM_LIMIT),
        name="ssm_scan",
    )(u, m, ws, wct, apow)


def _split2(x):
    hi = x.astype(BF16)
    return hi, (x - hi.astype(F32)).astype(BF16)


def _dot_nt_f32(a, b):
    ah, al = _split2(a)
    bh, bl = _split2(b)
    d = lambda x, y: lax.dot_general(x, y, NT_DIMS, preferred_element_type=F32)
    return d(ah, bh) + d(ah, bl) + d(al, bh)


def _ssm_prep_kernel(lam_ref, b_ref, c_ref, d_ref, m_ref, ws_ref, wct_ref, a_ref):
    half = SSM_S // 2
    tc = SSM_CHUNK
    ldt, lre, lim = lam_ref[0, 0:1, :], lam_ref[0, 1:2, :], lam_ref[0, 2:3, :]
    dt = jnp.exp(ldt)
    tau = lax.broadcasted_iota(jnp.int32, (2 * SUBLANES, half), 0).astype(F32)
    pmag = jnp.exp(tau * (lre * dt))
    p_re = pmag * jnp.cos(tau * (lim * dt))
    p_im = pmag * jnp.sin(tau * (lim * dt))
    a_re, a_im = p_re[1:2], p_im[1:2]
    den = lre * lre + lim * lim
    nr = a_re - 1.0
    s_re = (nr * lre + a_im * lim) / den
    s_im = (a_im * lre - nr * lim) / den
    b_re, b_im = b_ref[0, 0], b_ref[0, 1]
    bb_re = s_re * b_re - s_im * b_im
    bb_im = s_re * b_im + s_im * b_re
    c_re, c_im = c_ref[0, 0], c_ref[0, 1]
    eye = (lax.broadcasted_iota(jnp.int32, (LANES, LANES), 0)
           == lax.broadcasted_iota(jnp.int32, (LANES, LANES), 1))

    m_ref[...] = jnp.zeros(m_ref.shape, BF16)
    for lag in range(tc):
        pr, pi = p_re[lag:lag + 1], p_im[lag:lag + 1]
        l_re = pr * bb_re - pi * bb_im
        l_im = pr * bb_im + pi * bb_re
        rows = slice((tc - 1 - lag) * LANES, (tc - lag) * LANES)
        ws_ref[0, rows, 0:half] = l_re.astype(BF16)
        ws_ref[0, rows, half:SSM_S] = l_im.astype(BF16)
        x = _dot_nt_f32(l_re, c_re) - _dot_nt_f32(l_im, c_im)
        if lag == 0:
            x = x + jnp.where(eye, d_ref[0], 0.0)
        x = x.astype(BF16)
        for s in range(tc - lag):
            m_ref[0, s * LANES:(s + 1) * LANES, (s + lag) * LANES:(s + lag + 1) * LANES] = x
    for t in range(tc):
        pr, pi = p_re[t + 1:t + 2], p_im[t + 1:t + 2]
        rows = slice(t * LANES, (t + 1) * LANES)
        wct_ref[0, rows, 0:half] = (c_re * pr - c_im * pi).astype(BF16)
        wct_ref[0, rows, half:SSM_S] = (-(c_re * pi + c_im * pr)).astype(BF16)

    qr, qi = p_re[tc:tc + 1], p_im[tc:tc + 1]
    a_ref[0, 0:1, :] = qr
    a_ref[0, 1:2, :] = qi
    for _ in range(SSM_SEG.bit_length() - 1):
        qr, qi = qr * qr - qi * qi, 2.0 * qr * qi
    a_ref[0, 2:3, :] = qr
    a_ref[0, 3:4, :] = qi


def _ssm_prep(lam, bexp, cexp, dexp):
    n = lam.shape[0]
    blk = lambda *shape: pl.BlockSpec((1,) + shape, lambda j: (j,) + (0,) * len(shape))
    big = jax.ShapeDtypeStruct((n, SSM_W, SSM_S), BF16)
    return pl.pallas_call(
        _ssm_prep_kernel,
        grid=(n,),
        in_specs=[blk(3, SSM_S // 2), blk(2, LANES, SSM_S // 2), blk(2, LANES, SSM_S // 2), blk(1, LANES)],
        out_specs=[blk(SSM_W, SSM_W), blk(SSM_W, SSM_S), blk(SSM_W, SSM_S), blk(4, SSM_S // 2)],
        out_shape=[big, big, big, jax.ShapeDtypeStruct((n, 4, SSM_S // 2), F32)],
        compiler_params=pltpu.CompilerParams(
            dimension_semantics=("arbitrary",), vmem_limit_bytes=VMEM_LIMIT),
        name="ssm_prep",
    )(lam, bexp, cexp, dexp)


def _ssm_operands(log_dt, lam_re, lam_im, b_re, b_im, c_re, c_im, d_skip):
    nb, gb = SSM_NB, SSM_GB
    n = DEPTH * nb
    eye_g = jnp.eye(gb, dtype=F32)[None, :, None, :, None]
    ldt = jnp.broadcast_to(log_dt[..., None], (DEPTH, SSM_G, SSM_N))
    lam = jnp.stack([ldt, lam_re, lam_im], axis=1).reshape(DEPTH, 3, nb, gb * SSM_N)
    lam = lam.transpose(0, 2, 1, 3).reshape(n, 3, gb * SSM_N)

    def expand(w):
        return (w[:, :, :, None, :] * eye_g).reshape(n, LANES, gb * SSM_N)

    bt = lambda b: b.reshape(n, gb, SSM_N, SSM_P).transpose(0, 1, 3, 2)
    ct = lambda c: c.reshape(n, gb, SSM_P, SSM_N)
    bexp = jnp.stack([expand(bt(b_re)), expand(bt(b_im))], axis=1)
    cexp = jnp.stack([expand(ct(c_re)), expand(ct(c_im))], axis=1)
    return lam, bexp, cexp, d_skip.reshape(n, 1, LANES)


def _attn_kernel(k_ref, kb_ref, qt_ref, vt_ref, o_ref, rhs_ref, sa_ref, sb_ref, xa_ref, xb_ref,
                 acc_ref, m_ref):
    p = pl.program_id(1)
    row = lax.broadcasted_iota(jnp.int32, (LANES, TQ), 0)
    wide_row = lax.broadcasted_iota(jnp.int32, (LANES, QW), 0)
    in_head = []
    for hh in range(2):
        h = 2 * p + hh
        in_head.append((row >= hh * HEAD_DIM) & (row < (hh + 1) * HEAD_DIM))
        pick = (wide_row == h) | (wide_row == h + N_HEADS) | (wide_row == h + 2 * N_HEADS)
        rhs_ref[hh, LANES:2 * LANES, :] = jnp.where(pick, 1.0, 0.0).astype(BF16)

    key = lax.broadcasted_iota(jnp.int32, (TQ, QW), 0)
    qry = lax.broadcasted_iota(jnp.int32, (TQ, QW), 1)
    causal = key <= qry
    ones_row = jnp.where(lax.broadcasted_iota(jnp.int32, (V_PAD, TQ), 0) == 0, 1.0, 0.0).astype(BF16)
    both = (0, 1)
    full = slice(0, QW)
    upper = slice(TQ, QW)

    def scores(j, s_ref, x_ref, heads, lanes=full):
        rows = pl.ds(pl.multiple_of(j * TQ, TQ), TQ)
        lhs = jnp.concatenate([k_ref[rows, :], kb_ref[rows, :]], axis=1)
        for hh in heads:
            st = _dot(lhs, rhs_ref[hh, :, lanes])
            s_ref[hh, :, lanes] = st
            x_ref[hh, :, lanes] = jnp.max(st, axis=0, keepdims=True)

    def update(j, s_ref, x_ref, mask, heads, lanes=full):
        for hh in heads:
            st = s_ref[hh, :, lanes]
            if mask is None:
                blk_max = x_ref[hh, :, lanes]
            else:
                st = jnp.where(mask, st, NEG_BIG)
                blk_max = jnp.max(st, axis=0, keepdims=True)
            m_prev = m_ref[hh, :, lanes]
            m_next = jnp.maximum(m_prev, blk_max)
            alpha = jnp.exp2(m_prev - m_next)
            pt = jnp.exp2(st - m_next).astype(BF16)
            vt = jnp.concatenate([vt_ref[j, hh * HEAD_DIM:(hh + 1) * HEAD_DIM, :], ones_row], axis=0)
            acc_ref[hh, :, lanes] = alpha * acc_ref[hh, :, lanes] + _dot(vt, pt)
            m_ref[hh, :, lanes] = m_next

    def body(jj, carry):
        j = 2 * jj
        for hh in both:
            scores(j + 1, sb_ref, xb_ref, (hh,))
            update(j, sa_ref, xa_ref, None, (hh,))
        for hh in both:
            scores(j + 2, sa_ref, xa_ref, (hh,))
            update(j + 1, sb_ref, xb_ref, None, (hh,))
        return carry

    def tile(i):
        for t in range(QW // TQ):
            qt = qt_ref[t]
            for hh in both:
                rhs_ref[hh, 0:LANES, t * TQ:(t + 1) * TQ] = jnp.where(in_head[hh], qt, jnp.zeros_like(qt))
        m_ref[...] = jnp.full(m_ref.shape, NEG_BIG, F32)
        acc_ref[...] = jnp.zeros(acc_ref.shape, F32)

        scores(0, sa_ref, xa_ref, both)
        lax.fori_loop(0, i, body, 0)
        tri = causal[:, 0:TQ]
        for hh in both:
            scores(2 * i + 1, sb_ref, xb_ref, (hh,), upper)
            update(2 * i, sa_ref, xa_ref, tri, (hh,), slice(0, TQ))
            update(2 * i, sa_ref, xa_ref, None, (hh,), upper)
        update(2 * i + 1, sb_ref, xb_ref, tri, both, upper)

        for hh in both:
            acc = acc_ref[hh]
            out = (acc[0:HEAD_DIM] / acc[HEAD_DIM:HEAD_DIM + 1]).astype(BF16)
            for t in range(QW // TQ):
                o_ref[t, hh * HEAD_DIM:(hh + 1) * HEAD_DIM, :] = out[:, t * TQ:(t + 1) * TQ]

    tile(pl.program_id(2))


def _attention(k, kb, qt, vt):
    nq = SEQ // TQ
    nt = SEQ // QW
    return pl.pallas_call(
        _attn_kernel,
        grid=(BATCH, N_HEADS // 2, nt),
        in_specs=[
            pl.BlockSpec((SEQ, LANES), lambda b, p, i: (b, p)),
            pl.BlockSpec((SEQ, LANES), lambda b, p, i: (b, 0)),
            pl.BlockSpec((QW // TQ, LANES, TQ), lambda b, p, i: (b * nt + i, p, 0)),
            pl.BlockSpec((nq, LANES, TQ), lambda b, p, i: (b, p, 0)),
        ],
        out_specs=pl.BlockSpec((QW // TQ, LANES, TQ), lambda b, p, i: (b * nt + i, p, 0)),
        out_shape=jax.ShapeDtypeStruct((TOKENS // TQ, D_ATTN, TQ), BF16),
        scratch_shapes=[pltpu.VMEM((2, 2 * LANES, QW), BF16),
                        pltpu.VMEM((2, TQ, QW), F32),
                        pltpu.VMEM((2, TQ, QW), F32),
                        pltpu.VMEM((2, 1, QW), F32),
                        pltpu.VMEM((2, 1, QW), F32),
                        pltpu.VMEM((2, HEAD_DIM + V_PAD, QW), F32),
                        pltpu.VMEM((2, 1, QW), F32)],
        compiler_params=pltpu.CompilerParams(
            dimension_semantics=("arbitrary", "arbitrary", "arbitrary"), vmem_limit_bytes=VMEM_LIMIT),
        name="fox_attention",
    )(k, kb, qt, vt)


def _mlp_kernel(apply_final, h_ref, ys_ref, oa_ref, gw_ref, gb_ref, gs_ref, ga_ref, wo_ref,
                l2_ref, wu_ref, wd_ref, fg_ref, o_ref):
    g = jax.nn.gelu(ys_ref[...])
    gate = 1.0 / (1.0 + jnp.exp(-(_dot(g.astype(BF16), gw_ref[...]) + gb_ref[...])))
    n_ssm = _rms(g * gate, gs_ref[...]).astype(BF16)
    oa = jnp.concatenate([oa_ref[t] for t in range(TM_MLP // TQ)], axis=1).astype(F32)
    inv = lax.rsqrt(jnp.mean(oa * oa, axis=0, keepdims=True) + EPS)
    n_att = (oa * inv * ga_ref[...]).astype(BF16)
    h1 = (h_ref[...] + _dot(n_ssm, wo_ref[0:D_SSM, :])
          + lax.dot_general(n_att, wo_ref[D_SSM:D_MODEL, :], TN_DIMS, preferred_element_type=F32))
    xn = _rms(h1, l2_ref[...]).astype(BF16)
    o_ref[...] = h1
    for c in range(D_FF // TF_MLP):
        cols = slice(c * TF_MLP, (c + 1) * TF_MLP)
        a = jnp.maximum(_dot(xn, wu_ref[:, cols]), 0.0)
        o_ref[...] += _dot((a * a).astype(BF16), wd_ref[cols, :])
    if apply_final:
        o_ref[...] = _rms(o_ref[...], fg_ref[...])


def _mlp(h, y_ssm, o_att, glu_w, glu_b, gn_s, gn_a, w_out, ln2, w_up, w_down, final_g, apply_final):
    nm = TOKENS // TM_MLP
    row = lambda i: (i, 0)
    fixed = lambda *shape: pl.BlockSpec(shape, lambda i: (0, 0), pipeline_mode=pl.Buffered(1))
    return pl.pallas_call(
        functools.partial(_mlp_kernel, apply_final),
        grid=(nm,),
        in_specs=[
            pl.BlockSpec((TM_MLP, D_MODEL), row),
            pl.BlockSpec((TM_MLP, D_SSM), row),
            pl.BlockSpec((TM_MLP // TQ, D_ATTN, TQ), lambda i: (i, 0, 0)),
            fixed(D_SSM, D_SSM),
            fixed(1, D_SSM),
            fixed(1, D_SSM),
            fixed(D_ATTN, 1),
            fixed(D_MODEL, D_MODEL),
            fixed(1, D_MODEL),
            fixed(D_MODEL, D_FF),
            fixed(D_FF, D_MODEL),
            fixed(1, D_MODEL),
        ],
        out_specs=pl.BlockSpec((TM_MLP, D_MODEL), row),
        out_shape=jax.ShapeDtypeStruct((TOKENS, D_MODEL), F32),
        compiler_params=pltpu.CompilerParams(
            dimension_semantics=("arbitrary",), vmem_limit_bytes=VMEM_LIMIT),
        name="mixer_out_mlp",
    )(h, y_ssm, o_att, glu_w, glu_b, gn_s, gn_a, w_out, ln2, w_up, w_down, final_g)


def kernel(x, ln1_g, w_in, ssm_log_dt, ssm_lambda_re, ssm_lambda_im, ssm_b_re, ssm_b_im, ssm_c_re, ssm_c_im, ssm_d, glu_w, glu_b, fgate_b, gn_ssm_g, gn_attn_g, w_out, ln2_g, w_up, w_down, final_g):
    assert TM_MLP % TQ == 0 and TM_IN % TQ == 0
    h = x.reshape(TOKENS, D_MODEL)
    mats = _ssm_prep(*_ssm_operands(ssm_log_dt, ssm_lambda_re, ssm_lambda_im, ssm_b_re, ssm_b_im,
                                    ssm_c_re, ssm_c_im, ssm_d))
    for l in range(DEPTH):
        w = w_in[l]
        w_u = w[:, :D_SSM].astype(BF16)
        w_gt = w[:, 4 * D_SSM:].T
        w_qt = jnp.concatenate([w[:, D_SSM:D_SSM + D_ATTN].T, w_gt, w_gt, w_gt,
                                jnp.zeros((GATE_ROWS - 3 * N_HEADS, D_MODEL), F32)], axis=0).astype(BF16)
        w_k = w[:, D_SSM + D_ATTN:D_SSM + 2 * D_ATTN].astype(BF16)
        w_vt = w[:, D_SSM + 2 * D_ATTN:D_SSM + 3 * D_ATTN].T.astype(BF16)
        b_gate = jnp.pad(jnp.concatenate([fgate_b[l]] * 3), (0, GATE_ROWS - 3 * N_HEADS))
        b_gate = jnp.broadcast_to(b_gate[:, None], (GATE_ROWS, CS_BLK))

        u, k, kb, qt, vt = _inproj(h, ln1_g[l].reshape(1, D_MODEL), w_u, w_k, w_qt, w_vt, b_gate)

        y_ssm = _ssm(u, *mats, l)
        o_att = _attention(k, kb, qt, vt)

        h = _mlp(h, y_ssm, o_att, glu_w[l].astype(BF16), glu_b[l].reshape(1, D_SSM),
                 gn_ssm_g[l].reshape(1, D_SSM), gn_attn_g[l].reshape(D_ATTN, 1),
                 w_out[l].astype(BF16), ln2_g[l].reshape(1, D_MODEL), w_up[l].astype(BF16),
                 w_down[l].astype(BF16), final_g.reshape(1, D_MODEL), l == DEPTH - 1)
    return h.reshape(BATCH, SEQ, D_MODEL)
```
